```python
import jax, jax.numpy as jnp
from jax import lax
import numpy as np

D_MODEL = 4096
BATCH = 1
SEQ = 8192
DEPTH = 2
DEC_BATCH = 8
DEC_SEQ = 64
PAST_LEN = 4096

CHUNK = 64
WINDOW = 128
WIN_CHUNKS = WINDOW // CHUNK
ATTN_WIDTH = D_MODEL // 2
POOL_WIDTH = D_MODEL - ATTN_WIDTH
HEAD_DIM = 64
N_HEADS = ATTN_WIDTH // HEAD_DIM
N_KV_HEADS = N_HEADS // 8
GROUP = N_HEADS // N_KV_HEADS
KV_WIDTH = N_KV_HEADS * HEAD_DIM
POOL_WINDOWS = (2, 4, 8, 16)
N_POOL_GROUPS = len(POOL_WINDOWS)
POOL_GROUP_WIDTH = POOL_WIDTH // N_POOL_GROUPS
POOL_STATE = max(POOL_WINDOWS) - 1
IN_WIDTH = ATTN_WIDTH + 2 * KV_WIDTH + POOL_WIDTH
N_MEM = 256
MEM_HEADS = 4
MEM_HEAD_DIM = 128
MEM_WIDTH = MEM_HEADS * MEM_HEAD_DIM
D_FF = -(-8 * D_MODEL // (3 * 256)) * 256
EPS = 1e-6
NEG_INF = -1e30

kernel_name = "hybrid_swa_sink_pool_stream_step"


def rms_norm(x, g):
    xf = x.astype(jnp.float32)
    y = xf * lax.rsqrt(jnp.mean(xf * xf, axis=-1, keepdims=True) + EPS)
    return (y * g.astype(jnp.float32)).astype(x.dtype)


def alibi_slopes():
    return 2.0 ** (-8.0 * jnp.arange(1, N_HEADS + 1, dtype=jnp.float32) / N_HEADS)


def chunk_band_mask(q_pos, k_pos):
    qc = q_pos[:, :, None] // CHUNK
    kc = k_pos[:, None, :] // CHUNK
    return (k_pos[:, None, :] >= 0) & (kc <= qc) & (kc >= qc - WIN_CHUNKS)


def sink_attention(q, k, v, q_pos, k_pos, sinks):
    f32 = jnp.float32
    s = jnp.einsum('bnqhgd,bnkhd->bnhgqk', q.astype(f32), k.astype(f32)) * (HEAD_DIM ** -0.5)
    dist = jnp.abs(q_pos[:, :, None] - k_pos[:, None, :]).astype(f32)
    slopes = alibi_slopes().reshape(N_KV_HEADS, GROUP, 1, 1)
    s = s - slopes * dist[None, :, None, None]
    s = jnp.where(chunk_band_mask(q_pos, k_pos)[None, :, None, None], s, NEG_INF)
    sink = sinks.astype(f32).reshape(N_KV_HEADS, GROUP, 1, 1)
    m = jnp.maximum(jnp.max(s, axis=-1, keepdims=True), sink)
    p = jnp.exp(s - m)
    denom = jnp.sum(p, axis=-1, keepdims=True) + jnp.exp(sink - m)
    o = jnp.einsum('bnhgqk,bnkhd->bnqhgd', p / denom, v.astype(f32))
    return o.astype(q.dtype)


def band_blocks(k):
    B, T = k.shape[:2]
    nC = T // CHUNK
    kp = jnp.pad(k, ((0, 0), (WINDOW, 0), (0, 0), (0, 0)))
    kp = kp.reshape(B, nC + WIN_CHUNKS, CHUNK, N_KV_HEADS, HEAD_DIM)
    return jnp.concatenate([kp[:, j:j + nC] for j in range(WIN_CHUNKS + 1)], axis=2)


def split_mixer_input(h, w_in, q_gain, k_gain):
    B, T, _ = h.shape
    z = h @ w_in
    q = rms_norm(z[..., :ATTN_WIDTH].reshape(B, T, N_KV_HEADS, GROUP, HEAD_DIM), q_gain)
    k = rms_norm(z[..., ATTN_WIDTH:ATTN_WIDTH + KV_WIDTH].reshape(B, T, N_KV_HEADS, HEAD_DIM), k_gain)
    v = z[..., ATTN_WIDTH + KV_WIDTH:ATTN_WIDTH + 2 * KV_WIDTH].reshape(B, T, N_KV_HEADS, HEAD_DIM)
    u = z[..., ATTN_WIDTH + 2 * KV_WIDTH:]
    return q, k, v, u


def swa_prompt(q, k, v, sinks):
    B, T = q.shape[:2]
    nC = T // CHUNK
    q_pos = jnp.arange(T, dtype=jnp.int32).reshape(nC, CHUNK)
    k_pos = (jnp.arange(nC, dtype=jnp.int32)[:, None] * CHUNK - WINDOW
             + jnp.arange(WINDOW + CHUNK, dtype=jnp.int32)[None, :])
    o = sink_attention(q.reshape(B, nC, CHUNK, N_KV_HEADS, GROUP, HEAD_DIM),
                       band_blocks(k), band_blocks(v), q_pos, k_pos, sinks)
    return o.reshape(B, T, ATTN_WIDTH), k[:, -WINDOW:], v[:, -WINDOW:]


def swa_sample(q, k, v, ck, cv, sinks):
    B, T = q.shape[:2]
    kf = jnp.concatenate([ck.astype(k.dtype), k], axis=1)
    vf = jnp.concatenate([cv.astype(v.dtype), v], axis=1)
    q_pos = (PAST_LEN + jnp.arange(T, dtype=jnp.int32))[None]
    k_pos = (PAST_LEN - WINDOW + jnp.arange(WINDOW + T, dtype=jnp.int32))[None]
    o = sink_attention(q[:, None], kf[:, None], vf[:, None], q_pos, k_pos, sinks)
    return o.reshape(B, T, ATTN_WIDTH), kf[:, -WINDOW:], vf[:, -WINDOW:]


def multi_scale_pool(ext, pos, w_pool, pool_scale):
    B = ext.shape[0]
    T = pos.shape[0]
    xf = ext.astype(jnp.float32)
    cs = jnp.concatenate([jnp.zeros_like(xf[:, :1]), jnp.cumsum(xf, axis=1)], axis=1)
    end = cs[:, POOL_STATE + 1:]
    cur = xf[:, POOL_STATE:]
    outs = []
    for g, w in enumerate(POOL_WINDOWS):
        sl = slice(g * POOL_GROUP_WIDTH, (g + 1) * POOL_GROUP_WIDTH)
        win_sum = end[..., sl] - cs[:, POOL_STATE + 1 - w:POOL_STATE + 1 - w + T, sl]
        cnt = jnp.minimum(pos + 1, w).astype(jnp.float32)[None, :, None]
        outs.append(win_sum / cnt - cur[..., sl])
    d = jnp.stack(outs, axis=2)
    y = jnp.einsum('btgc,gcd->btgd', d, w_pool.astype(jnp.float32)).reshape(B, T, POOL_WIDTH)
    return (y * pool_scale.astype(jnp.float32)).astype(ext.dtype)


def memory_kv(mem, g_mem, w_k, w_v, k_gain):
    B = mem.shape[0]
    hm = rms_norm(mem, g_mem)
    k = rms_norm((hm @ w_k).reshape(B, N_MEM, MEM_HEADS, MEM_HEAD_DIM), k_gain)
    v = (hm @ w_v).reshape(B, N_MEM, MEM_HEADS, MEM_HEAD_DIM)
    return k, v


def cross_attention(h, mk, mv, w_q, q_gain, w_o):
    B, T, _ = h.shape
    q = rms_norm((h @ w_q).reshape(B, T, MEM_HEADS, MEM_HEAD_DIM), q_gain)
    s = jnp.einsum('bthd,bmhd->bhtm', q.astype(jnp.float32), mk.astype(jnp.float32)) * (MEM_HEAD_DIM ** -0.5)
    p = jax.nn.softmax(s, axis=-1)
    o = jnp.einsum('bhtm,bmhd->bthd', p, mv.astype(jnp.float32)).reshape(B, T, MEM_WIDTH)
    return o.astype(h.dtype) @ w_o


def swiglu(h, w_gate, w_up, w_down):
    return (jax.nn.silu(h @ w_gate) * (h @ w_up)) @ w_down


def setup_inputs(seed: int = 0) -> dict:
    key = jax.random.key(seed)
    ks = jax.random.split(key, 32)

    def nrm(k, shape, scale=1.0):
        return jax.random.normal(k, shape, dtype=jnp.float32) * scale

    def gain(k, shape):
        return 1.0 + 0.02 * jax.random.normal(k, shape, dtype=jnp.float32)

    return {
        "x_prompt": nrm(ks[0], (BATCH, SEQ, D_MODEL)),
        "x_sample": nrm(ks[1], (DEC_BATCH, DEC_SEQ, D_MODEL)),
        "cache_attn_k": nrm(ks[2], (DEPTH, DEC_BATCH, WINDOW, N_KV_HEADS, HEAD_DIM)),
        "cache_attn_v": nrm(ks[3], (DEPTH, DEC_BATCH, WINDOW, N_KV_HEADS, HEAD_DIM)),
        "state_pool": nrm(ks[4], (DEPTH, DEC_BATCH, POOL_STATE, POOL_WIDTH)),
        "cache_mem_k": nrm(ks[5], (DEPTH, DEC_BATCH, N_MEM, MEM_HEADS, MEM_HEAD_DIM)),
        "cache_mem_v": nrm(ks[6], (DEPTH, DEC_BATCH, N_MEM, MEM_HEADS, MEM_HEAD_DIM)),
        "mem_prompt": nrm(ks[7], (BATCH, N_MEM, D_MODEL)),
        "g_mix": gain(ks[8], (DEPTH, D_MODEL)),
        "w_in": nrm(ks[9], (DEPTH, D_MODEL, IN_WIDTH), D_MODEL ** -0.5),
        "q_norm": gain(ks[10], (DEPTH, HEAD_DIM)),
        "k_norm": gain(ks[11], (DEPTH, HEAD_DIM)),
        "attn_sinks": nrm(ks[12], (DEPTH, N_HEADS), 0.5),
        "w_pool": nrm(ks[13], (DEPTH, N_POOL_GROUPS, POOL_GROUP_WIDTH, POOL_GROUP_WIDTH), POOL_GROUP_WIDTH ** -0.5),
        "pool_scale": gain(ks[14], (DEPTH, POOL_WIDTH)),
        "w_out": nrm(ks[15], (DEPTH, D_MODEL, D_MODEL), D_MODEL ** -0.5),
        "g_cross": gain(ks[16], (DEPTH, D_MODEL)),
        "g_mem": gain(ks[17], (DEPTH, D_MODEL)),
        "w_q_mem": nrm(ks[18], (DEPTH, D_MODEL, MEM_WIDTH), D_MODEL ** -0.5),
        "w_k_mem": nrm(ks[19], (DEPTH, D_MODEL, MEM_WIDTH), D_MODEL ** -0.5),
        "w_v_mem": nrm(ks[20], (DEPTH, D_MODEL, MEM_WIDTH), D_MODEL ** -0.5),
        "q_norm_mem": gain(ks[21], (DEPTH, MEM_HEAD_DIM)),
        "k_norm_mem": gain(ks[22], (DEPTH, MEM_HEAD_DIM)),
        "w_o_mem": nrm(ks[23], (DEPTH, MEM_WIDTH, D_MODEL), MEM_WIDTH ** -0.5),
        "g_ffn": gain(ks[24], (DEPTH, D_MODEL)),
        "w_gate": nrm(ks[25], (DEPTH, D_MODEL, D_FF), D_MODEL ** -0.5),
        "w_up": nrm(ks[26], (DEPTH, D_MODEL, D_FF), D_MODEL ** -0.5),
        "w_down": nrm(ks[27], (DEPTH, D_FF, D_MODEL), D_FF ** -0.5),
    }


def reference(x_prompt, x_sample, cache_attn_k, cache_attn_v, state_pool, cache_mem_k, cache_mem_v, mem_prompt,
              g_mix, w_in, q_norm, k_norm, attn_sinks, w_pool, pool_scale, w_out,
              g_cross, g_mem, w_q_mem, w_k_mem, w_v_mem, q_norm_mem, k_norm_mem, w_o_mem,
              g_ffn, w_gate, w_up, w_down):
    T_p = x_prompt.shape[1]
    T_s = x_sample.shape[1]
    pos_p = jnp.arange(T_p, dtype=jnp.int32)
    pos_s = PAST_LEN + jnp.arange(T_s, dtype=jnp.int32)
    xp, xs = x_prompt, x_sample
    ak_p, av_p, pl_p, mk_p, mv_p = [], [], [], [], []
    ak_s, av_s, pl_s = [], [], []
    for l in range(DEPTH):
        q, k, v, u = split_mixer_input(rms_norm(xp, g_mix[l]), w_in[l], q_norm[l], k_norm[l])
        a, nk, nv = swa_prompt(q, k, v, attn_sinks[l])
        ext = jnp.pad(u, ((0, 0), (POOL_STATE, 0), (0, 0)))
        pmix = multi_scale_pool(ext, pos_p, w_pool[l], pool_scale[l])
        xp = xp + jnp.concatenate([a, pmix], axis=-1) @ w_out[l]
        ak_p.append(nk); av_p.append(nv); pl_p.append(u[:, -POOL_STATE:])
        q, k, v, u = split_mixer_input(rms_norm(xs, g_mix[l]), w_in[l], q_norm[l], k_norm[l])
        a, nk, nv = swa_sample(q, k, v, cache_attn_k[l], cache_attn_v[l], attn_sinks[l])
        ext = jnp.concatenate([state_pool[l].astype(u.dtype), u], axis=1)
        pmix = multi_scale_pool(ext, pos_s, w_pool[l], pool_scale[l])
        xs = xs + jnp.concatenate([a, pmix], axis=-1) @ w_out[l]
        ak_s.append(nk); av_s.append(nv); pl_s.append(ext[:, -POOL_STATE:])
        mk, mv = memory_kv(mem_prompt, g_mem[l], w_k_mem[l], w_v_mem[l], k_norm_mem[l])
        mk_p.append(mk); mv_p.append(mv)
        xp = xp + cross_attention(rms_norm(xp, g_cross[l]), mk, mv, w_q_mem[l], q_norm_mem[l], w_o_mem[l])
        xs = xs + cross_attention(rms_norm(xs, g_cross[l]), cache_mem_k[l], cache_mem_v[l],
                                  w_q_mem[l], q_norm_mem[l], w_o_mem[l])
        xp = xp + swiglu(rms_norm(xp, g_ffn[l]), w_gate[l], w_up[l], w_down[l])
        xs = xs + swiglu(rms_norm(xs, g_ffn[l]), w_gate[l], w_up[l], w_down[l])
    return (xp, xs,
            jnp.stack(ak_p), jnp.stack(av_p), jnp.stack(pl_p), jnp.stack(mk_p), jnp.stack(mv_p),
            jnp.stack(ak_s), jnp.stack(av_s), jnp.stack(pl_s))
```

```python
import functools

import jax
import jax.numpy as jnp
from jax import lax
from jax.experimental import pallas as pl
from jax.experimental.pallas import tpu as pltpu

F32 = jnp.float32
BF16 = jnp.bfloat16

D_MODEL = 4096
SEQ = 8192
DEPTH = 2
DEC_BATCH = 8
DEC_SEQ = 64
PAST_LEN = 4096
CHUNK = 64
WINDOW = 128
ATTN_WIDTH = D_MODEL // 2
POOL_WIDTH = D_MODEL - ATTN_WIDTH
HEAD_DIM = 64
N_HEADS = ATTN_WIDTH // HEAD_DIM
N_KV_HEADS = N_HEADS // 8
KV_WIDTH = N_KV_HEADS * HEAD_DIM
POOL_WINDOWS = (2, 4, 8, 16)
POOL_GROUP_WIDTH = POOL_WIDTH // len(POOL_WINDOWS)
POOL_STATE = max(POOL_WINDOWS) - 1
IN_WIDTH = ATTN_WIDTH + 2 * KV_WIDTH + POOL_WIDTH
N_MEM = 256
MEM_HEADS = 4
MEM_HEAD_DIM = 128
MEM_WIDTH = MEM_HEADS * MEM_HEAD_DIM
D_FF = -(-8 * D_MODEL // (3 * 256)) * 256
EPS = 1e-6
NEG_INF = -1e30

M_PROMPT = SEQ
M_SAMPLE = DEC_BATCH * DEC_SEQ
M_ALL = M_PROMPT + M_SAMPLE
N_CHUNKS_PROMPT = M_PROMPT // CHUNK
N_CHUNKS = M_ALL // CHUNK
BAND = WINDOW + CHUNK
BAND_BLOCKS = BAND // CHUNK

Z_Q0, Z_U0 = 0, ATTN_WIDTH
Z_K0 = ATTN_WIDTH + POOL_WIDTH
Z_V0 = Z_K0 + KV_WIDTH

LANES = 128
HALO = 16
VMEM_LIMIT_CAP = 60000 * 1024

TM_IN, TN_IN = 544, 768
TM_OUT, TN_OUT = 1088, 512
TM_FFN, TF_FFN = 544, 512
F_PAD = -(-D_FF // TF_FFN) * TF_FFN
TM_CROSS = 256
TM_KN = 512
TP_POOL = 256
NORM_ROWS = 32


def _params(vmem_bytes, n_axes):
    limit = min(int(vmem_bytes * 1.2) + (6 << 20), VMEM_LIMIT_CAP)
    return pltpu.CompilerParams(dimension_semantics=("arbitrary",) * n_axes, vmem_limit_bytes=limit)


def _rms_rows(x, g):
    return x * lax.rsqrt(jnp.mean(x * x, axis=-1, keepdims=True) + EPS) * g


def _norm_rows_to(x_ref, g_ref, h_ref, n_rows, copy_ref=None):
    def body(r, carry):
        rows = pl.ds(pl.multiple_of(r * NORM_ROWS, NORM_ROWS), NORM_ROWS)
        x = x_ref[rows, :]
        h_ref[rows, :] = _rms_rows(x, g_ref[...]).astype(BF16)
        if copy_ref is not None:
            copy_ref[rows, :] = x
        return carry

    lax.fori_loop(0, n_rows // NORM_ROWS, body, 0)


def _half_lane_rms(x, gain2):
    lo = lax.broadcasted_iota(jnp.int32, (1, LANES), 1) < HEAD_DIM
    x2 = x * x
    ss_lo = jnp.sum(jnp.where(lo, x2, 0.0), axis=-1, keepdims=True)
    ss_hi = jnp.sum(jnp.where(lo, 0.0, x2), axis=-1, keepdims=True)
    r = jnp.where(lo, lax.rsqrt(ss_lo / HEAD_DIM + EPS), lax.rsqrt(ss_hi / HEAD_DIM + EPS))
    return x * r * gain2


def _mixer_in_kernel(x_ref, g_ref, w_ref, z_ref, h_ref):
    @pl.when(pl.program_id(1) == 0)
    def _():
        _norm_rows_to(x_ref, g_ref, h_ref, TM_IN)

    z_ref[...] = jnp.dot(h_ref[...], w_ref[...], preferred_element_type=F32)


def _mixer_in(x, g, w):
    vmem = 2 * TM_IN * D_MODEL * 4 + TM_IN * D_MODEL * 2 + 2 * D_MODEL * TN_IN * 2 + 2 * TM_IN * TN_IN * 4
    return pl.pallas_call(
        _mixer_in_kernel,
        grid=(M_ALL // TM_IN, IN_WIDTH // TN_IN),
        in_specs=[
            pl.BlockSpec((TM_IN, D_MODEL), lambda i, j: (i, 0)),
            pl.BlockSpec((1, D_MODEL), lambda i, j: (0, 0)),
            pl.BlockSpec((D_MODEL, TN_IN), lambda i, j: (0, j)),
        ],
        out_specs=pl.BlockSpec((TM_IN, TN_IN), lambda i, j: (i, j)),
        out_shape=jax.ShapeDtypeStruct((M_ALL, IN_WIDTH), F32),
        scratch_shapes=[pltpu.VMEM((TM_IN, D_MODEL), BF16)],
        compiler_params=_params(vmem, 2),
        name="mixer_in",
    )(x, g, w)


def _k_norm_kernel(k_ref, g_ref, o_ref):
    for t in range(KV_WIDTH // LANES):
        cols = slice(t * LANES, (t + 1) * LANES)
        o_ref[:, cols] = _half_lane_rms(k_ref[:, cols], g_ref[...])


def _k_norm(z, gain2):
    vmem = 4 * TM_KN * KV_WIDTH * 4
    return pl.pallas_call(
        _k_norm_kernel,
        grid=(M_ALL // TM_KN,),
        in_specs=[
            pl.BlockSpec((TM_KN, KV_WIDTH), lambda i: (i, Z_K0 // KV_WIDTH)),
            pl.BlockSpec((1, LANES), lambda i: (0, 0)),
        ],
        out_specs=pl.BlockSpec((TM_KN, KV_WIDTH), lambda i: (i, 0)),
        out_shape=jax.ShapeDtypeStruct((M_ALL, KV_WIDTH), F32),
        compiler_params=_params(vmem, 1),
        name="k_norm",
    )(z, gain2)


PAIRS_PER_KV = N_HEADS // N_KV_HEADS // 2
ROWS_PER_KV = PAIRS_PER_KV * CHUNK
SCORE_COLS = 2 * BAND
TAIL = BAND - LANES


def _paired_band(t, t_rolled, odd_kv):
    lo = lax.broadcasted_iota(jnp.int32, (1, LANES), 1) < HEAD_DIM
    if odd_kv:
        even = jnp.where(lo, t_rolled, 0.0)
        odd = jnp.where(lo, 0.0, t)
    else:
        even = jnp.where(lo, t, 0.0)
        odd = jnp.where(lo, 0.0, t_rolled)
    return jnp.concatenate([even[:LANES], odd[:LANES], even[LANES:], odd[LANES:]], axis=0).astype(BF16)


def _swa_kernel(q_ref, k0_ref, k1_ref, k2_ref, v0_ref, v1_ref, v2_ref,
                qg_ref, bias_ref, sink_e_ref, sink_o_ref, o_ref):
    c = pl.program_id(0)
    lane = lax.broadcasted_iota(jnp.int32, (1, LANES), 1)
    lo = lane < HEAD_DIM

    q = jnp.concatenate([q_ref[:, p * LANES:(p + 1) * LANES] for p in range(N_HEADS // 2)], axis=0)
    qn = _half_lane_rms(q, qg_ref[...]).astype(BF16)

    kb = jnp.concatenate([k0_ref[...], k1_ref[...], k2_ref[...]], axis=0)
    vb = jnp.concatenate([v0_ref[...], v1_ref[...], v2_ref[...]], axis=0)

    col = lax.broadcasted_iota(jnp.int32, (1, SCORE_COLS), 1)
    col_j = jnp.where(col < 2 * LANES, col & (LANES - 1), LANES + ((col - 2 * LANES) & (TAIL - 1)))
    j_min = jnp.where(c < BAND_BLOCKS - 1, WINDOW - CHUNK * c, 0)
    valid = col_j >= j_min

    for kv_tile in range(KV_WIDTH // LANES):
        cols = slice(kv_tile * LANES, (kv_tile + 1) * LANES)
        kt, vt = kb[:, cols], vb[:, cols]
        kt_r = pltpu.roll(kt, HEAD_DIM, axis=1)
        vt_r = pltpu.roll(vt, HEAD_DIM, axis=1)
        for par in range(2):
            hk = 2 * kv_tile + par
            kk = _paired_band(kt, kt_r, par == 1)
            vv = _paired_band(vt, vt_r, par == 1)
            s = lax.dot_general(qn[hk * ROWS_PER_KV:(hk + 1) * ROWS_PER_KV], kk,
                                (((1,), (1,)), ((), ())), preferred_element_type=F32)
            s = jnp.where(valid, s - bias_ref[hk], NEG_INF)
            s0, s1, s2 = s[:, :LANES], s[:, LANES:2 * LANES], s[:, 2 * LANES:]
            sink_e, sink_o = sink_e_ref[hk], sink_o_ref[hk]
            m_e = jnp.maximum(jnp.maximum(jnp.max(s0, axis=-1, keepdims=True),
                                          jnp.max(jnp.where(lo, s2, -jnp.inf), axis=-1, keepdims=True)), sink_e)
            m_o = jnp.maximum(jnp.maximum(jnp.max(s1, axis=-1, keepdims=True),
                                          jnp.max(jnp.where(lo, -jnp.inf, s2), axis=-1, keepdims=True)), sink_o)
            p0 = jnp.exp(s0 - m_e)
            p1 = jnp.exp(s1 - m_o)
            p2 = jnp.exp(s2 - jnp.where(lo, m_e, m_o))
            d_e = (jnp.sum(p0, axis=-1, keepdims=True) + jnp.sum(jnp.where(lo, p2, 0.0), axis=-1, keepdims=True)
                   + jnp.exp(sink_e - m_e))
            d_o = (jnp.sum(p1, axis=-1, keepdims=True) + jnp.sum(jnp.where(lo, 0.0, p2), axis=-1, keepdims=True)
                   + jnp.exp(sink_o - m_o))
            p = jnp.concatenate([p0, p1, p2], axis=1).astype(BF16)
            o = jnp.dot(p, vv, preferred_element_type=F32) / jnp.where(lo, d_e, d_o)
            for pr in range(PAIRS_PER_KV):
                pair = hk * PAIRS_PER_KV + pr
                o_ref[:, pair * LANES:(pair + 1) * LANES] = o[pr * CHUNK:(pr + 1) * CHUNK].astype(BF16)


def _band_block(c, j):
    prompt = jnp.maximum(c - (BAND_BLOCKS - 1) + j, 0)
    sample = N_CHUNKS_PROMPT + (c - N_CHUNKS_PROMPT) * BAND_BLOCKS + j
    return jnp.where(c < N_CHUNKS_PROMPT, prompt, sample)


def _swa(z, k_band, v_band, q_gain2, bias, sink_e, sink_o):
    band_specs = [pl.BlockSpec((CHUNK, KV_WIDTH), functools.partial(lambda c, j: (_band_block(c, j), 0), j=j))
                  for j in range(BAND_BLOCKS)]
    vmem = (2 * CHUNK * ATTN_WIDTH * 4 + 12 * CHUNK * KV_WIDTH * 4 + 2 * bias.size * 4
            + 4 * sink_e.size * LANES * 4 + 2 * CHUNK * ATTN_WIDTH * 2 + (8 << 20))
    return pl.pallas_call(
        _swa_kernel,
        grid=(N_CHUNKS,),
        in_specs=[pl.BlockSpec((CHUNK, ATTN_WIDTH), lambda c: (c, 0))] + band_specs + band_specs + [
            pl.BlockSpec((1, LANES), lambda c: (0, 0)),
            pl.BlockSpec(bias.shape, lambda c: (0, 0, 0)),
            pl.BlockSpec(sink_e.shape, lambda c: (0, 0, 0)),
            pl.BlockSpec(sink_o.shape, lambda c: (0, 0, 0)),
        ],
        out_specs=pl.BlockSpec((CHUNK, ATTN_WIDTH), lambda c: (c, 0)),
        out_shape=jax.ShapeDtypeStruct((M_ALL, ATTN_WIDTH), BF16),
        compiler_params=_params(vmem, 1),
        name="swa",
    )(z, k_band, k_band, k_band, v_band, v_band, v_band, q_gain2, bias, sink_e, sink_o)


def _swa_constants(sinks):
    slopes = 2.0 ** (-8.0 * jnp.arange(1, N_HEADS + 1, dtype=F32) / N_HEADS)
    col = jnp.arange(SCORE_COLS)
    col_j = jnp.where(col < 2 * LANES, col % LANES, LANES + (col - 2 * LANES) % TAIL)
    col_odd = jnp.where(col < 2 * LANES, col // LANES, (col - 2 * LANES) // TAIL)
    t = jnp.arange(CHUNK)
    dist = jnp.abs(t[:, None] + WINDOW - col_j[None, :]).astype(F32)
    head = (jnp.arange(N_KV_HEADS)[:, None, None] * (2 * PAIRS_PER_KV)
            + 2 * jnp.arange(PAIRS_PER_KV)[None, :, None] + col_odd[None, None, :])
    bias = slopes[head][:, :, None, :] * dist[None, None, :, :]
    bias = bias.reshape(N_KV_HEADS, ROWS_PER_KV, SCORE_COLS)
    per_pair = sinks.astype(F32).reshape(N_KV_HEADS, PAIRS_PER_KV, 2)
    rows = jnp.repeat(per_pair, CHUNK, axis=1)
    return bias, rows[:, :, 0:1], rows[:, :, 1:2]


POOL_TILES_PROMPT = M_PROMPT // TP_POOL
BATCH_PER_POOL_TILE = TP_POOL // DEC_SEQ


def _trailing_sum(ext, w):
    s, k = ext, 1
    while k < w:
        s = s + pltpu.roll(s, k, axis=0)
        k *= 2
    return s


def _pool_kernel(u_ref, halo_ref, state_ref, w_ref, scale_ref, o_ref, d_ref):
    i = pl.program_id(0)

    def window_diffs(ext, cur, cnt, w):
        return _trailing_sum(ext, w)[HALO:] / cnt - cur

    @pl.when(i < POOL_TILES_PROMPT)
    def _():
        pos = i * TP_POOL + lax.broadcasted_iota(jnp.int32, (TP_POOL, 1), 0)
        for g, w in enumerate(POOL_WINDOWS):
            cols = slice(g * POOL_GROUP_WIDTH, (g + 1) * POOL_GROUP_WIDTH)
            cur = u_ref[:, cols]
            halo = jnp.where(i > 0, halo_ref[:, cols], 0.0)
            ext = jnp.concatenate([halo, cur], axis=0)
            cnt = jnp.minimum(pos + 1, w).astype(F32)
            d_ref[:, cols] = window_diffs(ext, cur, cnt, w)

    @pl.when(i >= POOL_TILES_PROMPT)
    def _():
        for g, w in enumerate(POOL_WINDOWS):
            cols = slice(g * POOL_GROUP_WIDTH, (g + 1) * POOL_GROUP_WIDTH)
            for b in range(BATCH_PER_POOL_TILE):
                rows = slice(b * DEC_SEQ, (b + 1) * DEC_SEQ)
                cur = u_ref[rows, cols]
                ext = jnp.concatenate([state_ref[b, :, cols], cur], axis=0)
                d_ref[rows, cols] = window_diffs(ext, cur, float(w), w)

    for g in range(len(POOL_WINDOWS)):
        cols = slice(g * POOL_GROUP_WIDTH, (g + 1) * POOL_GROUP_WIDTH)
        y = jnp.dot(d_ref[:, cols].astype(BF16), w_ref[g], preferred_element_type=F32)
        o_ref[:, cols] = (y * scale_ref[:, cols]).astype(BF16)


def _pool(z, state_halo, w_pool, scale):
    halo_blocks_per_tile = TP_POOL // HALO
    vmem = (2 * TP_POOL * POOL_WIDTH * 4 + 2 * HALO * POOL_WIDTH * 4 + 2 * BATCH_PER_POOL_TILE * HALO * POOL_WIDTH * 4
            + 2 * w_pool.size * 2 + 2 * TP_POOL * POOL_WIDTH * 2 + TP_POOL * POOL_WIDTH * 4 + (8 << 20))
    return pl.pallas_call(
        _pool_kernel,
        grid=(M_ALL // TP_POOL,),
        in_specs=[
            pl.BlockSpec((TP_POOL, POOL_WIDTH), lambda i: (i, Z_U0 // POOL_WIDTH)),
            pl.BlockSpec((HALO, POOL_WIDTH),
                         lambda i: (jnp.maximum(i * halo_blocks_per_tile - 1, 0), Z_U0 // POOL_WIDTH)),
            pl.BlockSpec((BATCH_PER_POOL_TILE, HALO, POOL_WIDTH),
                         lambda i: (jnp.maximum(i - POOL_TILES_PROMPT, 0), 0, 0)),
            pl.BlockSpec(w_pool.shape, lambda i: (0, 0, 0)),
            pl.BlockSpec((1, POOL_WIDTH), lambda i: (0, 0)),
        ],
        out_specs=pl.BlockSpec((TP_POOL, POOL_WIDTH), lambda i: (i, 0)),
        out_shape=jax.ShapeDtypeStruct((M_ALL, POOL_WIDTH), BF16),
        scratch_shapes=[pltpu.VMEM((TP_POOL, POOL_WIDTH), F32)],
        compiler_params=_params(vmem, 1),
        name="pool",
    )(z, z, state_halo, w_pool, scale)


def _mixer_out_kernel(a_ref, p_ref, x_ref, wa_ref, wp_ref, o_ref):
    y = jnp.dot(a_ref[...], wa_ref[...], preferred_element_type=F32)
    y = y + jnp.dot(p_ref[...], wp_ref[...], preferred_element_type=F32)
    o_ref[...] = x_ref[...] + y


def _mixer_out(attn, pooled, x, w_out):
    vmem = 2 * (2 * TM_OUT * ATTN_WIDTH * 2 + D_MODEL * TN_OUT * 2 + 2 * TM_OUT * TN_OUT * 4)
    return pl.pallas_call(
        _mixer_out_kernel,
        grid=(M_ALL // TM_OUT, D_MODEL // TN_OUT),
        in_specs=[
            pl.BlockSpec((TM_OUT, ATTN_WIDTH), lambda i, j: (i, 0)),
            pl.BlockSpec((TM_OUT, POOL_WIDTH), lambda i, j: (i, 0)),
            pl.BlockSpec((TM_OUT, TN_OUT), lambda i, j: (i, j)),
            pl.BlockSpec((ATTN_WIDTH, TN_OUT), lambda i, j: (0, j)),
            pl.BlockSpec((POOL_WIDTH, TN_OUT), lambda i, j: (1, j)),
        ],
        out_specs=pl.BlockSpec((TM_OUT, TN_OUT), lambda i, j: (i, j)),
        out_shape=jax.ShapeDtypeStruct((M_ALL, D_MODEL), F32),
        compiler_params=_params(vmem, 2),
        name="mixer_out",
    )(attn, pooled, x, w_out, w_out)


def _mem_kv_kernel(m_ref, g_ref, wk_ref, wv_ref, kg_ref, k_ref, v_ref, h_ref):
    _norm_rows_to(m_ref, g_ref, h_ref, N_MEM)
    h = h_ref[...]
    k = jnp.dot(h, wk_ref[...], preferred_element_type=F32)
    for hd in range(MEM_HEADS):
        cols = slice(hd * MEM_HEAD_DIM, (hd + 1) * MEM_HEAD_DIM)
        k_ref[:, cols] = _rms_rows(k[:, cols], kg_ref[...])
    v_ref[...] = jnp.dot(h, wv_ref[...], preferred_element_type=F32)


def _mem_kv(mem, g, w_k, w_v, k_gain):
    vmem = 2 * (N_MEM * D_MODEL * 4 + 2 * D_MODEL * MEM_WIDTH * 2 + 2 * N_MEM * MEM_WIDTH * 4) + N_MEM * D_MODEL * 2
    full = lambda shape: pl.BlockSpec(shape, lambda i: (0,) * len(shape))
    return pl.pallas_call(
        _mem_kv_kernel,
        grid=(1,),
        in_specs=[full((N_MEM, D_MODEL)), full((1, D_MODEL)), full((D_MODEL, MEM_WIDTH)),
                  full((D_MODEL, MEM_WIDTH)), full((1, MEM_HEAD_DIM))],
        out_specs=[full((N_MEM, MEM_WIDTH)), full((N_MEM, MEM_WIDTH))],
        out_shape=[jax.ShapeDtypeStruct((N_MEM, MEM_WIDTH), F32)] * 2,
        scratch_shapes=[pltpu.VMEM((N_MEM, D_MODEL), BF16)],
        compiler_params=_params(vmem, 1),
        name="mem_kv",
    )(mem, g, w_k, w_v, k_gain)


CROSS_TILES_PROMPT = M_PROMPT // TM_CROSS
BATCH_PER_CROSS_TILE = TM_CROSS // DEC_SEQ


def _cross_heads(q, mk_ref, mv_ref, b, qg, oc_ref, rows):
    for hd in range(MEM_HEADS):
        cols = slice(hd * MEM_HEAD_DIM, (hd + 1) * MEM_HEAD_DIM)
        qn = _rms_rows(q[:, cols], qg).astype(BF16)
        s = lax.dot_general(qn, mk_ref[b, :, cols].astype(BF16), (((1,), (1,)), ((), ())),
                            preferred_element_type=F32) * (MEM_HEAD_DIM ** -0.5)
        p = jnp.exp(s - jnp.max(s, axis=-1, keepdims=True))
        den = jnp.sum(p, axis=-1, keepdims=True)
        o = jnp.dot(p.astype(BF16), mv_ref[b, :, cols].astype(BF16), preferred_element_type=F32) / den
        oc_ref[rows, cols] = o.astype(BF16)


def _cross_kernel(x_ref, g_ref, wq_ref, qg_ref, mkp_ref, mvp_ref, mks_ref, mvs_ref, wo_ref,
                  o_ref, h_ref, q_ref, oc_ref):
    i = pl.program_id(0)
    _norm_rows_to(x_ref, g_ref, h_ref, TM_CROSS)
    q_ref[...] = jnp.dot(h_ref[...], wq_ref[...], preferred_element_type=F32)

    @pl.when(i < CROSS_TILES_PROMPT)
    def _():
        _cross_heads(q_ref[...], mkp_ref, mvp_ref, 0, qg_ref[...], oc_ref, slice(0, TM_CROSS))

    @pl.when(i >= CROSS_TILES_PROMPT)
    def _():
        for b in range(BATCH_PER_CROSS_TILE):
            rows = slice(b * DEC_SEQ, (b + 1) * DEC_SEQ)
            _cross_heads(q_ref[rows, :], mks_ref, mvs_ref, b, qg_ref[...], oc_ref, rows)

    o_ref[...] = x_ref[...] + jnp.dot(oc_ref[...], wo_ref[...], preferred_element_type=F32)


def _cross(x, g, w_q, q_gain, mk_p, mv_p, mk_s, mv_s, w_o):
    const = lambda shape: pl.BlockSpec(shape, lambda i: (0,) * len(shape))
    once = lambda shape: pl.BlockSpec(shape, lambda i: (0,) * len(shape), pipeline_mode=pl.Buffered(1))
    sample_mem = pl.BlockSpec((BATCH_PER_CROSS_TILE, N_MEM, MEM_WIDTH),
                              lambda i: (jnp.maximum(i - CROSS_TILES_PROMPT, 0), 0, 0))
    vmem = (4 * TM_CROSS * D_MODEL * 4 + 2 * D_MODEL * MEM_WIDTH * 2
            + 4 * (1 + BATCH_PER_CROSS_TILE) * N_MEM * MEM_WIDTH * 4
            + TM_CROSS * D_MODEL * 2 + TM_CROSS * MEM_WIDTH * 6 + (6 << 20))
    return pl.pallas_call(
        _cross_kernel,
        grid=(M_ALL // TM_CROSS,),
        in_specs=[
            pl.BlockSpec((TM_CROSS, D_MODEL), lambda i: (i, 0)),
            const((1, D_MODEL)),
            once((D_MODEL, MEM_WIDTH)),
            const((1, MEM_HEAD_DIM)),
            const(mk_p.shape), const(mv_p.shape), sample_mem, sample_mem,
            once((MEM_WIDTH, D_MODEL)),
        ],
        out_specs=pl.BlockSpec((TM_CROSS, D_MODEL), lambda i: (i, 0)),
        out_shape=jax.ShapeDtypeStruct((M_ALL, D_MODEL), F32),
        scratch_shapes=[pltpu.VMEM((TM_CROSS, D_MODEL), BF16), pltpu.VMEM((TM_CROSS, MEM_WIDTH), F32),
                        pltpu.VMEM((TM_CROSS, MEM_WIDTH), BF16)],
        compiler_params=_params(vmem, 1),
        name="cross",
    )(x, g, w_q, q_gain, mk_p, mv_p, mk_s, mv_s, w_o)


def _ffn_kernel(x_ref, g_ref, wg_ref, wu_ref, wd_ref, o_ref, h_ref):
    @pl.when(pl.program_id(1) == 0)
    def _():
        _norm_rows_to(x_ref, g_ref, h_ref, TM_FFN, copy_ref=o_ref)

    h = h_ref[...]
    gate = jnp.dot(h, wg_ref[...], preferred_element_type=F32)
    up = jnp.dot(h, wu_ref[...], preferred_element_type=F32)
    act = (gate * jax.nn.sigmoid(gate) * up).astype(BF16)
    o_ref[...] += jnp.dot(act, wd_ref[...], preferred_element_type=F32)


def _ffn(x, g, w_gate, w_up, w_down):
    vmem = 2 * TM_FFN * D_MODEL * 4 + TM_FFN * D_MODEL * 2 + 2 * 3 * D_MODEL * TF_FFN * 2 + 4 * TM_FFN * TF_FFN * 4
    return pl.pallas_call(
        _ffn_kernel,
        grid=(M_ALL // TM_FFN, F_PAD // TF_FFN),
        in_specs=[
            pl.BlockSpec((TM_FFN, D_MODEL), lambda i, f: (i, 0), pipeline_mode=pl.Buffered(1)),
            pl.BlockSpec((1, D_MODEL), lambda i, f: (0, 0)),
            pl.BlockSpec((D_MODEL, TF_FFN), lambda i, f: (0, f)),
            pl.BlockSpec((D_MODEL, TF_FFN), lambda i, f: (0, f)),
            pl.BlockSpec((TF_FFN, D_MODEL), lambda i, f: (f, 0)),
        ],
        out_specs=pl.BlockSpec((TM_FFN, D_MODEL), lambda i, f: (i, 0), pipeline_mode=pl.Buffered(1)),
        out_shape=jax.ShapeDtypeStruct((M_ALL, D_MODEL), F32),
        scratch_shapes=[pltpu.VMEM((TM_FFN, D_MODEL), BF16)],
        compiler_params=_params(vmem, 2),
        name="ffn",
    )(x, g, w_gate, w_up, w_down)


def _row(v):
    return v.astype(F32).reshape(1, -1)


def kernel(x_prompt, x_sample, cache_attn_k, cache_attn_v, state_pool, cache_mem_k, cache_mem_v, mem_prompt,
           g_mix, w_in, q_norm, k_norm, attn_sinks, w_pool, pool_scale, w_out,
           g_cross, g_mem, w_q_mem, w_k_mem, w_v_mem, q_norm_mem, k_norm_mem, w_o_mem,
           g_ffn, w_gate, w_up, w_down):
    x = jnp.concatenate([x_prompt.reshape(M_PROMPT, D_MODEL), x_sample.reshape(M_SAMPLE, D_MODEL)], axis=0)
    mem = mem_prompt.reshape(N_MEM, D_MODEL)
    q_end, k_end, v_end = ATTN_WIDTH, ATTN_WIDTH + KV_WIDTH, ATTN_WIDTH + 2 * KV_WIDTH

    ak_p, av_p, pl_p, mk_p, mv_p, ak_s, av_s, pl_s = ([] for _ in range(8))
    for l in range(DEPTH):
        w_in_l = jnp.concatenate([w_in[l, :, :q_end], w_in[l, :, v_end:], w_in[l, :, q_end:v_end]],
                                 axis=1).astype(BF16)
        pad_f = F_PAD - D_FF
        w_gate_l = jnp.pad(w_gate[l].astype(BF16), ((0, 0), (0, pad_f)))
        w_up_l = jnp.pad(w_up[l].astype(BF16), ((0, 0), (0, pad_f)))
        w_down_l = jnp.pad(w_down[l].astype(BF16), ((0, pad_f), (0, 0)))

        z = _mixer_in(x, _row(g_mix[l]), w_in_l)
        kn = _k_norm(z, _row(jnp.tile(k_norm[l], 2)))
        v = z[:, Z_V0:Z_V0 + KV_WIDTH]
        kn_s = kn[M_PROMPT:].reshape(DEC_BATCH, DEC_SEQ, KV_WIDTH)
        v_s = v[M_PROMPT:].reshape(DEC_BATCH, DEC_SEQ, KV_WIDTH)
        k_full = jnp.concatenate([cache_attn_k[l].reshape(DEC_BATCH, WINDOW, KV_WIDTH), kn_s], axis=1)
        v_full = jnp.concatenate([cache_attn_v[l].reshape(DEC_BATCH, WINDOW, KV_WIDTH), v_s], axis=1)
        k_band = jnp.concatenate([kn[:M_PROMPT], k_full.reshape(DEC_BATCH * BAND, KV_WIDTH)], axis=0)
        v_band = jnp.concatenate([v[:M_PROMPT], v_full.reshape(DEC_BATCH * BAND, KV_WIDTH)], axis=0)
        bias, sink_e, sink_o = _swa_constants(attn_sinks[l])
        q_gain2 = _row(jnp.tile(q_norm[l], 2)) * (HEAD_DIM ** -0.5)
        attn = _swa(z, k_band, v_band, q_gain2, bias, sink_e, sink_o)

        state_halo = jnp.pad(state_pool[l].astype(F32), ((0, 0), (HALO - POOL_STATE, 0), (0, 0)))
        pooled = _pool(z, state_halo, w_pool[l].astype(BF16), _row(pool_scale[l]))
        x = _mixer_out(attn, pooled, x, w_out[l].astype(BF16))

        ak_p.append(kn[M_PROMPT - WINDOW:M_PROMPT].reshape(1, WINDOW, N_KV_HEADS, HEAD_DIM))
        av_p.append(v[M_PROMPT - WINDOW:M_PROMPT].reshape(1, WINDOW, N_KV_HEADS, HEAD_DIM))
        pl_p.append(z[M_PROMPT - POOL_STATE:M_PROMPT, Z_U0:Z_U0 + POOL_WIDTH].reshape(1, POOL_STATE, POOL_WIDTH))
        ak_s.append(k_full[:, -WINDOW:].reshape(DEC_BATCH, WINDOW, N_KV_HEADS, HEAD_DIM))
        av_s.append(v_full[:, -WINDOW:].reshape(DEC_BATCH, WINDOW, N_KV_HEADS, HEAD_DIM))
        u_s = z[M_PROMPT:, Z_U0:Z_U0 + POOL_WIDTH].reshape(DEC_BATCH, DEC_SEQ, POOL_WIDTH)
        pl_s.append(jnp.concatenate([state_pool[l].astype(F32), u_s], axis=1)[:, -POOL_STATE:])

        mk, mv = _mem_kv(mem, _row(g_mem[l]), w_k_mem[l].astype(BF16), w_v_mem[l].astype(BF16),
                         _row(k_norm_mem[l]))
        mk_p.append(mk.reshape(1, N_MEM, MEM_HEADS, MEM_HEAD_DIM))
        mv_p.append(mv.reshape(1, N_MEM, MEM_HEADS, MEM_HEAD_DIM))
        x = _cross(x, _row(g_cross[l]), w_q_mem[l].astype(BF16), _row(q_norm_mem[l]),
                   mk.reshape(1, N_MEM, MEM_WIDTH), mv.reshape(1, N_MEM, MEM_WIDTH),
                   cache_mem_k[l].reshape(DEC_BATCH, N_MEM, MEM_WIDTH).astype(F32),
                   cache_mem_v[l].reshape(DEC_BATCH, N_MEM, MEM_WIDTH).astype(F32),
                   w_o_mem[l].astype(BF16))

        x = _ffn(x, _row(g_ffn[l]), w_gate_l, w_up_l, w_down_l)

    return (x[:M_PROMPT].reshape(1, SEQ, D_MODEL), x[M_PROMPT:].reshape(DEC_BATCH, DEC_SEQ, D_MODEL),
            jnp.stack(ak_p), jnp.stack(av_p), jnp.stack(pl_p), jnp.stack(mk_p), jnp.stack(mv_p),
            jnp.stack(ak_s), jnp.stack(av_s), jnp.stack(pl_s))
```

```python
import functools

import jax
import jax.numpy as jnp
from jax import lax
from jax.experimental import pallas as pl
from jax.experimental.pallas import tpu as pltpu

F32 = jnp.float32
BF16 = jnp.bfloat16

D_MODEL = 4096
SEQ = 8192
DEPTH = 2
DEC_BATCH = 8
DEC_SEQ = 64
PAST_LEN = 4096
CHUNK = 64
WINDOW = 128
ATTN_WIDTH = D_MODEL // 2
POOL_WIDTH = D_MODEL - ATTN_WIDTH
HEAD_DIM = 64
N_HEADS = ATTN_WIDTH // HEAD_DIM
N_KV_HEADS = N_HEADS // 8
KV_WIDTH = N_KV_HEADS * HEAD_DIM
POOL_WINDOWS = (2, 4, 8, 16)
POOL_GROUP_WIDTH = POOL_WIDTH // len(POOL_WINDOWS)
POOL_STATE = max(POOL_WINDOWS) - 1
IN_WIDTH = ATTN_WIDTH + 2 * KV_WIDTH + POOL_WIDTH
N_MEM = 256
MEM_HEADS = 4
MEM_HEAD_DIM = 128
MEM_WIDTH = MEM_HEADS * MEM_HEAD_DIM
D_FF = -(-8 * D_MODEL // (3 * 256)) * 256
EPS = 1e-6
NEG_INF = -1e30

M_PROMPT = SEQ
M_SAMPLE = DEC_BATCH * DEC_SEQ
M_ALL = M_PROMPT + M_SAMPLE
N_CHUNKS_PROMPT = M_PROMPT // CHUNK
N_CHUNKS = M_ALL // CHUNK
BAND = WINDOW + CHUNK
BAND_BLOCKS = BAND // CHUNK

Z_U0 = ATTN_WIDTH
Z_K0 = ATTN_WIDTH + POOL_WIDTH
Z_V0 = Z_K0 + KV_WIDTH

LANES = 128
SUBLANES = 8
BF16_TILE_ROWS = 16
HALO = BF16_TILE_ROWS
VMEM_LIMIT_CAP = 60000 * 1024
CAST_BLOCK_BYTES = 8 << 20

TM = 512
PROMPT_TILES = M_PROMPT // TM
ROW_TILES = M_ALL // TM
TN_IN = 768
TN_OUT = 1024
TF_FFN = 512
NF_FFN = -(-D_FF // TF_FFN)
TF_LAST = D_FF - (NF_FFN - 1) * TF_FFN
TM_CROSS = 256
TM_KN = 512
TP_POOL = 256
NORM_ROWS = BF16_TILE_ROWS
NORM_UNROLL = 4
W_IN_COL_BLOCK = KV_WIDTH


def _params(vmem_bytes, n_axes):
    limit = min(int(vmem_bytes * 1.2) + (6 << 20), VMEM_LIMIT_CAP)
    return pltpu.CompilerParams(dimension_semantics=("arbitrary",) * n_axes, vmem_limit_bytes=limit)


def _rms_rows(x, g):
    return x * lax.rsqrt(jnp.mean(x * x, axis=-1, keepdims=True) + EPS) * g


def _norm_rows_to(x_ref, g_ref, h_ref, n_rows, copy_ref=None):
    def body(r, carry):
        rows = pl.ds(pl.multiple_of(r * NORM_ROWS, NORM_ROWS), NORM_ROWS)
        x = x_ref[rows, :]
        h_ref[rows, :] = _rms_rows(x, g_ref[...]).astype(BF16)
        if copy_ref is not None:
            copy_ref[rows, :] = x
        return carry

    lax.fori_loop(0, n_rows // NORM_ROWS, body, 0, unroll=NORM_UNROLL)


def _half_lane_rms(x, gain2):
    lo = lax.broadcasted_iota(jnp.int32, (1, LANES), 1) < HEAD_DIM
    x2 = x * x
    ss_lo = jnp.sum(jnp.where(lo, x2, 0.0), axis=-1, keepdims=True)
    ss_hi = jnp.sum(jnp.where(lo, 0.0, x2), axis=-1, keepdims=True)
    r = jnp.where(lo, lax.rsqrt(ss_lo / HEAD_DIM + EPS), lax.rsqrt(ss_hi / HEAD_DIM + EPS))
    return x * r * gain2


def _cast_kernel(w_ref, o_ref):
    o_ref[...] = w_ref[...].astype(BF16)


def _cast_rows(w):
    depth, rows, cols = w.shape
    tr = max(t for t in range(BF16_TILE_ROWS, rows + 1, BF16_TILE_ROWS)
             if rows % t == 0 and t * cols * 4 <= CAST_BLOCK_BYTES)
    spec = pl.BlockSpec((None, tr, cols), lambda l, i: (l, i, 0))
    return pl.pallas_call(
        _cast_kernel, grid=(depth, rows // tr), in_specs=[spec], out_specs=spec,
        out_shape=jax.ShapeDtypeStruct(w.shape, BF16),
        compiler_params=_params(2 * tr * cols * 6, 2), name="cast_rows",
    )(w)


def _w_in_source_block(j):
    q_blocks, u_blocks = ATTN_WIDTH // W_IN_COL_BLOCK, POOL_WIDTH // W_IN_COL_BLOCK
    kv_blocks = 2 * KV_WIDTH // W_IN_COL_BLOCK
    return jnp.where(j < q_blocks, j, jnp.where(j < q_blocks + u_blocks, j + kv_blocks, j - u_blocks))


def _cast_w_in(w):
    depth = w.shape[0]
    return pl.pallas_call(
        _cast_kernel, grid=(depth, IN_WIDTH // W_IN_COL_BLOCK),
        in_specs=[pl.BlockSpec((None, D_MODEL, W_IN_COL_BLOCK), lambda l, j: (l, 0, _w_in_source_block(j)))],
        out_specs=pl.BlockSpec((None, D_MODEL, W_IN_COL_BLOCK), lambda l, j: (l, 0, j)),
        out_shape=jax.ShapeDtypeStruct(w.shape, BF16),
        compiler_params=_params(2 * D_MODEL * W_IN_COL_BLOCK * 6, 2), name="cast_w_in",
    )(w)


def _mixer_in_kernel(*refs, split):
    x_refs, (g_ref, w_ref, z_ref, h_ref) = refs[:-4], refs[-4:]
    i, j = pl.program_id(0), pl.program_id(1)
    if split:
        conds = ((j == 0) & (i < PROMPT_TILES), (j == 0) & (i >= PROMPT_TILES))
    else:
        conds = (j == 0,)
    for x_ref, cond in zip(x_refs, conds):
        @pl.when(cond)
        def _(x_ref=x_ref):
            _norm_rows_to(x_ref, g_ref, h_ref, TM)

    z_ref[...] = jnp.dot(h_ref[...], w_ref[...], preferred_element_type=F32)


def _x_specs(split, width, col):
    if not split:
        return [pl.BlockSpec((TM, width), lambda i, j: (i, col(i, j)))]
    return [
        pl.BlockSpec((TM, width), lambda i, j: (jnp.minimum(i, PROMPT_TILES - 1), col(i, j))),
        pl.BlockSpec((TM, width), lambda i, j: (0, jnp.where(i >= PROMPT_TILES, col(i, j), 0))),
    ]


def _mixer_in(xs, g, w, layer):
    split = len(xs) == 2
    vmem = (2 * len(xs) * TM * D_MODEL * 4 + TM * D_MODEL * 2 + 2 * D_MODEL * TN_IN * 2 + 2 * TM * TN_IN * 4)
    return pl.pallas_call(
        functools.partial(_mixer_in_kernel, split=split),
        grid=(ROW_TILES, IN_WIDTH // TN_IN),
        in_specs=_x_specs(split, D_MODEL, lambda i, j: 0) + [
            pl.BlockSpec((1, D_MODEL), lambda i, j: (0, 0)),
            pl.BlockSpec((None, D_MODEL, TN_IN), lambda i, j: (layer, 0, j)),
        ],
        out_specs=pl.BlockSpec((TM, TN_IN), lambda i, j: (i, j)),
        out_shape=jax.ShapeDtypeStruct((M_ALL, IN_WIDTH), F32),
        scratch_shapes=[pltpu.VMEM((TM, D_MODEL), BF16)],
        compiler_params=_params(vmem, 2),
        name="mixer_in",
    )(*xs, g, w)


def _k_norm_kernel(k_ref, g_ref, o_ref):
    for t in range(KV_WIDTH // LANES):
        cols = slice(t * LANES, (t + 1) * LANES)
        o_ref[:, cols] = _half_lane_rms(k_ref[:, cols], g_ref[...])


def _k_norm(z, gain2):
    vmem = 4 * TM_KN * KV_WIDTH * 4
    return pl.pallas_call(
        _k_norm_kernel,
        grid=(M_ALL // TM_KN,),
        in_specs=[
            pl.BlockSpec((TM_KN, KV_WIDTH), lambda i: (i, Z_K0 // KV_WIDTH)),
            pl.BlockSpec((1, LANES), lambda i: (0, 0)),
        ],
        out_specs=pl.BlockSpec((TM_KN, KV_WIDTH), lambda i: (i, 0)),
        out_shape=jax.ShapeDtypeStruct((M_ALL, KV_WIDTH), F32),
        compiler_params=_params(vmem, 1),
        name="k_norm",
    )(z, gain2)


PAIRS_PER_KV = N_HEADS // N_KV_HEADS // 2
QUERIES_PER_KV = PAIRS_PER_KV * CHUNK
N_PAIRS = N_HEADS // 2


def _swa_kernel(q_ref, k0_ref, k1_ref, k2_ref, v0_ref, v1_ref, v2_ref,
                qg_ref, bias_ref, sink_e_ref, sink_o_ref, o_ref):
    c = pl.program_id(0)
    lo = lax.broadcasted_iota(jnp.int32, (1, LANES), 1) < HEAD_DIM

    q = jnp.concatenate([q_ref[:, p * LANES:(p + 1) * LANES] for p in range(N_PAIRS)], axis=0)

    sel_row = lax.broadcasted_iota(jnp.int32, (SUBLANES, LANES), 0)
    sel_lo = lax.broadcasted_iota(jnp.int32, (SUBLANES, LANES), 1) < HEAD_DIM
    sel = jnp.where(((sel_row == 0) & sel_lo) | ((sel_row == 1) & ~sel_lo), 1.0, 0.0).astype(BF16)
    q2 = q * q
    q2_hi = q2.astype(BF16)
    q2_lo = (q2 - q2_hi.astype(F32)).astype(BF16)
    nt = (((1,), (1,)), ((), ()))
    ss = (lax.dot_general(sel, q2_hi, nt, preferred_element_type=F32)
          + lax.dot_general(sel, q2_lo, nt, preferred_element_type=F32))
    r = lax.rsqrt(ss / HEAD_DIM + EPS)
    qg = (q * qg_ref[...]).astype(BF16)

    kb = jnp.concatenate([k0_ref[...], k1_ref[...], k2_ref[...]], axis=0)
    vb = jnp.concatenate([v0_ref[...], v1_ref[...], v2_ref[...],
                          jnp.zeros((KV_WIDTH - BAND, KV_WIDTH), F32)], axis=0)
    v_t = vb.T

    j_min = jnp.where(c < BAND_BLOCKS - 1, WINDOW - CHUNK * c, 0)
    valid = lax.broadcasted_iota(jnp.int32, (BAND, 1), 0) >= j_min

    def head_softmax(s_raw, r_row, bias, sink):
        s = jnp.where(valid, s_raw * r_row - bias, NEG_INF)
        m = jnp.maximum(jnp.max(s, axis=0, keepdims=True), sink)
        p = jnp.exp(s - m)
        return p.astype(BF16), jnp.sum(p, axis=0, keepdims=True) + jnp.exp(sink - m)

    for kv_tile in range(KV_WIDTH // LANES):
        kt = kb[:, kv_tile * LANES:(kv_tile + 1) * LANES]
        kt_r = pltpu.roll(kt, HEAD_DIM, axis=1)
        for par in range(2):
            hk = 2 * kv_tile + par
            k_e = jnp.where(lo, kt_r if par else kt, 0.0)
            k_o = jnp.where(lo, 0.0, kt if par else kt_r)
            kk = jnp.concatenate([k_e, k_o], axis=0).astype(BF16)
            cols = slice(hk * QUERIES_PER_KV, (hk + 1) * QUERIES_PER_KV)
            s_t = lax.dot_general(kk, qg[cols], nt, preferred_element_type=F32)
            bias = bias_ref[hk]
            p_e, d_e = head_softmax(s_t[:BAND], r[0:1, cols], bias[:BAND], sink_e_ref[hk])
            p_o, d_o = head_softmax(s_t[BAND:], r[1:2, cols], bias[BAND:], sink_o_ref[hk])
            v_h = v_t[hk * HEAD_DIM:(hk + 1) * HEAD_DIM, :BAND].astype(BF16)
            o_e = jnp.dot(v_h, p_e, preferred_element_type=F32) / d_e
            o_o = jnp.dot(v_h, p_o, preferred_element_type=F32) / d_o
            o = jnp.concatenate([o_e, o_o], axis=0).T
            for pr in range(PAIRS_PER_KV):
                pair = hk * PAIRS_PER_KV + pr
                o_ref[:, pair * LANES:(pair + 1) * LANES] = o[pr * CHUNK:(pr + 1) * CHUNK].astype(BF16)


def _band_block(c, j):
    prompt = jnp.maximum(c - (BAND_BLOCKS - 1) + j, 0)
    sample = N_CHUNKS_PROMPT + (c - N_CHUNKS_PROMPT) * BAND_BLOCKS + j
    return jnp.where(c < N_CHUNKS_PROMPT, prompt, sample)


def _swa(z, k_band, v_band, q_gain2, bias, sink_e, sink_o):
    band_specs = [pl.BlockSpec((CHUNK, KV_WIDTH), functools.partial(lambda c, j: (_band_block(c, j), 0), j=j))
                  for j in range(BAND_BLOCKS)]
    const3 = lambda a: pl.BlockSpec(a.shape, lambda c: (0, 0, 0))
    vmem = (2 * CHUNK * ATTN_WIDTH * 4 + 12 * CHUNK * KV_WIDTH * 4 + 2 * bias.size * 4
            + 4 * N_KV_HEADS * SUBLANES * QUERIES_PER_KV * 4 + 2 * CHUNK * ATTN_WIDTH * 2 + (8 << 20))
    return pl.pallas_call(
        _swa_kernel,
        grid=(N_CHUNKS,),
        in_specs=[pl.BlockSpec((CHUNK, ATTN_WIDTH), lambda c: (c, 0))] + band_specs + band_specs + [
            pl.BlockSpec((1, LANES), lambda c: (0, 0)), const3(bias), const3(sink_e), const3(sink_o)],
        out_specs=pl.BlockSpec((CHUNK, ATTN_WIDTH), lambda c: (c, 0)),
        out_shape=jax.ShapeDtypeStruct((M_ALL, ATTN_WIDTH), BF16),
        compiler_params=_params(vmem, 1),
        name="swa",
    )(z, k_band, k_band, k_band, v_band, v_band, v_band, q_gain2, bias, sink_e, sink_o)


def _swa_constants(sinks):
    slopes = 2.0 ** (-8.0 * jnp.arange(1, N_HEADS + 1, dtype=F32) / N_HEADS)
    dist = jnp.abs(jnp.arange(CHUNK)[None, :] + WINDOW - jnp.arange(BAND)[:, None]).astype(F32)
    head = (jnp.arange(N_KV_HEADS)[:, None, None] * (2 * PAIRS_PER_KV)
            + jnp.arange(2)[None, :, None] + 2 * jnp.arange(PAIRS_PER_KV)[None, None, :])
    bias = slopes[head][:, :, None, :, None] * dist[None, None, :, None, :]
    bias = bias.reshape(N_KV_HEADS, 2 * BAND, QUERIES_PER_KV)
    per_pair = sinks.astype(F32).reshape(N_KV_HEADS, PAIRS_PER_KV, 2)
    rows = jnp.repeat(per_pair, CHUNK, axis=1)
    return bias, rows[:, None, :, 0], rows[:, None, :, 1]


POOL_TILES_PROMPT = M_PROMPT // TP_POOL
BATCH_PER_POOL_TILE = TP_POOL // DEC_SEQ


def _trailing_sum(ext, w):
    s, k = ext, 1
    while k < w:
        s = s + pltpu.roll(s, k, axis=0)
        k *= 2
    return s


def _pool_kernel(u_ref, halo_ref, state_ref, w_ref, scale_ref, o_ref, d_ref):
    i = pl.program_id(0)

    def window_diffs(ext, cur, cnt, w):
        return _trailing_sum(ext, w)[HALO:] / cnt - cur

    @pl.when(i < POOL_TILES_PROMPT)
    def _():
        pos = i * TP_POOL + lax.broadcasted_iota(jnp.int32, (TP_POOL, 1), 0)
        for g, w in enumerate(POOL_WINDOWS):
            cols = slice(g * POOL_GROUP_WIDTH, (g + 1) * POOL_GROUP_WIDTH)
            cur = u_ref[:, cols]
            halo = jnp.where(i > 0, halo_ref[:, cols], 0.0)
            ext = jnp.concatenate([halo, cur], axis=0)
            cnt = jnp.minimum(pos + 1, w).astype(F32)
            d_ref[:, cols] = window_diffs(ext, cur, cnt, w)

    @pl.when(i >= POOL_TILES_PROMPT)
    def _():
        for g, w in enumerate(POOL_WINDOWS):
            cols = slice(g * POOL_GROUP_WIDTH, (g + 1) * POOL_GROUP_WIDTH)
            for b in range(BATCH_PER_POOL_TILE):
                rows = slice(b * DEC_SEQ, (b + 1) * DEC_SEQ)
                cur = u_ref[rows, cols]
                ext = jnp.concatenate([state_ref[b, :, cols], cur], axis=0)
                d_ref[rows, cols] = window_diffs(ext, cur, float(w), w)

    for g in range(len(POOL_WINDOWS)):
        cols = slice(g * POOL_GROUP_WIDTH, (g + 1) * POOL_GROUP_WIDTH)
        y = jnp.dot(d_ref[:, cols].astype(BF16), w_ref[cols, :], preferred_element_type=F32)
        o_ref[:, cols] = (y * scale_ref[:, cols]).astype(BF16)


def _pool(z, state_halo, w_pool, scale, layer):
    halo_blocks_per_tile = TP_POOL // HALO
    vmem = (2 * TP_POOL * POOL_WIDTH * 4 + 2 * HALO * POOL_WIDTH * 4 + 2 * BATCH_PER_POOL_TILE * HALO * POOL_WIDTH * 4
            + 2 * POOL_WIDTH * POOL_GROUP_WIDTH * 2 + 2 * TP_POOL * POOL_WIDTH * 2 + TP_POOL * POOL_WIDTH * 4
            + (8 << 20))
    return pl.pallas_call(
        _pool_kernel,
        grid=(M_ALL // TP_POOL,),
        in_specs=[
            pl.BlockSpec((TP_POOL, POOL_WIDTH), lambda i: (i, Z_U0 // POOL_WIDTH)),
            pl.BlockSpec((HALO, POOL_WIDTH),
                         lambda i: (jnp.maximum(i * halo_blocks_per_tile - 1, 0), Z_U0 // POOL_WIDTH)),
            pl.BlockSpec((BATCH_PER_POOL_TILE, HALO, POOL_WIDTH),
                         lambda i: (jnp.maximum(i - POOL_TILES_PROMPT, 0), 0, 0)),
            pl.BlockSpec((None, POOL_WIDTH, POOL_GROUP_WIDTH), lambda i: (layer, 0, 0)),
            pl.BlockSpec((1, POOL_WIDTH), lambda i: (0, 0)),
        ],
        out_specs=pl.BlockSpec((TP_POOL, POOL_WIDTH), lambda i: (i, 0)),
        out_shape=jax.ShapeDtypeStruct((M_ALL, POOL_WIDTH), BF16),
        scratch_shapes=[pltpu.VMEM((TP_POOL, POOL_WIDTH), F32)],
        compiler_params=_params(vmem, 1),
        name="pool",
    )(z, z, state_halo, w_pool, scale)


def _mixer_out_kernel(*refs, split):
    a_ref, p_ref = refs[:2]
    x_refs, (wa_ref, wp_ref, o_ref) = refs[2:-3], refs[-3:]
    y = jnp.dot(a_ref[...], wa_ref[...], preferred_element_type=F32)
    y = y + jnp.dot(p_ref[...], wp_ref[...], preferred_element_type=F32)
    if split:
        x = jnp.where(pl.program_id(0) < PROMPT_TILES, x_refs[0][...], x_refs[1][...])
    else:
        x = x_refs[0][...]
    o_ref[...] = x + y


def _mixer_out(attn, pooled, xs, w_out, layer):
    split = len(xs) == 2
    vmem = 2 * (2 * TM * ATTN_WIDTH * 2 + D_MODEL * TN_OUT * 2 + (1 + len(xs)) * TM * TN_OUT * 4)
    return pl.pallas_call(
        functools.partial(_mixer_out_kernel, split=split),
        grid=(ROW_TILES, D_MODEL // TN_OUT),
        in_specs=[
            pl.BlockSpec((TM, ATTN_WIDTH), lambda i, j: (i, 0)),
            pl.BlockSpec((TM, POOL_WIDTH), lambda i, j: (i, 0)),
        ] + _x_specs(split, TN_OUT, lambda i, j: j) + [
            pl.BlockSpec((None, ATTN_WIDTH, TN_OUT), lambda i, j: (layer, 0, j)),
            pl.BlockSpec((None, POOL_WIDTH, TN_OUT), lambda i, j: (layer, 1, j)),
        ],
        out_specs=pl.BlockSpec((TM, TN_OUT), lambda i, j: (i, j)),
        out_shape=jax.ShapeDtypeStruct((M_ALL, D_MODEL), F32),
        compiler_params=_params(vmem, 2),
        name="mixer_out",
    )(attn, pooled, *xs, w_out, w_out)


def _mem_kv_kernel(m_ref, g_ref, wk_ref, wv_ref, kg_ref, k_ref, v_ref, h_ref):
    _norm_rows_to(m_ref, g_ref, h_ref, N_MEM)
    h = h_ref[...]
    k = jnp.dot(h, wk_ref[...], preferred_element_type=F32)
    for hd in range(MEM_HEADS):
        cols = slice(hd * MEM_HEAD_DIM, (hd + 1) * MEM_HEAD_DIM)
        k_ref[:, cols] = _rms_rows(k[:, cols], kg_ref[...])
    v_ref[...] = jnp.dot(h, wv_ref[...], preferred_element_type=F32)


def _mem_kv(mem, g, w_k, w_v, k_gain, layer):
    vmem = 2 * (N_MEM * D_MODEL * 4 + 2 * D_MODEL * MEM_WIDTH * 2 + 2 * N_MEM * MEM_WIDTH * 4) + N_MEM * D_MODEL * 2
    full = lambda shape: pl.BlockSpec(shape, lambda i: (0,) * len(shape))
    weight = pl.BlockSpec((None, D_MODEL, MEM_WIDTH), lambda i: (layer, 0, 0))
    return pl.pallas_call(
        _mem_kv_kernel,
        grid=(1,),
        in_specs=[full((N_MEM, D_MODEL)), full((1, D_MODEL)), weight, weight, full((1, MEM_HEAD_DIM))],
        out_specs=[full((N_MEM, MEM_WIDTH)), full((N_MEM, MEM_WIDTH))],
        out_shape=[jax.ShapeDtypeStruct((N_MEM, MEM_WIDTH), F32)] * 2,
        scratch_shapes=[pltpu.VMEM((N_MEM, D_MODEL), BF16)],
        compiler_params=_params(vmem, 1),
        name="mem_kv",
    )(mem, g, w_k, w_v, k_gain)


CROSS_TILES_PROMPT = M_PROMPT // TM_CROSS
BATCH_PER_CROSS_TILE = TM_CROSS // DEC_SEQ


def _cross_heads(q, mk_ref, mv_ref, b, qg, oc_ref, rows):
    for hd in range(MEM_HEADS):
        cols = slice(hd * MEM_HEAD_DIM, (hd + 1) * MEM_HEAD_DIM)
        qn = _rms_rows(q[:, cols], qg).astype(BF16)
        s = lax.dot_general(qn, mk_ref[b, :, cols].astype(BF16), (((1,), (1,)), ((), ())),
                            preferred_element_type=F32) * (MEM_HEAD_DIM ** -0.5)
        p = jnp.exp(s - jnp.max(s, axis=-1, keepdims=True))
        den = jnp.sum(p, axis=-1, keepdims=True)
        o = jnp.dot(p.astype(BF16), mv_ref[b, :, cols].astype(BF16), preferred_element_type=F32) / den
        oc_ref[rows, cols] = o.astype(BF16)


def _cross_kernel(x_ref, g_ref, wq_ref, qg_ref, mkp_ref, mvp_ref, mks_ref, mvs_ref, wo_ref,
                  o_ref, h_ref, q_ref, oc_ref):
    i = pl.program_id(0)
    _norm_rows_to(x_ref, g_ref, h_ref, TM_CROSS)
    q_ref[...] = jnp.dot(h_ref[...], wq_ref[...], preferred_element_type=F32)

    @pl.when(i < CROSS_TILES_PROMPT)
    def _():
        _cross_heads(q_ref[...], mkp_ref, mvp_ref, 0, qg_ref[...], oc_ref, slice(0, TM_CROSS))

    @pl.when(i >= CROSS_TILES_PROMPT)
    def _():
        for b in range(BATCH_PER_CROSS_TILE):
            rows = slice(b * DEC_SEQ, (b + 1) * DEC_SEQ)
            _cross_heads(q_ref[rows, :], mks_ref, mvs_ref, b, qg_ref[...], oc_ref, rows)

    o_ref[...] = x_ref[...] + jnp.dot(oc_ref[...], wo_ref[...], preferred_element_type=F32)


def _cross(x, g, w_q, q_gain, mk_p, mv_p, mk_s, mv_s, w_o, layer):
    const = lambda shape: pl.BlockSpec(shape, lambda i: (0,) * len(shape))
    once = lambda shape: pl.BlockSpec((None,) + shape, lambda i: (layer, 0, 0), pipeline_mode=pl.Buffered(1))
    sample_mem = pl.BlockSpec((BATCH_PER_CROSS_TILE, N_MEM, MEM_WIDTH),
                              lambda i: (jnp.maximum(i - CROSS_TILES_PROMPT, 0), 0, 0))
    vmem = (4 * TM_CROSS * D_MODEL * 4 + 2 * D_MODEL * MEM_WIDTH * 2
            + 4 * (1 + BATCH_PER_CROSS_TILE) * N_MEM * MEM_WIDTH * 4
            + TM_CROSS * D_MODEL * 2 + TM_CROSS * MEM_WIDTH * 6 + (6 << 20))
    return pl.pallas_call(
        _cross_kernel,
        grid=(M_ALL // TM_CROSS,),
        in_specs=[
            pl.BlockSpec((TM_CROSS, D_MODEL), lambda i: (i, 0)),
            const((1, D_MODEL)),
            once((D_MODEL, MEM_WIDTH)),
            const((1, MEM_HEAD_DIM)),
            const(mk_p.shape), const(mv_p.shape), sample_mem, sample_mem,
            once((MEM_WIDTH, D_MODEL)),
        ],
        out_specs=pl.BlockSpec((TM_CROSS, D_MODEL), lambda i: (i, 0)),
        out_shape=jax.ShapeDtypeStruct((M_ALL, D_MODEL), F32),
        scratch_shapes=[pltpu.VMEM((TM_CROSS, D_MODEL), BF16), pltpu.VMEM((TM_CROSS, MEM_WIDTH), F32),
                        pltpu.VMEM((TM_CROSS, MEM_WIDTH), BF16)],
        compiler_params=_params(vmem, 1),
        name="cross",
    )(x, g, w_q, q_gain, mk_p, mv_p, mk_s, mv_s, w_o)


def _ffn_kernel(x_ref, g_ref, wg_ref, wu_ref, wd_ref, o_ref, h_ref):
    f = pl.program_id(1)

    @pl.when(f == 0)
    def _():
        _norm_rows_to(x_ref, g_ref, h_ref, TM, copy_ref=o_ref)

    def accumulate(width):
        h = h_ref[...]
        gate = jnp.dot(h, wg_ref[:, :width], preferred_element_type=F32)
        up = jnp.dot(h, wu_ref[:, :width], preferred_element_type=F32)
        act = (gate * jax.nn.sigmoid(gate) * up).astype(BF16)
        o_ref[...] += jnp.dot(act, wd_ref[:width, :], preferred_element_type=F32)

    pl.when(f < NF_FFN - 1)(functools.partial(accumulate, TF_FFN))
    pl.when(f == NF_FFN - 1)(functools.partial(accumulate, TF_LAST))


def _ffn(x, g, w_gate, w_up, w_down, layer, first_tile, n_tiles):
    vmem = (3 * TM * D_MODEL * 4 + TM * D_MODEL * 2 + 2 * 3 * D_MODEL * TF_FFN * 2 + 4 * TM * TF_FFN * 4)
    return pl.pallas_call(
        _ffn_kernel,
        grid=(n_tiles, NF_FFN),
        in_specs=[
            pl.BlockSpec((TM, D_MODEL), lambda i, f: (first_tile + i, 0)),
            pl.BlockSpec((1, D_MODEL), lambda i, f: (0, 0)),
            pl.BlockSpec((None, D_MODEL, TF_FFN), lambda i, f: (layer, 0, f)),
            pl.BlockSpec((None, D_MODEL, TF_FFN), lambda i, f: (layer, 0, f)),
            pl.BlockSpec((None, TF_FFN, D_MODEL), lambda i, f: (layer, f, 0)),
        ],
        out_specs=pl.BlockSpec((TM, D_MODEL), lambda i, f: (i, 0), pipeline_mode=pl.Buffered(1)),
        out_shape=jax.ShapeDtypeStruct((n_tiles * TM, D_MODEL), F32),
        scratch_shapes=[pltpu.VMEM((TM, D_MODEL), BF16)],
        compiler_params=_params(vmem, 2),
        name="ffn",
    )(x, g, w_gate, w_up, w_down)


def _row(v):
    return v.astype(F32).reshape(1, -1)


def kernel(x_prompt, x_sample, cache_attn_k, cache_attn_v, state_pool, cache_mem_k, cache_mem_v, mem_prompt,
           g_mix, w_in, q_norm, k_norm, attn_sinks, w_pool, pool_scale, w_out,
           g_cross, g_mem, w_q_mem, w_k_mem, w_v_mem, q_norm_mem, k_norm_mem, w_o_mem,
           g_ffn, w_gate, w_up, w_down):
    xs = [x_prompt.reshape(M_PROMPT, D_MODEL), x_sample.reshape(M_SAMPLE, D_MODEL)]
    mem = mem_prompt.reshape(N_MEM, D_MODEL)

    w_in_b = _cast_w_in(w_in)
    w_out_b, w_gate_b, w_up_b, w_down_b = (_cast_rows(w) for w in (w_out, w_gate, w_up, w_down))
    w_q_b, w_k_b, w_v_b, w_o_b = (_cast_rows(w) for w in (w_q_mem, w_k_mem, w_v_mem, w_o_mem))
    w_pool_b = _cast_rows(w_pool.reshape(DEPTH, POOL_WIDTH, POOL_GROUP_WIDTH))

    ak_p, av_p, pl_p, mk_p, mv_p, ak_s, av_s, pl_s = ([] for _ in range(8))
    for l in range(DEPTH):
        z = _mixer_in(xs, _row(g_mix[l]), w_in_b, l)
        kn = _k_norm(z, _row(jnp.tile(k_norm[l], 2)))
        v = z[:, Z_V0:Z_V0 + KV_WIDTH]
        kn_s = kn[M_PROMPT:].reshape(DEC_BATCH, DEC_SEQ, KV_WIDTH)
        v_s = v[M_PROMPT:].reshape(DEC_BATCH, DEC_SEQ, KV_WIDTH)
        k_full = jnp.concatenate([cache_attn_k[l].reshape(DEC_BATCH, WINDOW, KV_WIDTH), kn_s], axis=1)
        v_full = jnp.concatenate([cache_attn_v[l].reshape(DEC_BATCH, WINDOW, KV_WIDTH), v_s], axis=1)
        k_band = jnp.concatenate([kn[:M_PROMPT], k_full.reshape(DEC_BATCH * BAND, KV_WIDTH)], axis=0)
        v_band = jnp.concatenate([v[:M_PROMPT], v_full.reshape(DEC_BATCH * BAND, KV_WIDTH)], axis=0)
        bias, sink_e, sink_o = _swa_constants(attn_sinks[l])
        q_gain2 = _row(jnp.tile(q_norm[l], 2)) * (HEAD_DIM ** -0.5)
        attn = _swa(z, k_band, v_band, q_gain2, bias, sink_e, sink_o)

        state_halo = jnp.pad(state_pool[l].astype(F32), ((0, 0), (HALO - POOL_STATE, 0), (0, 0)))
        pooled = _pool(z, state_halo, w_pool_b, _row(pool_scale[l]), l)
        x = _mixer_out(attn, pooled, xs, w_out_b, l)

        ak_p.append(kn[M_PROMPT - WINDOW:M_PROMPT].reshape(1, WINDOW, N_KV_HEADS, HEAD_DIM))
        av_p.append(v[M_PROMPT - WINDOW:M_PROMPT].reshape(1, WINDOW, N_KV_HEADS, HEAD_DIM))
        pl_p.append(z[M_PROMPT - POOL_STATE:M_PROMPT, Z_U0:Z_U0 + POOL_WIDTH].reshape(1, POOL_STATE, POOL_WIDTH))
        ak_s.append(k_full[:, -WINDOW:].reshape(DEC_BATCH, WINDOW, N_KV_HEADS, HEAD_DIM))
        av_s.append(v_full[:, -WINDOW:].reshape(DEC_BATCH, WINDOW, N_KV_HEADS, HEAD_DIM))
        u_s = z[M_PROMPT:, Z_U0:Z_U0 + POOL_WIDTH].reshape(DEC_BATCH, DEC_SEQ, POOL_WIDTH)
        pl_s.append(jnp.concatenate([state_pool[l].astype(F32), u_s], axis=1)[:, -POOL_STATE:])

        mk, mv = _mem_kv(mem, _row(g_mem[l]), w_k_b, w_v_b, _row(k_norm_mem[l]), l)
        mk_p.append(mk.reshape(1, N_MEM, MEM_HEADS, MEM_HEAD_DIM))
        mv_p.append(mv.reshape(1, N_MEM, MEM_HEADS, MEM_HEAD_DIM))
        x = _cross(x, _row(g_cross[l]), w_q_b, _row(q_norm_mem[l]),
                   mk.reshape(1, N_MEM, MEM_WIDTH), mv.reshape(1, N_MEM, MEM_WIDTH),
                   cache_mem_k[l].reshape(DEC_BATCH, N_MEM, MEM_WIDTH).astype(F32),
                   cache_mem_v[l].reshape(DEC_BATCH, N_MEM, MEM_WIDTH).astype(F32),
                   w_o_b, l)

        ffn = functools.partial(_ffn, x, _row(g_ffn[l]), w_gate_b, w_up_b, w_down_b, l)
        xs = [ffn(0, ROW_TILES)] if l < DEPTH - 1 else [ffn(0, PROMPT_TILES), ffn(PROMPT_TILES, 1)]

    return (xs[0].reshape(1, SEQ, D_MODEL), xs[1].reshape(DEC_BATCH, DEC_SEQ, D_MODEL),
            jnp.stack(ak_p), jnp.stack(av_p), jnp.stack(pl_p), jnp.stack(mk_p), jnp.stack(mv_p),
            jnp.stack(ak_s), jnp.stack(av_s), jnp.stack(pl_s))
```

```python
import functools

import jax
import jax.numpy as jnp
from jax import lax
from jax.experimental import pallas as pl
from jax.experimental.pallas import tpu as pltpu

F32 = jnp.float32
BF16 = jnp.bfloat16

D_MODEL = 4096
SEQ = 8192
DEPTH = 2
DEC_BATCH = 8
DEC_SEQ = 64
PAST_LEN = 4096
CHUNK = 64
WINDOW = 128
ATTN_WIDTH = D_MODEL // 2
POOL_WIDTH = D_MODEL - ATTN_WIDTH
HEAD_DIM = 64
N_HEADS = ATTN_WIDTH // HEAD_DIM
N_KV_HEADS = N_HEADS // 8
KV_WIDTH = N_KV_HEADS * HEAD_DIM
POOL_WINDOWS = (2, 4, 8, 16)
POOL_GROUP_WIDTH = POOL_WIDTH // len(POOL_WINDOWS)
POOL_STATE = max(POOL_WINDOWS) - 1
IN_WIDTH = ATTN_WIDTH + 2 * KV_WIDTH + POOL_WIDTH
N_MEM = 256
MEM_HEADS = 4
MEM_HEAD_DIM = 128
MEM_WIDTH = MEM_HEADS * MEM_HEAD_DIM
D_FF = -(-8 * D_MODEL // (3 * 256)) * 256
EPS = 1e-6
NEG_INF = -1e30

M_PROMPT = SEQ
M_SAMPLE = DEC_BATCH * DEC_SEQ
M_ALL = M_PROMPT + M_SAMPLE
N_CHUNKS_PROMPT = M_PROMPT // CHUNK
N_CHUNKS = M_ALL // CHUNK
BAND = WINDOW + CHUNK
BAND_BLOCKS = BAND // CHUNK

Z_U0 = ATTN_WIDTH
Z_K0 = ATTN_WIDTH + POOL_WIDTH
Z_V0 = Z_K0 + KV_WIDTH

LANES = 128
SUBLANES = 8
BF16_TILE_ROWS = 16
HALO = BF16_TILE_ROWS
VMEM_LIMIT_CAP = 60000 * 1024
CAST_BLOCK_BYTES = 8 << 20

TM = 512
PROMPT_TILES = M_PROMPT // TM
ROW_TILES = M_ALL // TM
TN_IN = 768
TN_OUT = 1024
TF_FFN = 512
NF_FFN = -(-D_FF // TF_FFN)
TF_LAST = D_FF - (NF_FFN - 1) * TF_FFN
TF_SAMPLE = 128
TM_CROSS = 256
TM_KN = 512
TP_POOL = 256
NORM_ROWS = BF16_TILE_ROWS
NORM_UNROLL = 4
W_IN_COL_BLOCK = KV_WIDTH


def _params(vmem_bytes, n_axes):
    limit = min(int(vmem_bytes * 1.2) + (6 << 20), VMEM_LIMIT_CAP)
    return pltpu.CompilerParams(dimension_semantics=("arbitrary",) * n_axes, vmem_limit_bytes=limit)


def _rms_rows(x, g):
    return x * lax.rsqrt(jnp.mean(x * x, axis=-1, keepdims=True) + EPS) * g


def _norm_rows_to(x_ref, g_ref, h_ref, n_rows, copy_ref=None):
    def body(r, carry):
        rows = pl.ds(pl.multiple_of(r * NORM_ROWS, NORM_ROWS), NORM_ROWS)
        x = x_ref[rows, :]
        h_ref[rows, :] = _rms_rows(x, g_ref[...]).astype(BF16)
        if copy_ref is not None:
            copy_ref[rows, :] = x
        return carry

    lax.fori_loop(0, n_rows // NORM_ROWS, body, 0, unroll=NORM_UNROLL)


def _half_lane_rms(x, gain2):
    lo = lax.broadcasted_iota(jnp.int32, (1, LANES), 1) < HEAD_DIM
    x2 = x * x
    ss_lo = jnp.sum(jnp.where(lo, x2, 0.0), axis=-1, keepdims=True)
    ss_hi = jnp.sum(jnp.where(lo, 0.0, x2), axis=-1, keepdims=True)
    r = jnp.where(lo, lax.rsqrt(ss_lo / HEAD_DIM + EPS), lax.rsqrt(ss_hi / HEAD_DIM + EPS))
    return x * r * gain2


def _cast_kernel(w_ref, o_ref):
    o_ref[...] = w_ref[...].astype(BF16)


def _cast_rows(w):
    depth, rows, cols = w.shape
    tr = max(t for t in range(BF16_TILE_ROWS, rows + 1, BF16_TILE_ROWS)
             if rows % t == 0 and t * cols * 4 <= CAST_BLOCK_BYTES)
    spec = pl.BlockSpec((None, tr, cols), lambda l, i: (l, i, 0))
    return pl.pallas_call(
        _cast_kernel, grid=(depth, rows // tr), in_specs=[spec], out_specs=spec,
        out_shape=jax.ShapeDtypeStruct(w.shape, BF16),
        compiler_params=_params(2 * tr * cols * 6, 2), name="cast_rows",
    )(w)


def _w_in_source_block(j):
    q_blocks, u_blocks = ATTN_WIDTH // W_IN_COL_BLOCK, POOL_WIDTH // W_IN_COL_BLOCK
    kv_blocks = 2 * KV_WIDTH // W_IN_COL_BLOCK
    return jnp.where(j < q_blocks, j, jnp.where(j < q_blocks + u_blocks, j + kv_blocks, j - u_blocks))


def _cast_w_in(w):
    depth = w.shape[0]
    return pl.pallas_call(
        _cast_kernel, grid=(depth, IN_WIDTH // W_IN_COL_BLOCK),
        in_specs=[pl.BlockSpec((None, D_MODEL, W_IN_COL_BLOCK), lambda l, j: (l, 0, _w_in_source_block(j)))],
        out_specs=pl.BlockSpec((None, D_MODEL, W_IN_COL_BLOCK), lambda l, j: (l, 0, j)),
        out_shape=jax.ShapeDtypeStruct(w.shape, BF16),
        compiler_params=_params(2 * D_MODEL * W_IN_COL_BLOCK * 6, 2), name="cast_w_in",
    )(w)


def _mixer_in_kernel(xp_ref, xs_ref, g_ref, w_ref, z_ref, h_ref):
    i, j = pl.program_id(0), pl.program_id(1)

    @pl.when((j == 0) & (i < PROMPT_TILES))
    def _():
        _norm_rows_to(xp_ref, g_ref, h_ref, TM)

    @pl.when((j == 0) & (i >= PROMPT_TILES))
    def _():
        _norm_rows_to(xs_ref, g_ref, h_ref, TM)

    z_ref[...] = jnp.dot(h_ref[...], w_ref[...], preferred_element_type=F32)


def _x_specs(width, col):
    return [
        pl.BlockSpec((TM, width), lambda i, j: (jnp.minimum(i, PROMPT_TILES - 1), col(i, j))),
        pl.BlockSpec((TM, width), lambda i, j: (0, jnp.where(i >= PROMPT_TILES, col(i, j), 0))),
    ]


def _mixer_in(x_p, x_s, g, w, layer):
    vmem = 4 * TM * D_MODEL * 4 + TM * D_MODEL * 2 + 2 * D_MODEL * TN_IN * 2 + 2 * TM * TN_IN * 4
    return pl.pallas_call(
        _mixer_in_kernel,
        grid=(ROW_TILES, IN_WIDTH // TN_IN),
        in_specs=_x_specs(D_MODEL, lambda i, j: 0) + [
            pl.BlockSpec((1, D_MODEL), lambda i, j: (0, 0)),
            pl.BlockSpec((None, D_MODEL, TN_IN), lambda i, j: (layer, 0, j)),
        ],
        out_specs=pl.BlockSpec((TM, TN_IN), lambda i, j: (i, j)),
        out_shape=jax.ShapeDtypeStruct((M_ALL, IN_WIDTH), F32),
        scratch_shapes=[pltpu.VMEM((TM, D_MODEL), BF16)],
        compiler_params=_params(vmem, 2),
        name="mixer_in",
    )(x_p, x_s, g, w)


def _k_norm_kernel(k_ref, g_ref, o_ref):
    for t in range(KV_WIDTH // LANES):
        cols = slice(t * LANES, (t + 1) * LANES)
        o_ref[:, cols] = _half_lane_rms(k_ref[:, cols], g_ref[...])


def _k_norm(z, gain2):
    vmem = 4 * TM_KN * KV_WIDTH * 4
    return pl.pallas_call(
        _k_norm_kernel,
        grid=(M_ALL // TM_KN,),
        in_specs=[
            pl.BlockSpec((TM_KN, KV_WIDTH), lambda i: (i, Z_K0 // KV_WIDTH)),
            pl.BlockSpec((1, LANES), lambda i: (0, 0)),
        ],
        out_specs=pl.BlockSpec((TM_KN, KV_WIDTH), lambda i: (i, 0)),
        out_shape=jax.ShapeDtypeStruct((M_ALL, KV_WIDTH), F32),
        compiler_params=_params(vmem, 1),
        name="k_norm",
    )(z, gain2)


PAIRS_PER_KV = N_HEADS // N_KV_HEADS // 2
QUERIES_PER_KV = PAIRS_PER_KV * CHUNK
N_PAIRS = N_HEADS // 2


def _swa_kernel(q_ref, k0_ref, k1_ref, k2_ref, v0_ref, v1_ref, v2_ref,
                qg_ref, bias_ref, sink_e_ref, sink_o_ref, o_ref):
    c = pl.program_id(0)
    lo = lax.broadcasted_iota(jnp.int32, (1, LANES), 1) < HEAD_DIM

    q = jnp.concatenate([q_ref[:, p * LANES:(p + 1) * LANES] for p in range(N_PAIRS)], axis=0)

    sel_row = lax.broadcasted_iota(jnp.int32, (SUBLANES, LANES), 0)
    sel_lo = lax.broadcasted_iota(jnp.int32, (SUBLANES, LANES), 1) < HEAD_DIM
    sel = jnp.where(((sel_row == 0) & sel_lo) | ((sel_row == 1) & ~sel_lo), 1.0, 0.0).astype(BF16)
    q2 = q * q
    q2_hi = q2.astype(BF16)
    q2_lo = (q2 - q2_hi.astype(F32)).astype(BF16)
    nt = (((1,), (1,)), ((), ()))
    ss = (lax.dot_general(sel, q2_hi, nt, preferred_element_type=F32)
          + lax.dot_general(sel, q2_lo, nt, preferred_element_type=F32))
    r = lax.rsqrt(ss / HEAD_DIM + EPS)
    qg = (q * qg_ref[...]).astype(BF16)

    kb = jnp.concatenate([k0_ref[...], k1_ref[...], k2_ref[...]], axis=0)
    vb = jnp.concatenate([v0_ref[...], v1_ref[...], v2_ref[...],
                          jnp.zeros((KV_WIDTH - BAND, KV_WIDTH), F32)], axis=0)
    v_t = vb.T

    j_min = jnp.where(c < BAND_BLOCKS - 1, WINDOW - CHUNK * c, 0)
    valid = lax.broadcasted_iota(jnp.int32, (BAND, 1), 0) >= j_min

    def head_softmax(s_raw, r_row, bias, sink):
        s = jnp.where(valid, s_raw * r_row - bias, NEG_INF)
        m = jnp.maximum(jnp.max(s, axis=0, keepdims=True), sink)
        p = jnp.exp(s - m)
        return p.astype(BF16), jnp.sum(p, axis=0, keepdims=True) + jnp.exp(sink - m)

    for kv_tile in range(KV_WIDTH // LANES):
        kt = kb[:, kv_tile * LANES:(kv_tile + 1) * LANES]
        kt_r = pltpu.roll(kt, HEAD_DIM, axis=1)
        for par in range(2):
            hk = 2 * kv_tile + par
            k_e = jnp.where(lo, kt_r if par else kt, 0.0)
            k_o = jnp.where(lo, 0.0, kt if par else kt_r)
            kk = jnp.concatenate([k_e, k_o], axis=0).astype(BF16)
            cols = slice(hk * QUERIES_PER_KV, (hk + 1) * QUERIES_PER_KV)
            s_t = lax.dot_general(kk, qg[cols], nt, preferred_element_type=F32)
            bias = bias_ref[hk]
            p_e, d_e = head_softmax(s_t[:BAND], r[0:1, cols], bias[:BAND], sink_e_ref[hk])
            p_o, d_o = head_softmax(s_t[BAND:], r[1:2, cols], bias[BAND:], sink_o_ref[hk])
            v_h = v_t[hk * HEAD_DIM:(hk + 1) * HEAD_DIM, :BAND].astype(BF16)
            o_e = jnp.dot(v_h, p_e, preferred_element_type=F32) / d_e
            o_o = jnp.dot(v_h, p_o, preferred_element_type=F32) / d_o
            o = jnp.concatenate([o_e, o_o], axis=0).T
            for pr in range(PAIRS_PER_KV):
                pair = hk * PAIRS_PER_KV + pr
                o_ref[:, pair * LANES:(pair + 1) * LANES] = o[pr * CHUNK:(pr + 1) * CHUNK].astype(BF16)


def _band_block(c, j):
    prompt = jnp.maximum(c - (BAND_BLOCKS - 1) + j, 0)
    sample = N_CHUNKS_PROMPT + (c - N_CHUNKS_PROMPT) * BAND_BLOCKS + j
    return jnp.where(c < N_CHUNKS_PROMPT, prompt, sample)


def _swa(z, k_band, v_band, q_gain2, bias, sink_e, sink_o):
    band_specs = [pl.BlockSpec((CHUNK, KV_WIDTH), functools.partial(lambda c, j: (_band_block(c, j), 0), j=j))
                  for j in range(BAND_BLOCKS)]
    const3 = lambda a: pl.BlockSpec(a.shape, lambda c: (0, 0, 0))
    vmem = (2 * CHUNK * ATTN_WIDTH * 4 + 12 * CHUNK * KV_WIDTH * 4 + 2 * bias.size * 4
            + 4 * N_KV_HEADS * SUBLANES * QUERIES_PER_KV * 4 + 2 * CHUNK * ATTN_WIDTH * 2 + (8 << 20))
    return pl.pallas_call(
        _swa_kernel,
        grid=(N_CHUNKS,),
        in_specs=[pl.BlockSpec((CHUNK, ATTN_WIDTH), lambda c: (c, 0))] + band_specs + band_specs + [
            pl.BlockSpec((1, LANES), lambda c: (0, 0)), const3(bias), const3(sink_e), const3(sink_o)],
        out_specs=pl.BlockSpec((CHUNK, ATTN_WIDTH), lambda c: (c, 0)),
        out_shape=jax.ShapeDtypeStruct((M_ALL, ATTN_WIDTH), BF16),
        compiler_params=_params(vmem, 1),
        name="swa",
    )(z, k_band, k_band, k_band, v_band, v_band, v_band, q_gain2, bias, sink_e, sink_o)


def _swa_constants(sinks):
    slopes = 2.0 ** (-8.0 * jnp.arange(1, N_HEADS + 1, dtype=F32) / N_HEADS)
    dist = jnp.abs(jnp.arange(CHUNK)[None, :] + WINDOW - jnp.arange(BAND)[:, None]).astype(F32)
    head = (jnp.arange(N_KV_HEADS)[:, None, None] * (2 * PAIRS_PER_KV)
            + jnp.arange(2)[None, :, None] + 2 * jnp.arange(PAIRS_PER_KV)[None, None, :])
    bias = slopes[head][:, :, None, :, None] * dist[None, None, :, None, :]
    bias = bias.reshape(N_KV_HEADS, 2 * BAND, QUERIES_PER_KV)
    per_pair = sinks.astype(F32).reshape(N_KV_HEADS, PAIRS_PER_KV, 2)
    rows = jnp.repeat(per_pair, CHUNK, axis=1)
    return bias, rows[:, None, :, 0], rows[:, None, :, 1]


POOL_TILES_PROMPT = M_PROMPT // TP_POOL
BATCH_PER_POOL_TILE = TP_POOL // DEC_SEQ


def _trailing_sum(ext, w):
    s, k = ext, 1
    while k < w:
        s = s + pltpu.roll(s, k, axis=0)
        k *= 2
    return s


def _pool_kernel(u_ref, halo_ref, state_ref, w_ref, scale_ref, o_ref, d_ref):
    i = pl.program_id(0)

    def window_diffs(ext, cur, cnt, w):
        return _trailing_sum(ext, w)[HALO:] / cnt - cur

    @pl.when(i < POOL_TILES_PROMPT)
    def _():
        pos = i * TP_POOL + lax.broadcasted_iota(jnp.int32, (TP_POOL, 1), 0)
        for g, w in enumerate(POOL_WINDOWS):
            cols = slice(g * POOL_GROUP_WIDTH, (g + 1) * POOL_GROUP_WIDTH)
            cur = u_ref[:, cols]
            halo = jnp.where(i > 0, halo_ref[:, cols], 0.0)
            ext = jnp.concatenate([halo, cur], axis=0)
            cnt = jnp.minimum(pos + 1, w).astype(F32)
            d_ref[:, cols] = window_diffs(ext, cur, cnt, w)

    @pl.when(i >= POOL_TILES_PROMPT)
    def _():
        for g, w in enumerate(POOL_WINDOWS):
            cols = slice(g * POOL_GROUP_WIDTH, (g + 1) * POOL_GROUP_WIDTH)
            for b in range(BATCH_PER_POOL_TILE):
                rows = slice(b * DEC_SEQ, (b + 1) * DEC_SEQ)
                cur = u_ref[rows, cols]
                ext = jnp.concatenate([state_ref[b, :, cols], cur], axis=0)
                d_ref[rows, cols] = window_diffs(ext, cur, float(w), w)

    for g in range(len(POOL_WINDOWS)):
        cols = slice(g * POOL_GROUP_WIDTH, (g + 1) * POOL_GROUP_WIDTH)
        y = jnp.dot(d_ref[:, cols].astype(BF16), w_ref[cols, :], preferred_element_type=F32)
        o_ref[:, cols] = (y * scale_ref[:, cols]).astype(BF16)


def _pool(z, state_halo, w_pool, scale, layer):
    halo_blocks_per_tile = TP_POOL // HALO
    vmem = (2 * TP_POOL * POOL_WIDTH * 4 + 2 * HALO * POOL_WIDTH * 4 + 2 * BATCH_PER_POOL_TILE * HALO * POOL_WIDTH * 4
            + 2 * POOL_WIDTH * POOL_GROUP_WIDTH * 2 + 2 * TP_POOL * POOL_WIDTH * 2 + TP_POOL * POOL_WIDTH * 4
            + (8 << 20))
    return pl.pallas_call(
        _pool_kernel,
        grid=(M_ALL // TP_POOL,),
        in_specs=[
            pl.BlockSpec((TP_POOL, POOL_WIDTH), lambda i: (i, Z_U0 // POOL_WIDTH)),
            pl.BlockSpec((HALO, POOL_WIDTH),
                         lambda i: (jnp.maximum(i * halo_blocks_per_tile - 1, 0), Z_U0 // POOL_WIDTH)),
            pl.BlockSpec((BATCH_PER_POOL_TILE, HALO, POOL_WIDTH),
                         lambda i: (jnp.maximum(i - POOL_TILES_PROMPT, 0), 0, 0)),
            pl.BlockSpec((None, POOL_WIDTH, POOL_GROUP_WIDTH), lambda i: (layer, 0, 0)),
            pl.BlockSpec((1, POOL_WIDTH), lambda i: (0, 0)),
        ],
        out_specs=pl.BlockSpec((TP_POOL, POOL_WIDTH), lambda i: (i, 0)),
        out_shape=jax.ShapeDtypeStruct((M_ALL, POOL_WIDTH), BF16),
        scratch_shapes=[pltpu.VMEM((TP_POOL, POOL_WIDTH), F32)],
        compiler_params=_params(vmem, 1),
        name="pool",
    )(z, z, state_halo, w_pool, scale)


def _mixer_out_kernel(a_ref, p_ref, xp_ref, xs_ref, wa_ref, wp_ref, o_ref):
    y = jnp.dot(a_ref[...], wa_ref[...], preferred_element_type=F32)
    y = y + jnp.dot(p_ref[...], wp_ref[...], preferred_element_type=F32)
    o_ref[...] = jnp.where(pl.program_id(0) < PROMPT_TILES, xp_ref[...], xs_ref[...]) + y


def _mixer_out(attn, pooled, x_p, x_s, w_out, layer):
    vmem = 2 * (2 * TM * ATTN_WIDTH * 2 + D_MODEL * TN_OUT * 2 + 3 * TM * TN_OUT * 4)
    return pl.pallas_call(
        _mixer_out_kernel,
        grid=(ROW_TILES, D_MODEL // TN_OUT),
        in_specs=[
            pl.BlockSpec((TM, ATTN_WIDTH), lambda i, j: (i, 0)),
            pl.BlockSpec((TM, POOL_WIDTH), lambda i, j: (i, 0)),
        ] + _x_specs(TN_OUT, lambda i, j: j) + [
            pl.BlockSpec((None, ATTN_WIDTH, TN_OUT), lambda i, j: (layer, 0, j)),
            pl.BlockSpec((None, POOL_WIDTH, TN_OUT), lambda i, j: (layer, 1, j)),
        ],
        out_specs=pl.BlockSpec((TM, TN_OUT), lambda i, j: (i, j)),
        out_shape=jax.ShapeDtypeStruct((M_ALL, D_MODEL), F32),
        compiler_params=_params(vmem, 2),
        name="mixer_out",
    )(attn, pooled, x_p, x_s, w_out, w_out)


def _mem_kv_kernel(m_ref, g_ref, wk_ref, wv_ref, kg_ref, k_ref, v_ref, h_ref):
    _norm_rows_to(m_ref, g_ref, h_ref, N_MEM)
    h = h_ref[...]
    k = jnp.dot(h, wk_ref[...], preferred_element_type=F32)
    for hd in range(MEM_HEADS):
        cols = slice(hd * MEM_HEAD_DIM, (hd + 1) * MEM_HEAD_DIM)
        k_ref[:, cols] = _rms_rows(k[:, cols], kg_ref[...])
    v_ref[...] = jnp.dot(h, wv_ref[...], preferred_element_type=F32)


def _mem_kv(mem, g, w_k, w_v, k_gain, layer):
    vmem = 2 * (N_MEM * D_MODEL * 4 + 2 * D_MODEL * MEM_WIDTH * 2 + 2 * N_MEM * MEM_WIDTH * 4) + N_MEM * D_MODEL * 2
    full = lambda shape: pl.BlockSpec(shape, lambda i: (0,) * len(shape))
    weight = pl.BlockSpec((None, D_MODEL, MEM_WIDTH), lambda i: (layer, 0, 0))
    return pl.pallas_call(
        _mem_kv_kernel,
        grid=(1,),
        in_specs=[full((N_MEM, D_MODEL)), full((1, D_MODEL)), weight, weight, full((1, MEM_HEAD_DIM))],
        out_specs=[full((N_MEM, MEM_WIDTH)), full((N_MEM, MEM_WIDTH))],
        out_shape=[jax.ShapeDtypeStruct((N_MEM, MEM_WIDTH), F32)] * 2,
        scratch_shapes=[pltpu.VMEM((N_MEM, D_MODEL), BF16)],
        compiler_params=_params(vmem, 1),
        name="mem_kv",
    )(mem, g, w_k, w_v, k_gain)


CROSS_TILES_PROMPT = M_PROMPT // TM_CROSS
BATCH_PER_CROSS_TILE = TM_CROSS // DEC_SEQ


def _cross_heads(q, mk_ref, mv_ref, b, qg, oc_ref, rows):
    for hd in range(MEM_HEADS):
        cols = slice(hd * MEM_HEAD_DIM, (hd + 1) * MEM_HEAD_DIM)
        qn = _rms_rows(q[:, cols], qg).astype(BF16)
        s = lax.dot_general(qn, mk_ref[b, :, cols].astype(BF16), (((1,), (1,)), ((), ())),
                            preferred_element_type=F32) * (MEM_HEAD_DIM ** -0.5)
        p = jnp.exp(s - jnp.max(s, axis=-1, keepdims=True))
        den = jnp.sum(p, axis=-1, keepdims=True)
        o = jnp.dot(p.astype(BF16), mv_ref[b, :, cols].astype(BF16), preferred_element_type=F32) / den
        oc_ref[rows, cols] = o.astype(BF16)


def _cross_kernel(x_ref, g_ref, wq_ref, qg_ref, mkp_ref, mvp_ref, mks_ref, mvs_ref, wo_ref,
                  o_ref, h_ref, q_ref, oc_ref):
    i = pl.program_id(0)
    _norm_rows_to(x_ref, g_ref, h_ref, TM_CROSS)
    q_ref[...] = jnp.dot(h_ref[...], wq_ref[...], preferred_element_type=F32)

    @pl.when(i < CROSS_TILES_PROMPT)
    def _():
        _cross_heads(q_ref[...], mkp_ref, mvp_ref, 0, qg_ref[...], oc_ref, slice(0, TM_CROSS))

    @pl.when(i >= CROSS_TILES_PROMPT)
    def _():
        for b in range(BATCH_PER_CROSS_TILE):
            rows = slice(b * DEC_SEQ, (b + 1) * DEC_SEQ)
            _cross_heads(q_ref[rows, :], mks_ref, mvs_ref, b, qg_ref[...], oc_ref, rows)

    o_ref[...] = x_ref[...] + jnp.dot(oc_ref[...], wo_ref[...], preferred_element_type=F32)


def _cross(x, g, w_q, q_gain, mk_p, mv_p, mk_s, mv_s, w_o, layer):
    const = lambda shape: pl.BlockSpec(shape, lambda i: (0,) * len(shape))
    once = lambda shape: pl.BlockSpec((None,) + shape, lambda i: (layer, 0, 0), pipeline_mode=pl.Buffered(1))
    sample_mem = pl.BlockSpec((BATCH_PER_CROSS_TILE, N_MEM, MEM_WIDTH),
                              lambda i: (jnp.maximum(i - CROSS_TILES_PROMPT, 0), 0, 0))
    vmem = (4 * TM_CROSS * D_MODEL * 4 + 2 * D_MODEL * MEM_WIDTH * 2
            + 4 * (1 + BATCH_PER_CROSS_TILE) * N_MEM * MEM_WIDTH * 4
            + TM_CROSS * D_MODEL * 2 + TM_CROSS * MEM_WIDTH * 6 + (6 << 20))
    return pl.pallas_call(
        _cross_kernel,
        grid=(M_ALL // TM_CROSS,),
        in_specs=[
            pl.BlockSpec((TM_CROSS, D_MODEL), lambda i: (i, 0)),
            const((1, D_MODEL)),
            once((D_MODEL, MEM_WIDTH)),
            const((1, MEM_HEAD_DIM)),
            const(mk_p.shape), const(mv_p.shape), sample_mem, sample_mem,
            once((MEM_WIDTH, D_MODEL)),
        ],
        out_specs=pl.BlockSpec((TM_CROSS, D_MODEL), lambda i: (i, 0)),
        out_shape=jax.ShapeDtypeStruct((M_ALL, D_MODEL), F32),
        scratch_shapes=[pltpu.VMEM((TM_CROSS, D_MODEL), BF16), pltpu.VMEM((TM_CROSS, MEM_WIDTH), F32),
                        pltpu.VMEM((TM_CROSS, MEM_WIDTH), BF16)],
        compiler_params=_params(vmem, 1),
        name="cross",
    )(x, g, w_q, q_gain, mk_p, mv_p, mk_s, mv_s, w_o)


GROUP = TF_SAMPLE
GROUPS_PER_TILE = TF_FFN // GROUP
GROUPS_LAST = TF_LAST // GROUP
N_GROUPS = D_FF // GROUP
SAMPLE_PAIR = 2


def _swiglu_groups(gate_up, n_groups):
    acts = []
    for k in range(n_groups):
        gate = gate_up[:, 2 * k * GROUP:(2 * k + 1) * GROUP]
        up = gate_up[:, (2 * k + 1) * GROUP:(2 * k + 2) * GROUP]
        acts.append(gate * jax.nn.sigmoid(gate) * up)
    return (acts[0] if n_groups == 1 else jnp.concatenate(acts, axis=1)).astype(BF16)


def _ffn_prompt_kernel(x_ref, g_ref, wgu_ref, wd_ref, o_ref, h_ref):
    f = pl.program_id(1)

    @pl.when(f == 0)
    def _():
        _norm_rows_to(x_ref, g_ref, h_ref, TM, copy_ref=o_ref)

    def accumulate(n_groups):
        gate_up = jnp.dot(h_ref[...], wgu_ref[:, :2 * n_groups * GROUP], preferred_element_type=F32)
        act = _swiglu_groups(gate_up, n_groups)
        o_ref[...] += jnp.dot(act, wd_ref[:n_groups * GROUP, :], preferred_element_type=F32)

    pl.when(f < NF_FFN - 1)(functools.partial(accumulate, GROUPS_PER_TILE))
    pl.when(f == NF_FFN - 1)(functools.partial(accumulate, GROUPS_LAST))


def _ffn_prompt(x, g, w_gate_up, w_down):
    vmem = 3 * TM * D_MODEL * 4 + TM * D_MODEL * 2 + 2 * 3 * D_MODEL * TF_FFN * 2 + 4 * TM * TF_FFN * 4
    return pl.pallas_call(
        _ffn_prompt_kernel,
        grid=(PROMPT_TILES, NF_FFN),
        in_specs=[
            pl.BlockSpec((TM, D_MODEL), lambda i, f: (i, 0)),
            pl.BlockSpec((1, D_MODEL), lambda i, f: (0, 0)),
            pl.BlockSpec((D_MODEL, 2 * TF_FFN), lambda i, f: (0, f)),
            pl.BlockSpec((TF_FFN, D_MODEL), lambda i, f: (f, 0)),
        ],
        out_specs=pl.BlockSpec((TM, D_MODEL), lambda i, f: (i, 0), pipeline_mode=pl.Buffered(1)),
        out_shape=jax.ShapeDtypeStruct((M_PROMPT, D_MODEL), F32),
        scratch_shapes=[pltpu.VMEM((TM, D_MODEL), BF16)],
        compiler_params=_params(vmem, 2),
        name="ffn_prompt",
    )(x, g, w_gate_up, w_down)


def _ffn_sample_kernel(x_ref, g_ref, wg_ref, wu_ref, wd_ref, o_ref, wgu_out, wd_out, h_ref, act_ref, wd_pair_ref):
    f = pl.program_id(0)

    @pl.when(f == 0)
    def _():
        _norm_rows_to(x_ref, g_ref, h_ref, TM, copy_ref=o_ref)

    wgu_out[:, :GROUP] = wg_ref[...].astype(BF16)
    wgu_out[:, GROUP:] = wu_ref[...].astype(BF16)
    wd = wd_ref[...].astype(BF16)
    wd_out[...] = wd
    act = _swiglu_groups(jnp.dot(h_ref[...], wgu_out[...], preferred_element_type=F32), 1)

    @pl.when(f % SAMPLE_PAIR == 0)
    def _():
        act_ref[:, :GROUP] = act
        wd_pair_ref[:GROUP, :] = wd

    @pl.when(f % SAMPLE_PAIR == 1)
    def _():
        act_ref[:, GROUP:] = act
        wd_pair_ref[GROUP:, :] = wd
        o_ref[...] += jnp.dot(act_ref[...], wd_pair_ref[...], preferred_element_type=F32)


def _ffn_sample(x, g, w_gate, w_up, w_down, layer):
    assert N_GROUPS % SAMPLE_PAIR == 0
    vmem = (2 * TM * D_MODEL * 4 + TM * D_MODEL * 2 + 2 * 3 * D_MODEL * GROUP * 6 + 4 * TM * GROUP * 4
            + SAMPLE_PAIR * GROUP * (D_MODEL + TM) * 2)
    once = pl.Buffered(1)
    col_f32 = pl.BlockSpec((None, D_MODEL, GROUP), lambda f: (layer, 0, f))
    return pl.pallas_call(
        _ffn_sample_kernel,
        grid=(N_GROUPS,),
        in_specs=[
            pl.BlockSpec((TM, D_MODEL), lambda f: (PROMPT_TILES, 0), pipeline_mode=once),
            pl.BlockSpec((1, D_MODEL), lambda f: (0, 0)),
            col_f32, col_f32,
            pl.BlockSpec((None, GROUP, D_MODEL), lambda f: (layer, f, 0)),
        ],
        out_specs=[
            pl.BlockSpec((TM, D_MODEL), lambda f: (0, 0), pipeline_mode=once),
            pl.BlockSpec((D_MODEL, 2 * GROUP), lambda f: (0, f)),
            pl.BlockSpec((GROUP, D_MODEL), lambda f: (f, 0)),
        ],
        out_shape=[jax.ShapeDtypeStruct((M_SAMPLE, D_MODEL), F32),
                   jax.ShapeDtypeStruct((D_MODEL, 2 * D_FF), BF16),
                   jax.ShapeDtypeStruct((D_FF, D_MODEL), BF16)],
        scratch_shapes=[pltpu.VMEM((TM, D_MODEL), BF16), pltpu.VMEM((TM, SAMPLE_PAIR * GROUP), BF16),
                        pltpu.VMEM((SAMPLE_PAIR * GROUP, D_MODEL), BF16)],
        compiler_params=_params(vmem, 1),
        name="ffn_sample",
    )(x, g, w_gate, w_up, w_down)


def _row(v):
    return v.astype(F32).reshape(1, -1)


def kernel(x_prompt, x_sample, cache_attn_k, cache_attn_v, state_pool, cache_mem_k, cache_mem_v, mem_prompt,
           g_mix, w_in, q_norm, k_norm, attn_sinks, w_pool, pool_scale, w_out,
           g_cross, g_mem, w_q_mem, w_k_mem, w_v_mem, q_norm_mem, k_norm_mem, w_o_mem,
           g_ffn, w_gate, w_up, w_down):
    x_p, x_s = x_prompt.reshape(M_PROMPT, D_MODEL), x_sample.reshape(M_SAMPLE, D_MODEL)
    mem = mem_prompt.reshape(N_MEM, D_MODEL)

    w_in_b = _cast_w_in(w_in)
    w_out_b, w_q_b, w_k_b, w_v_b, w_o_b = (_cast_rows(w) for w in (w_out, w_q_mem, w_k_mem, w_v_mem, w_o_mem))
    w_pool_b = _cast_rows(w_pool.reshape(DEPTH, POOL_WIDTH, POOL_GROUP_WIDTH))

    ak_p, av_p, pl_p, mk_p, mv_p, ak_s, av_s, pl_s = ([] for _ in range(8))
    for l in range(DEPTH):
        z = _mixer_in(x_p, x_s, _row(g_mix[l]), w_in_b, l)
        kn = _k_norm(z, _row(jnp.tile(k_norm[l], 2)))
        v = z[:, Z_V0:Z_V0 + KV_WIDTH]
        kn_s = kn[M_PROMPT:].reshape(DEC_BATCH, DEC_SEQ, KV_WIDTH)
        v_s = v[M_PROMPT:].reshape(DEC_BATCH, DEC_SEQ, KV_WIDTH)
        k_full = jnp.concatenate([cache_attn_k[l].reshape(DEC_BATCH, WINDOW, KV_WIDTH), kn_s], axis=1)
        v_full = jnp.concatenate([cache_attn_v[l].reshape(DEC_BATCH, WINDOW, KV_WIDTH), v_s], axis=1)
        k_band = jnp.concatenate([kn[:M_PROMPT], k_full.reshape(DEC_BATCH * BAND, KV_WIDTH)], axis=0)
        v_band = jnp.concatenate([v[:M_PROMPT], v_full.reshape(DEC_BATCH * BAND, KV_WIDTH)], axis=0)
        bias, sink_e, sink_o = _swa_constants(attn_sinks[l])
        q_gain2 = _row(jnp.tile(q_norm[l], 2)) * (HEAD_DIM ** -0.5)
        attn = _swa(z, k_band, v_band, q_gain2, bias, sink_e, sink_o)

        state_halo = jnp.pad(state_pool[l].astype(F32), ((0, 0), (HALO - POOL_STATE, 0), (0, 0)))
        pooled = _pool(z, state_halo, w_pool_b, _row(pool_scale[l]), l)
        x = _mixer_out(attn, pooled, x_p, x_s, w_out_b, l)

        ak_p.append(kn[M_PROMPT - WINDOW:M_PROMPT].reshape(1, WINDOW, N_KV_HEADS, HEAD_DIM))
        av_p.append(v[M_PROMPT - WINDOW:M_PROMPT].reshape(1, WINDOW, N_KV_HEADS, HEAD_DIM))
        pl_p.append(z[M_PROMPT - POOL_STATE:M_PROMPT, Z_U0:Z_U0 + POOL_WIDTH].reshape(1, POOL_STATE, POOL_WIDTH))
        ak_s.append(k_full[:, -WINDOW:].reshape(DEC_BATCH, WINDOW, N_KV_HEADS, HEAD_DIM))
        av_s.append(v_full[:, -WINDOW:].reshape(DEC_BATCH, WINDOW, N_KV_HEADS, HEAD_DIM))
        u_s = z[M_PROMPT:, Z_U0:Z_U0 + POOL_WIDTH].reshape(DEC_BATCH, DEC_SEQ, POOL_WIDTH)
        pl_s.append(jnp.concatenate([state_pool[l].astype(F32), u_s], axis=1)[:, -POOL_STATE:])

        mk, mv = _mem_kv(mem, _row(g_mem[l]), w_k_b, w_v_b, _row(k_norm_mem[l]), l)
        mk_p.append(mk.reshape(1, N_MEM, MEM_HEADS, MEM_HEAD_DIM))
        mv_p.append(mv.reshape(1, N_MEM, MEM_HEADS, MEM_HEAD_DIM))
        x = _cross(x, _row(g_cross[l]), w_q_b, _row(q_norm_mem[l]),
                   mk.reshape(1, N_MEM, MEM_WIDTH), mv.reshape(1, N_MEM, MEM_WIDTH),
                   cache_mem_k[l].reshape(DEC_BATCH, N_MEM, MEM_WIDTH).astype(F32),
                   cache_mem_v[l].reshape(DEC_BATCH, N_MEM, MEM_WIDTH).astype(F32),
                   w_o_b, l)

        x_s, w_gate_up_b, w_down_b = _ffn_sample(x, _row(g_ffn[l]), w_gate, w_up, w_down, l)
        x_p = _ffn_prompt(x, _row(g_ffn[l]), w_gate_up_b, w_down_b)

    return (x_p.reshape(1, SEQ, D_MODEL), x_s.reshape(DEC_BATCH, DEC_SEQ, D_MODEL),
            jnp.stack(ak_p), jnp.stack(av_p), jnp.stack(pl_p), jnp.stack(mk_p), jnp.stack(mv_p),
            jnp.stack(ak_s), jnp.stack(av_s), jnp.stack(pl_s))
```

```python
import functools

import jax
import jax.numpy as jnp
from jax import lax
from jax.experimental import pallas as pl
from jax.experimental.pallas import tpu as pltpu

F32 = jnp.float32
BF16 = jnp.bfloat16

D_MODEL = 4096
SEQ = 8192
DEPTH = 2
DEC_BATCH = 8
DEC_SEQ = 64
PAST_LEN = 4096
CHUNK = 64
WINDOW = 128
ATTN_WIDTH = D_MODEL // 2
POOL_WIDTH = D_MODEL - ATTN_WIDTH
HEAD_DIM = 64
N_HEADS = ATTN_WIDTH // HEAD_DIM
N_KV_HEADS = N_HEADS // 8
KV_WIDTH = N_KV_HEADS * HEAD_DIM
POOL_WINDOWS = (2, 4, 8, 16)
POOL_GROUP_WIDTH = POOL_WIDTH // len(POOL_WINDOWS)
POOL_STATE = max(POOL_WINDOWS) - 1
IN_WIDTH = ATTN_WIDTH + 2 * KV_WIDTH + POOL_WIDTH
N_MEM = 256
MEM_HEADS = 4
MEM_HEAD_DIM = 128
MEM_WIDTH = MEM_HEADS * MEM_HEAD_DIM
D_FF = -(-8 * D_MODEL // (3 * 256)) * 256
EPS = 1e-6
NEG_INF = -1e30

M_PROMPT = SEQ
M_SAMPLE = DEC_BATCH * DEC_SEQ
M_ALL = M_PROMPT + M_SAMPLE
N_CHUNKS_PROMPT = M_PROMPT // CHUNK
N_CHUNKS = M_ALL // CHUNK
BAND = WINDOW + CHUNK
BAND_BLOCKS = BAND // CHUNK

Z_U0 = ATTN_WIDTH
Z_K0 = ATTN_WIDTH + POOL_WIDTH
Z_V0 = Z_K0 + KV_WIDTH

LANES = 128
SUBLANES = 8
BF16_TILE_ROWS = 16
HALO = BF16_TILE_ROWS
VMEM_LIMIT_CAP = 60000 * 1024
CAST_BLOCK_BYTES = 8 << 20

TM = 512
PROMPT_TILES = M_PROMPT // TM
ROW_TILES = M_ALL // TM
TN_IN = 768
TN_OUT = 1024
TF_FFN = 512
NF_FFN = -(-D_FF // TF_FFN)
TF_LAST = D_FF - (NF_FFN - 1) * TF_FFN
TF_SAMPLE = 256
TM_CROSS = 256
TM_KN = 512
TP_POOL = 256
NORM_ROWS = BF16_TILE_ROWS
NORM_UNROLL = 4
W_IN_COL_BLOCK = KV_WIDTH


def _params(vmem_bytes, n_axes):
    limit = min(int(vmem_bytes * 1.2) + (6 << 20), VMEM_LIMIT_CAP)
    return pltpu.CompilerParams(dimension_semantics=("arbitrary",) * n_axes, vmem_limit_bytes=limit)


def _rms_rows(x, g):
    return x * lax.rsqrt(jnp.mean(x * x, axis=-1, keepdims=True) + EPS) * g


def _norm_rows_to(x_ref, g_ref, h_ref, n_rows, copy_ref=None):
    def body(r, carry):
        rows = pl.ds(pl.multiple_of(r * NORM_ROWS, NORM_ROWS), NORM_ROWS)
        x = x_ref[rows, :]
        h_ref[rows, :] = _rms_rows(x, g_ref[...]).astype(BF16)
        if copy_ref is not None:
            copy_ref[rows, :] = x
        return carry

    lax.fori_loop(0, n_rows // NORM_ROWS, body, 0, unroll=NORM_UNROLL)


def _half_lane_rms(x, gain2):
    lo = lax.broadcasted_iota(jnp.int32, (1, LANES), 1) < HEAD_DIM
    x2 = x * x
    ss_lo = jnp.sum(jnp.where(lo, x2, 0.0), axis=-1, keepdims=True)
    ss_hi = jnp.sum(jnp.where(lo, 0.0, x2), axis=-1, keepdims=True)
    r = jnp.where(lo, lax.rsqrt(ss_lo / HEAD_DIM + EPS), lax.rsqrt(ss_hi / HEAD_DIM + EPS))
    return x * r * gain2


def _cast_kernel(w_ref, o_ref):
    o_ref[...] = w_ref[...].astype(BF16)


def _cast_rows(w):
    depth, rows, cols = w.shape
    tr = max(t for t in range(BF16_TILE_ROWS, rows + 1, BF16_TILE_ROWS)
             if rows % t == 0 and t * cols * 4 <= CAST_BLOCK_BYTES)
    spec = pl.BlockSpec((None, tr, cols), lambda l, i: (l, i, 0))
    return pl.pallas_call(
        _cast_kernel, grid=(depth, rows // tr), in_specs=[spec], out_specs=spec,
        out_shape=jax.ShapeDtypeStruct(w.shape, BF16),
        compiler_params=_params(2 * tr * cols * 6, 2), name="cast_rows",
    )(w)


def _w_in_source_block(j):
    q_blocks, u_blocks = ATTN_WIDTH // W_IN_COL_BLOCK, POOL_WIDTH // W_IN_COL_BLOCK
    kv_blocks = 2 * KV_WIDTH // W_IN_COL_BLOCK
    return jnp.where(j < q_blocks, j, jnp.where(j < q_blocks + u_blocks, j + kv_blocks, j - u_blocks))


def _cast_w_in(w):
    depth = w.shape[0]
    return pl.pallas_call(
        _cast_kernel, grid=(depth, IN_WIDTH // W_IN_COL_BLOCK),
        in_specs=[pl.BlockSpec((None, D_MODEL, W_IN_COL_BLOCK), lambda l, j: (l, 0, _w_in_source_block(j)))],
        out_specs=pl.BlockSpec((None, D_MODEL, W_IN_COL_BLOCK), lambda l, j: (l, 0, j)),
        out_shape=jax.ShapeDtypeStruct(w.shape, BF16),
        compiler_params=_params(2 * D_MODEL * W_IN_COL_BLOCK * 6, 2), name="cast_w_in",
    )(w)


def _mixer_in_kernel(xp_ref, xs_ref, g_ref, w_ref, z_ref, h_ref):
    i, j = pl.program_id(0), pl.program_id(1)

    @pl.when((j == 0) & (i < PROMPT_TILES))
    def _():
        _norm_rows_to(xp_ref, g_ref, h_ref, TM)

    @pl.when((j == 0) & (i >= PROMPT_TILES))
    def _():
        _norm_rows_to(xs_ref, g_ref, h_ref, TM)

    z_ref[...] = jnp.dot(h_ref[...], w_ref[...], preferred_element_type=F32)


def _x_specs(width, col):
    return [
        pl.BlockSpec((TM, width), lambda i, j: (jnp.minimum(i, PROMPT_TILES - 1), col(i, j))),
        pl.BlockSpec((TM, width), lambda i, j: (0, jnp.where(i >= PROMPT_TILES, col(i, j), 0))),
    ]


def _mixer_in(x_p, x_s, g, w, layer):
    vmem = 4 * TM * D_MODEL * 4 + TM * D_MODEL * 2 + 2 * D_MODEL * TN_IN * 2 + 2 * TM * TN_IN * 4
    return pl.pallas_call(
        _mixer_in_kernel,
        grid=(ROW_TILES, IN_WIDTH // TN_IN),
        in_specs=_x_specs(D_MODEL, lambda i, j: 0) + [
            pl.BlockSpec((1, D_MODEL), lambda i, j: (0, 0)),
            pl.BlockSpec((None, D_MODEL, TN_IN), lambda i, j: (layer, 0, j)),
        ],
        out_specs=pl.BlockSpec((TM, TN_IN), lambda i, j: (i, j)),
        out_shape=jax.ShapeDtypeStruct((M_ALL, IN_WIDTH), F32),
        scratch_shapes=[pltpu.VMEM((TM, D_MODEL), BF16)],
        compiler_params=_params(vmem, 2),
        name="mixer_in",
    )(x_p, x_s, g, w)


def _k_norm_kernel(k_ref, g_ref, o_ref):
    for t in range(KV_WIDTH // LANES):
        cols = slice(t * LANES, (t + 1) * LANES)
        o_ref[:, cols] = _half_lane_rms(k_ref[:, cols], g_ref[...])


def _k_norm(z, gain2):
    vmem = 4 * TM_KN * KV_WIDTH * 4
    return pl.pallas_call(
        _k_norm_kernel,
        grid=(M_ALL // TM_KN,),
        in_specs=[
            pl.BlockSpec((TM_KN, KV_WIDTH), lambda i: (i, Z_K0 // KV_WIDTH)),
            pl.BlockSpec((1, LANES), lambda i: (0, 0)),
        ],
        out_specs=pl.BlockSpec((TM_KN, KV_WIDTH), lambda i: (i, 0)),
        out_shape=jax.ShapeDtypeStruct((M_ALL, KV_WIDTH), F32),
        compiler_params=_params(vmem, 1),
        name="k_norm",
    )(z, gain2)


PAIRS_PER_KV = N_HEADS // N_KV_HEADS // 2
QUERIES_PER_KV = PAIRS_PER_KV * CHUNK
N_PAIRS = N_HEADS // 2


def _swa_kernel(q_ref, k0_ref, k1_ref, k2_ref, v0_ref, v1_ref, v2_ref,
                qg_ref, bias_ref, sink_e_ref, sink_o_ref, o_ref):
    c = pl.program_id(0)
    lo = lax.broadcasted_iota(jnp.int32, (1, LANES), 1) < HEAD_DIM

    q = jnp.concatenate([q_ref[:, p * LANES:(p + 1) * LANES] for p in range(N_PAIRS)], axis=0)

    sel_row = lax.broadcasted_iota(jnp.int32, (SUBLANES, LANES), 0)
    sel_lo = lax.broadcasted_iota(jnp.int32, (SUBLANES, LANES), 1) < HEAD_DIM
    sel = jnp.where(((sel_row == 0) & sel_lo) | ((sel_row == 1) & ~sel_lo), 1.0, 0.0).astype(BF16)
    q2 = q * q
    q2_hi = q2.astype(BF16)
    q2_lo = (q2 - q2_hi.astype(F32)).astype(BF16)
    nt = (((1,), (1,)), ((), ()))
    ss = (lax.dot_general(sel, q2_hi, nt, preferred_element_type=F32)
          + lax.dot_general(sel, q2_lo, nt, preferred_element_type=F32))
    r = lax.rsqrt(ss / HEAD_DIM + EPS)
    qg = (q * qg_ref[...]).astype(BF16)

    kb = jnp.concatenate([k0_ref[...], k1_ref[...], k2_ref[...]], axis=0)
    vb = jnp.concatenate([v0_ref[...], v1_ref[...], v2_ref[...],
                          jnp.zeros((KV_WIDTH - BAND, KV_WIDTH), F32)], axis=0)
    v_t = vb.T

    j_min = jnp.where(c < BAND_BLOCKS - 1, WINDOW - CHUNK * c, 0)
    valid = lax.broadcasted_iota(jnp.int32, (BAND, 1), 0) >= j_min

    def head_softmax(s_raw, r_row, bias, sink):
        s = jnp.where(valid, s_raw * r_row - bias, NEG_INF)
        m = jnp.maximum(jnp.max(s, axis=0, keepdims=True), sink)
        p = jnp.exp(s - m)
        return p.astype(BF16), jnp.sum(p, axis=0, keepdims=True) + jnp.exp(sink - m)

    for kv_tile in range(KV_WIDTH // LANES):
        kt = kb[:, kv_tile * LANES:(kv_tile + 1) * LANES]
        kt_r = pltpu.roll(kt, HEAD_DIM, axis=1)
        for par in range(2):
            hk = 2 * kv_tile + par
            k_e = jnp.where(lo, kt_r if par else kt, 0.0)
            k_o = jnp.where(lo, 0.0, kt if par else kt_r)
            kk = jnp.concatenate([k_e, k_o], axis=0).astype(BF16)
            cols = slice(hk * QUERIES_PER_KV, (hk + 1) * QUERIES_PER_KV)
            s_t = lax.dot_general(kk, qg[cols], nt, preferred_element_type=F32)
            bias = bias_ref[hk]
            p_e, d_e = head_softmax(s_t[:BAND], r[0:1, cols], bias[:BAND], sink_e_ref[hk])
            p_o, d_o = head_softmax(s_t[BAND:], r[1:2, cols], bias[BAND:], sink_o_ref[hk])
            v_h = v_t[hk * HEAD_DIM:(hk + 1) * HEAD_DIM, :BAND].astype(BF16)
            o_e = jnp.dot(v_h, p_e, preferred_element_type=F32) / d_e
            o_o = jnp.dot(v_h, p_o, preferred_element_type=F32) / d_o
            o = jnp.concatenate([o_e, o_o], axis=0).T
            for pr in range(PAIRS_PER_KV):
                pair = hk * PAIRS_PER_KV + pr
                o_ref[:, pair * LANES:(pair + 1) * LANES] = o[pr * CHUNK:(pr + 1) * CHUNK].astype(BF16)


def _band_block(c, j):
    prompt = jnp.maximum(c - (BAND_BLOCKS - 1) + j, 0)
    sample = N_CHUNKS_PROMPT + (c - N_CHUNKS_PROMPT) * BAND_BLOCKS + j
    return jnp.where(c < N_CHUNKS_PROMPT, prompt, sample)


def _swa(z, k_band, v_band, q_gain2, bias, sink_e, sink_o):
    band_specs = [pl.BlockSpec((CHUNK, KV_WIDTH), functools.partial(lambda c, j: (_band_block(c, j), 0), j=j))
                  for j in range(BAND_BLOCKS)]
    const3 = lambda a: pl.BlockSpec(a.shape, lambda c: (0, 0, 0))
    vmem = (2 * CHUNK * ATTN_WIDTH * 4 + 12 * CHUNK * KV_WIDTH * 4 + 2 * bias.size * 4
            + 4 * N_KV_HEADS * SUBLANES * QUERIES_PER_KV * 4 + 2 * CHUNK * ATTN_WIDTH * 2 + (8 << 20))
    return pl.pallas_call(
        _swa_kernel,
        grid=(N_CHUNKS,),
        in_specs=[pl.BlockSpec((CHUNK, ATTN_WIDTH), lambda c: (c, 0))] + band_specs + band_specs + [
            pl.BlockSpec((1, LANES), lambda c: (0, 0)), const3(bias), const3(sink_e), const3(sink_o)],
        out_specs=pl.BlockSpec((CHUNK, ATTN_WIDTH), lambda c: (c, 0)),
        out_shape=jax.ShapeDtypeStruct((M_ALL, ATTN_WIDTH), BF16),
        compiler_params=_params(vmem, 1),
        name="swa",
    )(z, k_band, k_band, k_band, v_band, v_band, v_band, q_gain2, bias, sink_e, sink_o)


def _swa_constants(sinks):
    slopes = 2.0 ** (-8.0 * jnp.arange(1, N_HEADS + 1, dtype=F32) / N_HEADS)
    dist = jnp.abs(jnp.arange(CHUNK)[None, :] + WINDOW - jnp.arange(BAND)[:, None]).astype(F32)
    head = (jnp.arange(N_KV_HEADS)[:, None, None] * (2 * PAIRS_PER_KV)
            + jnp.arange(2)[None, :, None] + 2 * jnp.arange(PAIRS_PER_KV)[None, None, :])
    bias = slopes[head][:, :, None, :, None] * dist[None, None, :, None, :]
    bias = bias.reshape(N_KV_HEADS, 2 * BAND, QUERIES_PER_KV)
    per_pair = sinks.astype(F32).reshape(N_KV_HEADS, PAIRS_PER_KV, 2)
    rows = jnp.repeat(per_pair, CHUNK, axis=1)
    return bias, rows[:, None, :, 0], rows[:, None, :, 1]


POOL_TILES_PROMPT = M_PROMPT // TP_POOL
BATCH_PER_POOL_TILE = TP_POOL // DEC_SEQ


def _trailing_sum(ext, w):
    s, k = ext, 1
    while k < w:
        s = s + pltpu.roll(s, k, axis=0)
        k *= 2
    return s


def _pool_kernel(u_ref, halo_ref, state_ref, w_ref, scale_ref, o_ref, d_ref):
    i = pl.program_id(0)

    def window_diffs(ext, cur, cnt, w):
        return _trailing_sum(ext, w)[HALO:] / cnt - cur

    @pl.when(i < POOL_TILES_PROMPT)
    def _():
        pos = i * TP_POOL + lax.broadcasted_iota(jnp.int32, (TP_POOL, 1), 0)
        for g, w in enumerate(POOL_WINDOWS):
            cols = slice(g * POOL_GROUP_WIDTH, (g + 1) * POOL_GROUP_WIDTH)
            cur = u_ref[:, cols]
            halo = jnp.where(i > 0, halo_ref[:, cols], 0.0)
            ext = jnp.concatenate([halo, cur], axis=0)
            cnt = jnp.minimum(pos + 1, w).astype(F32)
            d_ref[:, cols] = window_diffs(ext, cur, cnt, w)

    @pl.when(i >= POOL_TILES_PROMPT)
    def _():
        for g, w in enumerate(POOL_WINDOWS):
            cols = slice(g * POOL_GROUP_WIDTH, (g + 1) * POOL_GROUP_WIDTH)
            for b in range(BATCH_PER_POOL_TILE):
                rows = slice(b * DEC_SEQ, (b + 1) * DEC_SEQ)
                cur = u_ref[rows, cols]
                ext = jnp.concatenate([state_ref[b, :, cols], cur], axis=0)
                d_ref[rows, cols] = window_diffs(ext, cur, float(w), w)

    for g in range(len(POOL_WINDOWS)):
        cols = slice(g * POOL_GROUP_WIDTH, (g + 1) * POOL_GROUP_WIDTH)
        y = jnp.dot(d_ref[:, cols].astype(BF16), w_ref[cols, :], preferred_element_type=F32)
        o_ref[:, cols] = (y * scale_ref[:, cols]).astype(BF16)


def _pool(z, state_halo, w_pool, scale, layer):
    halo_blocks_per_tile = TP_POOL // HALO
    vmem = (2 * TP_POOL * POOL_WIDTH * 4 + 2 * HALO * POOL_WIDTH * 4 + 2 * BATCH_PER_POOL_TILE * HALO * POOL_WIDTH * 4
            + 2 * POOL_WIDTH * POOL_GROUP_WIDTH * 2 + 2 * TP_POOL * POOL_WIDTH * 2 + TP_POOL * POOL_WIDTH * 4
            + (8 << 20))
    return pl.pallas_call(
        _pool_kernel,
        grid=(M_ALL // TP_POOL,),
        in_specs=[
            pl.BlockSpec((TP_POOL, POOL_WIDTH), lambda i: (i, Z_U0 // POOL_WIDTH)),
            pl.BlockSpec((HALO, POOL_WIDTH),
                         lambda i: (jnp.maximum(i * halo_blocks_per_tile - 1, 0), Z_U0 // POOL_WIDTH)),
            pl.BlockSpec((BATCH_PER_POOL_TILE, HALO, POOL_WIDTH),
                         lambda i: (jnp.maximum(i - POOL_TILES_PROMPT, 0), 0, 0)),
            pl.BlockSpec((None, POOL_WIDTH, POOL_GROUP_WIDTH), lambda i: (layer, 0, 0)),
            pl.BlockSpec((1, POOL_WIDTH), lambda i: (0, 0)),
        ],
        out_specs=pl.BlockSpec((TP_POOL, POOL_WIDTH), lambda i: (i, 0)),
        out_shape=jax.ShapeDtypeStruct((M_ALL, POOL_WIDTH), BF16),
        scratch_shapes=[pltpu.VMEM((TP_POOL, POOL_WIDTH), F32)],
        compiler_params=_params(vmem, 1),
        name="pool",
    )(z, z, state_halo, w_pool, scale)


def _mixer_out_kernel(a_ref, p_ref, xp_ref, xs_ref, wa_ref, wp_ref, o_ref):
    y = jnp.dot(a_ref[...], wa_ref[...], preferred_element_type=F32)
    y = y + jnp.dot(p_ref[...], wp_ref[...], preferred_element_type=F32)
    o_ref[...] = jnp.where(pl.program_id(0) < PROMPT_TILES, xp_ref[...], xs_ref[...]) + y


def _mixer_out(attn, pooled, x_p, x_s, w_out, layer):
    vmem = 2 * (2 * TM * ATTN_WIDTH * 2 + D_MODEL * TN_OUT * 2 + 3 * TM * TN_OUT * 4)
    return pl.pallas_call(
        _mixer_out_kernel,
        grid=(ROW_TILES, D_MODEL // TN_OUT),
        in_specs=[
            pl.BlockSpec((TM, ATTN_WIDTH), lambda i, j: (i, 0)),
            pl.BlockSpec((TM, POOL_WIDTH), lambda i, j: (i, 0)),
        ] + _x_specs(TN_OUT, lambda i, j: j) + [
            pl.BlockSpec((None, ATTN_WIDTH, TN_OUT), lambda i, j: (layer, 0, j)),
            pl.BlockSpec((None, POOL_WIDTH, TN_OUT), lambda i, j: (layer, 1, j)),
        ],
        out_specs=pl.BlockSpec((TM, TN_OUT), lambda i, j: (i, j)),
        out_shape=jax.ShapeDtypeStruct((M_ALL, D_MODEL), F32),
        compiler_params=_params(vmem, 2),
        name="mixer_out",
    )(attn, pooled, x_p, x_s, w_out, w_out)


def _mem_kv_kernel(m_ref, g_ref, wk_ref, wv_ref, kg_ref, k_ref, v_ref, h_ref):
    _norm_rows_to(m_ref, g_ref, h_ref, N_MEM)
    h = h_ref[...]
    k = jnp.dot(h, wk_ref[...], preferred_element_type=F32)
    for hd in range(MEM_HEADS):
        cols = slice(hd * MEM_HEAD_DIM, (hd + 1) * MEM_HEAD_DIM)
        k_ref[:, cols] = _rms_rows(k[:, cols], kg_ref[...])
    v_ref[...] = jnp.dot(h, wv_ref[...], preferred_element_type=F32)


def _mem_kv(mem, g, w_k, w_v, k_gain, layer):
    vmem = 2 * (N_MEM * D_MODEL * 4 + 2 * D_MODEL * MEM_WIDTH * 2 + 2 * N_MEM * MEM_WIDTH * 4) + N_MEM * D_MODEL * 2
    full = lambda shape: pl.BlockSpec(shape, lambda i: (0,) * len(shape))
    weight = pl.BlockSpec((None, D_MODEL, MEM_WIDTH), lambda i: (layer, 0, 0))
    return pl.pallas_call(
        _mem_kv_kernel,
        grid=(1,),
        in_specs=[full((N_MEM, D_MODEL)), full((1, D_MODEL)), weight, weight, full((1, MEM_HEAD_DIM))],
        out_specs=[full((N_MEM, MEM_WIDTH)), full((N_MEM, MEM_WIDTH))],
        out_shape=[jax.ShapeDtypeStruct((N_MEM, MEM_WIDTH), F32)] * 2,
        scratch_shapes=[pltpu.VMEM((N_MEM, D_MODEL), BF16)],
        compiler_params=_params(vmem, 1),
        name="mem_kv",
    )(mem, g, w_k, w_v, k_gain)


CROSS_TILES_PROMPT = M_PROMPT // TM_CROSS
BATCH_PER_CROSS_TILE = TM_CROSS // DEC_SEQ


def _cross_heads(q, mk_ref, mv_ref, b, qg, oc_ref, rows):
    for hd in range(MEM_HEADS):
        cols = slice(hd * MEM_HEAD_DIM, (hd + 1) * MEM_HEAD_DIM)
        qn = _rms_rows(q[:, cols], qg).astype(BF16)
        s = lax.dot_general(qn, mk_ref[b, :, cols].astype(BF16), (((1,), (1,)), ((), ())),
                            preferred_element_type=F32) * (MEM_HEAD_DIM ** -0.5)
        p = jnp.exp(s - jnp.max(s, axis=-1, keepdims=True))
        den = jnp.sum(p, axis=-1, keepdims=True)
        o = jnp.dot(p.astype(BF16), mv_ref[b, :, cols].astype(BF16), preferred_element_type=F32) / den
        oc_ref[rows, cols] = o.astype(BF16)


def _cross_kernel(x_ref, g_ref, wq_ref, qg_ref, mkp_ref, mvp_ref, mks_ref, mvs_ref, wo_ref,
                  o_ref, h_ref, q_ref, oc_ref):
    i = pl.program_id(0)
    _norm_rows_to(x_ref, g_ref, h_ref, TM_CROSS)
    q_ref[...] = jnp.dot(h_ref[...], wq_ref[...], preferred_element_type=F32)

    @pl.when(i < CROSS_TILES_PROMPT)
    def _():
        _cross_heads(q_ref[...], mkp_ref, mvp_ref, 0, qg_ref[...], oc_ref, slice(0, TM_CROSS))

    @pl.when(i >= CROSS_TILES_PROMPT)
    def _():
        for b in range(BATCH_PER_CROSS_TILE):
            rows = slice(b * DEC_SEQ, (b + 1) * DEC_SEQ)
            _cross_heads(q_ref[rows, :], mks_ref, mvs_ref, b, qg_ref[...], oc_ref, rows)

    o_ref[...] = x_ref[...] + jnp.dot(oc_ref[...], wo_ref[...], preferred_element_type=F32)


def _cross(x, g, w_q, q_gain, mk_p, mv_p, mk_s, mv_s, w_o, layer):
    const = lambda shape: pl.BlockSpec(shape, lambda i: (0,) * len(shape))
    once = lambda shape: pl.BlockSpec((None,) + shape, lambda i: (layer, 0, 0), pipeline_mode=pl.Buffered(1))
    sample_mem = pl.BlockSpec((BATCH_PER_CROSS_TILE, N_MEM, MEM_WIDTH),
                              lambda i: (jnp.maximum(i - CROSS_TILES_PROMPT, 0), 0, 0))
    vmem = (4 * TM_CROSS * D_MODEL * 4 + 2 * D_MODEL * MEM_WIDTH * 2
            + 4 * (1 + BATCH_PER_CROSS_TILE) * N_MEM * MEM_WIDTH * 4
            + TM_CROSS * D_MODEL * 2 + TM_CROSS * MEM_WIDTH * 6 + (6 << 20))
    return pl.pallas_call(
        _cross_kernel,
        grid=(M_ALL // TM_CROSS,),
        in_specs=[
            pl.BlockSpec((TM_CROSS, D_MODEL), lambda i: (i, 0)),
            const((1, D_MODEL)),
            once((D_MODEL, MEM_WIDTH)),
            const((1, MEM_HEAD_DIM)),
            const(mk_p.shape), const(mv_p.shape), sample_mem, sample_mem,
            once((MEM_WIDTH, D_MODEL)),
        ],
        out_specs=pl.BlockSpec((TM_CROSS, D_MODEL), lambda i: (i, 0)),
        out_shape=jax.ShapeDtypeStruct((M_ALL, D_MODEL), F32),
        scratch_shapes=[pltpu.VMEM((TM_CROSS, D_MODEL), BF16), pltpu.VMEM((TM_CROSS, MEM_WIDTH), F32),
                        pltpu.VMEM((TM_CROSS, MEM_WIDTH), BF16)],
        compiler_params=_params(vmem, 1),
        name="cross",
    )(x, g, w_q, q_gain, mk_p, mv_p, mk_s, mv_s, w_o)


GROUP = TF_SAMPLE
GROUPS_PER_TILE = TF_FFN // GROUP
GROUPS_LAST = TF_LAST // GROUP
N_GROUPS = D_FF // GROUP


def _swiglu_groups(gate_up, n_groups):
    acts = []
    for k in range(n_groups):
        gate = gate_up[:, 2 * k * GROUP:(2 * k + 1) * GROUP]
        up = gate_up[:, (2 * k + 1) * GROUP:(2 * k + 2) * GROUP]
        acts.append(gate * jax.nn.sigmoid(gate) * up)
    return (acts[0] if n_groups == 1 else jnp.concatenate(acts, axis=1)).astype(BF16)


def _ffn_prompt_kernel(x_ref, g_ref, wgu_ref, wd_ref, o_ref, h_ref):
    f = pl.program_id(1)

    @pl.when(f == 0)
    def _():
        _norm_rows_to(x_ref, g_ref, h_ref, TM, copy_ref=o_ref)

    def accumulate(n_groups):
        gate_up = jnp.dot(h_ref[...], wgu_ref[:, :2 * n_groups * GROUP], preferred_element_type=F32)
        act = _swiglu_groups(gate_up, n_groups)
        o_ref[...] += jnp.dot(act, wd_ref[:n_groups * GROUP, :], preferred_element_type=F32)

    pl.when(f < NF_FFN - 1)(functools.partial(accumulate, GROUPS_PER_TILE))
    pl.when(f == NF_FFN - 1)(functools.partial(accumulate, GROUPS_LAST))


def _ffn_prompt(x, g, w_gate_up, w_down):
    vmem = 3 * TM * D_MODEL * 4 + TM * D_MODEL * 2 + 2 * 3 * D_MODEL * TF_FFN * 2 + 4 * TM * TF_FFN * 4
    return pl.pallas_call(
        _ffn_prompt_kernel,
        grid=(PROMPT_TILES, NF_FFN),
        in_specs=[
            pl.BlockSpec((TM, D_MODEL), lambda i, f: (i, 0)),
            pl.BlockSpec((1, D_MODEL), lambda i, f: (0, 0)),
            pl.BlockSpec((D_MODEL, 2 * TF_FFN), lambda i, f: (0, f)),
            pl.BlockSpec((TF_FFN, D_MODEL), lambda i, f: (f, 0)),
        ],
        out_specs=pl.BlockSpec((TM, D_MODEL), lambda i, f: (i, 0), pipeline_mode=pl.Buffered(1)),
        out_shape=jax.ShapeDtypeStruct((M_PROMPT, D_MODEL), F32),
        scratch_shapes=[pltpu.VMEM((TM, D_MODEL), BF16)],
        compiler_params=_params(vmem, 2),
        name="ffn_prompt",
    )(x, g, w_gate_up, w_down)


def _ffn_sample_kernel(x_ref, g_ref, wg_ref, wu_ref, wd_ref, o_ref, wgu_out, wd_out, h_ref):
    @pl.when(pl.program_id(0) == 0)
    def _():
        _norm_rows_to(x_ref, g_ref, h_ref, TM, copy_ref=o_ref)

    wgu_out[:, :GROUP] = wg_ref[...].astype(BF16)
    wgu_out[:, GROUP:] = wu_ref[...].astype(BF16)
    wd_out[...] = wd_ref[...].astype(BF16)
    act = _swiglu_groups(jnp.dot(h_ref[...], wgu_out[...], preferred_element_type=F32), 1)
    o_ref[...] += jnp.dot(act, wd_out[...], preferred_element_type=F32)


def _ffn_sample(x, g, w_gate, w_up, w_down, layer):
    vmem = 2 * TM * D_MODEL * 4 + TM * D_MODEL * 2 + 2 * 3 * D_MODEL * GROUP * 6 + 4 * TM * GROUP * 4
    once = pl.Buffered(1)
    col_f32 = pl.BlockSpec((None, D_MODEL, GROUP), lambda f: (layer, 0, f))
    return pl.pallas_call(
        _ffn_sample_kernel,
        grid=(N_GROUPS,),
        in_specs=[
            pl.BlockSpec((TM, D_MODEL), lambda f: (PROMPT_TILES, 0), pipeline_mode=once),
            pl.BlockSpec((1, D_MODEL), lambda f: (0, 0)),
            col_f32, col_f32,
            pl.BlockSpec((None, GROUP, D_MODEL), lambda f: (layer, f, 0)),
        ],
        out_specs=[
            pl.BlockSpec((TM, D_MODEL), lambda f: (0, 0), pipeline_mode=once),
            pl.BlockSpec((D_MODEL, 2 * GROUP), lambda f: (0, f)),
            pl.BlockSpec((GROUP, D_MODEL), lambda f: (f, 0)),
        ],
        out_shape=[jax.ShapeDtypeStruct((M_SAMPLE, D_MODEL), F32),
                   jax.ShapeDtypeStruct((D_MODEL, 2 * D_FF), BF16),
                   jax.ShapeDtypeStruct((D_FF, D_MODEL), BF16)],
        scratch_shapes=[pltpu.VMEM((TM, D_MODEL), BF16)],
        compiler_params=_params(vmem, 1),
        name="ffn_sample",
    )(x, g, w_gate, w_up, w_down)


def _row(v):
    return v.astype(F32).reshape(1, -1)


def kernel(x_prompt, x_sample, cache_attn_k, cache_attn_v, state_pool, cache_mem_k, cache_mem_v, mem_prompt,
           g_mix, w_in, q_norm, k_norm, attn_sinks, w_pool, pool_scale, w_out,
           g_cross, g_mem, w_q_mem, w_k_mem, w_v_mem, q_norm_mem, k_norm_mem, w_o_mem,
           g_ffn, w_gate, w_up, w_down):
    x_p, x_s = x_prompt.reshape(M_PROMPT, D_MODEL), x_sample.reshape(M_SAMPLE, D_MODEL)
    mem = mem_prompt.reshape(N_MEM, D_MODEL)

    w_in_b = _cast_w_in(w_in)
    w_out_b, w_q_b, w_k_b, w_v_b, w_o_b = (_cast_rows(w) for w in (w_out, w_q_mem, w_k_mem, w_v_mem, w_o_mem))
    w_pool_b = _cast_rows(w_pool.reshape(DEPTH, POOL_WIDTH, POOL_GROUP_WIDTH))

    ak_p, av_p, pl_p, mk_p, mv_p, ak_s, av_s, pl_s = ([] for _ in range(8))
    for l in range(DEPTH):
        z = _mixer_in(x_p, x_s, _row(g_mix[l]), w_in_b, l)
        kn = _k_norm(z, _row(jnp.tile(k_norm[l], 2)))
        v = z[:, Z_V0:Z_V0 + KV_WIDTH]
        kn_s = kn[M_PROMPT:].reshape(DEC_BATCH, DEC_SEQ, KV_WIDTH)
        v_s = v[M_PROMPT:].reshape(DEC_BATCH, DEC_SEQ, KV_WIDTH)
        k_full = jnp.concatenate([cache_attn_k[l].reshape(DEC_BATCH, WINDOW, KV_WIDTH), kn_s], axis=1)
        v_full = jnp.concatenate([cache_attn_v[l].reshape(DEC_BATCH, WINDOW, KV_WIDTH), v_s], axis=1)
        k_band = jnp.concatenate([kn[:M_PROMPT], k_full.reshape(DEC_BATCH * BAND, KV_WIDTH)], axis=0)
        v_band = jnp.concatenate([v[:M_PROMPT], v_full.reshape(DEC_BATCH * BAND, KV_WIDTH)], axis=0)
        bias, sink_e, sink_o = _swa_constants(attn_sinks[l])
        q_gain2 = _row(jnp.tile(q_norm[l], 2)) * (HEAD_DIM ** -0.5)
        attn = _swa(z, k_band, v_band, q_gain2, bias, sink_e, sink_o)

        state_halo = jnp.pad(state_pool[l].astype(F32), ((0, 0), (HALO - POOL_STATE, 0), (0, 0)))
        pooled = _pool(z, state_halo, w_pool_b, _row(pool_scale[l]), l)
        x = _mixer_out(attn, pooled, x_p, x_s, w_out_b, l)

        ak_p.append(kn[M_PROMPT - WINDOW:M_PROMPT].reshape(1, WINDOW, N_KV_HEADS, HEAD_DIM))
        av_p.append(v[M_PROMPT - WINDOW:M_PROMPT].reshape(1, WINDOW, N_KV_HEADS, HEAD_DIM))
        pl_p.append(z[M_PROMPT - POOL_STATE:M_PROMPT, Z_U0:Z_U0 + POOL_WIDTH].reshape(1, POOL_STATE, POOL_WIDTH))
        ak_s.append(k_full[:, -WINDOW:].reshape(DEC_BATCH, WINDOW, N_KV_HEADS, HEAD_DIM))
        av_s.append(v_full[:, -WINDOW:].reshape(DEC_BATCH, WINDOW, N_KV_HEADS, HEAD_DIM))
        u_s = z[M_PROMPT:, Z_U0:Z_U0 + POOL_WIDTH].reshape(DEC_BATCH, DEC_SEQ, POOL_WIDTH)
        pl_s.append(jnp.concatenate([state_pool[l].astype(F32), u_s], axis=1)[:, -POOL_STATE:])

        mk, mv = _mem_kv(mem, _row(g_mem[l]), w_k_b, w_v_b, _row(k_norm_mem[l]), l)
        mk_p.append(mk.reshape(1, N_MEM, MEM_HEADS, MEM_HEAD_DIM))
        mv_p.append(mv.reshape(1, N_MEM, MEM_HEADS, MEM_HEAD_DIM))
        x = _cross(x, _row(g_cross[l]), w_q_b, _row(q_norm_mem[l]),
                   mk.reshape(1, N_MEM, MEM_WIDTH), mv.reshape(1, N_MEM, MEM_WIDTH),
                   cache_mem_k[l].reshape(DEC_BATCH, N_MEM, MEM_WIDTH).astype(F32),
                   cache_mem_v[l].reshape(DEC_BATCH, N_MEM, MEM_WIDTH).astype(F32),
                   w_o_b, l)

        x_s, w_gate_up_b, w_down_b = _ffn_sample(x, _row(g_ffn[l]), w_gate, w_up, w_down, l)
        x_p = _ffn_prompt(x, _row(g_ffn[l]), w_gate_up_b, w_down_b)

    return (x_p.reshape(1, SEQ, D_MODEL), x_s.reshape(DEC_BATCH, DEC_SEQ, D_MODEL),
            jnp.stack(ak_p), jnp.stack(av_p), jnp.stack(pl_p), jnp.stack(mk_p), jnp.stack(mv_p),
            jnp.stack(ak_s), jnp.stack(av_s), jnp.stack(pl_s))
```

```python
import functools

import jax
import jax.numpy as jnp
from jax import lax
from jax.experimental import pallas as pl
from jax.experimental.pallas import tpu as pltpu

F32 = jnp.float32
BF16 = jnp.bfloat16

D_MODEL = 4096
SEQ = 8192
DEPTH = 2
DEC_BATCH = 8
DEC_SEQ = 64
PAST_LEN = 4096
CHUNK = 64
WINDOW = 128
ATTN_WIDTH = D_MODEL // 2
POOL_WIDTH = D_MODEL - ATTN_WIDTH
HEAD_DIM = 64
N_HEADS = ATTN_WIDTH // HEAD_DIM
N_KV_HEADS = N_HEADS // 8
KV_WIDTH = N_KV_HEADS * HEAD_DIM
POOL_WINDOWS = (2, 4, 8, 16)
POOL_GROUP_WIDTH = POOL_WIDTH // len(POOL_WINDOWS)
POOL_STATE = max(POOL_WINDOWS) - 1
IN_WIDTH = ATTN_WIDTH + 2 * KV_WIDTH + POOL_WIDTH
N_MEM = 256
MEM_HEADS = 4
MEM_HEAD_DIM = 128
MEM_WIDTH = MEM_HEADS * MEM_HEAD_DIM
D_FF = -(-8 * D_MODEL // (3 * 256)) * 256
EPS = 1e-6
NEG_INF = -1e30

M_PROMPT = SEQ
M_SAMPLE = DEC_BATCH * DEC_SEQ
M_ALL = M_PROMPT + M_SAMPLE
N_CHUNKS_PROMPT = M_PROMPT // CHUNK
N_CHUNKS = M_ALL // CHUNK
BAND = WINDOW + CHUNK
BAND_BLOCKS = BAND // CHUNK

Z_U0 = ATTN_WIDTH
Z_K0 = ATTN_WIDTH + POOL_WIDTH
Z_V0 = Z_K0 + KV_WIDTH

LANES = 128
SUBLANES = 8
BF16_TILE_ROWS = 16
HALO = BF16_TILE_ROWS
VMEM_LIMIT_CAP = 60000 * 1024
CAST_BLOCK_BYTES = 8 << 20

TM = 512
PROMPT_TILES = M_PROMPT // TM
ROW_TILES = M_ALL // TM
TN_IN = 768
TN_OUT = 1024
TF_FFN = 512
NF_FFN = -(-D_FF // TF_FFN)
TF_LAST = D_FF - (NF_FFN - 1) * TF_FFN
TF_SAMPLE = 256
TM_CROSS = 256
TM_KN = 512
TP_POOL = 256
NORM_ROWS = BF16_TILE_ROWS
NORM_UNROLL = 4
W_IN_COL_BLOCK = KV_WIDTH


def _params(vmem_bytes, n_axes):
    limit = min(int(vmem_bytes * 1.2) + (6 << 20), VMEM_LIMIT_CAP)
    return pltpu.CompilerParams(dimension_semantics=("arbitrary",) * n_axes, vmem_limit_bytes=limit)


def _rms_rows(x, g):
    return x * lax.rsqrt(jnp.mean(x * x, axis=-1, keepdims=True) + EPS) * g


def _norm_rows_to(x_ref, g_ref, h_ref, n_rows, copy_ref=None):
    def body(r, carry):
        rows = pl.ds(pl.multiple_of(r * NORM_ROWS, NORM_ROWS), NORM_ROWS)
        x = x_ref[rows, :]
        h_ref[rows, :] = _rms_rows(x, g_ref[...]).astype(BF16)
        if copy_ref is not None:
            copy_ref[rows, :] = x
        return carry

    lax.fori_loop(0, n_rows // NORM_ROWS, body, 0, unroll=NORM_UNROLL)


def _half_lane_rms(x, gain2):
    lo = lax.broadcasted_iota(jnp.int32, (1, LANES), 1) < HEAD_DIM
    x2 = x * x
    ss_lo = jnp.sum(jnp.where(lo, x2, 0.0), axis=-1, keepdims=True)
    ss_hi = jnp.sum(jnp.where(lo, 0.0, x2), axis=-1, keepdims=True)
    r = jnp.where(lo, lax.rsqrt(ss_lo / HEAD_DIM + EPS), lax.rsqrt(ss_hi / HEAD_DIM + EPS))
    return x * r * gain2


def _cast_kernel(w_ref, o_ref):
    o_ref[...] = w_ref[...].astype(BF16)


def _cast_rows(w):
    depth, rows, cols = w.shape
    tr = max(t for t in range(BF16_TILE_ROWS, rows + 1, BF16_TILE_ROWS)
             if rows % t == 0 and t * cols * 4 <= CAST_BLOCK_BYTES)
    spec = pl.BlockSpec((None, tr, cols), lambda l, i: (l, i, 0))
    return pl.pallas_call(
        _cast_kernel, grid=(depth, rows // tr), in_specs=[spec], out_specs=spec,
        out_shape=jax.ShapeDtypeStruct(w.shape, BF16),
        compiler_params=_params(2 * tr * cols * 6, 2), name="cast_rows",
    )(w)


def _w_in_source_block(j):
    q_blocks, u_blocks = ATTN_WIDTH // W_IN_COL_BLOCK, POOL_WIDTH // W_IN_COL_BLOCK
    kv_blocks = 2 * KV_WIDTH // W_IN_COL_BLOCK
    return jnp.where(j < q_blocks, j, jnp.where(j < q_blocks + u_blocks, j + kv_blocks, j - u_blocks))


def _cast_w_in(w):
    depth = w.shape[0]
    return pl.pallas_call(
        _cast_kernel, grid=(depth, IN_WIDTH // W_IN_COL_BLOCK),
        in_specs=[pl.BlockSpec((None, D_MODEL, W_IN_COL_BLOCK), lambda l, j: (l, 0, _w_in_source_block(j)))],
        out_specs=pl.BlockSpec((None, D_MODEL, W_IN_COL_BLOCK), lambda l, j: (l, 0, j)),
        out_shape=jax.ShapeDtypeStruct(w.shape, BF16),
        compiler_params=_params(2 * D_MODEL * W_IN_COL_BLOCK * 6, 2), name="cast_w_in",
    )(w)


def _mixer_in_kernel(xp_ref, xs_ref, g_ref, w_ref, z_ref, h_ref):
    i, j = pl.program_id(0), pl.program_id(1)

    @pl.when((j == 0) & (i < PROMPT_TILES))
    def _():
        _norm_rows_to(xp_ref, g_ref, h_ref, TM)

    @pl.when((j == 0) & (i >= PROMPT_TILES))
    def _():
        _norm_rows_to(xs_ref, g_ref, h_ref, TM)

    z_ref[...] = jnp.dot(h_ref[...], w_ref[...], preferred_element_type=F32)


def _x_specs(width, col):
    return [
        pl.BlockSpec((TM, width), lambda i, j: (jnp.minimum(i, PROMPT_TILES - 1), col(i, j))),
        pl.BlockSpec((TM, width), lambda i, j: (0, jnp.where(i >= PROMPT_TILES, col(i, j), 0))),
    ]


def _mixer_in(x_p, x_s, g, w, layer):
    vmem = 4 * TM * D_MODEL * 4 + TM * D_MODEL * 2 + 2 * D_MODEL * TN_IN * 2 + 2 * TM * TN_IN * 4
    return pl.pallas_call(
        _mixer_in_kernel,
        grid=(ROW_TILES, IN_WIDTH // TN_IN),
        in_specs=_x_specs(D_MODEL, lambda i, j: 0) + [
            pl.BlockSpec((1, D_MODEL), lambda i, j: (0, 0)),
            pl.BlockSpec((None, D_MODEL, TN_IN), lambda i, j: (layer, 0, j)),
        ],
        out_specs=pl.BlockSpec((TM, TN_IN), lambda i, j: (i, j)),
        out_shape=jax.ShapeDtypeStruct((M_ALL, IN_WIDTH), F32),
        scratch_shapes=[pltpu.VMEM((TM, D_MODEL), BF16)],
        compiler_params=_params(vmem, 2),
        name="mixer_in",
    )(x_p, x_s, g, w)


def _k_norm_kernel(k_ref, g_ref, o_ref):
    for t in range(KV_WIDTH // LANES):
        cols = slice(t * LANES, (t + 1) * LANES)
        o_ref[:, cols] = _half_lane_rms(k_ref[:, cols], g_ref[...])


def _k_norm(z, gain2):
    vmem = 4 * TM_KN * KV_WIDTH * 4
    return pl.pallas_call(
        _k_norm_kernel,
        grid=(M_ALL // TM_KN,),
        in_specs=[
            pl.BlockSpec((TM_KN, KV_WIDTH), lambda i: (i, Z_K0 // KV_WIDTH)),
            pl.BlockSpec((1, LANES), lambda i: (0, 0)),
        ],
        out_specs=pl.BlockSpec((TM_KN, KV_WIDTH), lambda i: (i, 0)),
        out_shape=jax.ShapeDtypeStruct((M_ALL, KV_WIDTH), F32),
        compiler_params=_params(vmem, 1),
        name="k_norm",
    )(z, gain2)


PAIRS_PER_KV = N_HEADS // N_KV_HEADS // 2
QUERIES_PER_KV = PAIRS_PER_KV * CHUNK
N_PAIRS = N_HEADS // 2
SWA_CHUNKS = 2
LOG2_E = 1.4426950408889634
Z_V_BLOCK = Z_V0 // KV_WIDTH


def _swa_chunk(q_ref, rows, k_refs, v_refs, qg_ref, bias_ref, table, sink_e_ref, sink_o_ref, o_ref):
    lo = lax.broadcasted_iota(jnp.int32, (1, LANES), 1) < HEAD_DIM

    q = jnp.concatenate([q_ref[rows, p * LANES:(p + 1) * LANES] for p in range(N_PAIRS)], axis=0)

    sel_row = lax.broadcasted_iota(jnp.int32, (SUBLANES, LANES), 0)
    sel_lo = lax.broadcasted_iota(jnp.int32, (SUBLANES, LANES), 1) < HEAD_DIM
    sel = jnp.where(((sel_row == 0) & sel_lo) | ((sel_row == 1) & ~sel_lo), 1.0, 0.0).astype(BF16)
    q2 = q * q
    q2_hi = q2.astype(BF16)
    q2_lo = (q2 - q2_hi.astype(F32)).astype(BF16)
    nt = (((1,), (1,)), ((), ()))
    ss = (lax.dot_general(sel, q2_hi, nt, preferred_element_type=F32)
          + lax.dot_general(sel, q2_lo, nt, preferred_element_type=F32))
    r = lax.rsqrt(ss / HEAD_DIM + EPS) * LOG2_E
    qg = (q * qg_ref[...]).astype(BF16)

    kb = jnp.concatenate([ref[...] for ref in k_refs], axis=0)
    vb = jnp.concatenate([ref[...] for ref in v_refs] + [jnp.zeros((KV_WIDTH - BAND, KV_WIDTH), F32)], axis=0)
    v_t = vb.T

    def head_softmax(s_raw, r_row, bias, sink):
        s = s_raw * r_row - bias
        m = jnp.maximum(jnp.max(s, axis=0, keepdims=True), sink)
        p = jnp.exp2(s - m)
        return p.astype(BF16), jnp.sum(p, axis=0, keepdims=True) + jnp.exp2(sink - m)

    for kv_tile in range(KV_WIDTH // LANES):
        kt = kb[:, kv_tile * LANES:(kv_tile + 1) * LANES]
        kt_r = pltpu.roll(kt, HEAD_DIM, axis=1)
        for par in range(2):
            hk = 2 * kv_tile + par
            k_e = jnp.where(lo, kt_r if par else kt, 0.0)
            k_o = jnp.where(lo, 0.0, kt if par else kt_r)
            kk = jnp.concatenate([k_e, k_o], axis=0).astype(BF16)
            cols = slice(hk * QUERIES_PER_KV, (hk + 1) * QUERIES_PER_KV)
            s_t = lax.dot_general(kk, qg[cols], nt, preferred_element_type=F32)
            bias = bias_ref[table, hk]
            p_e, d_e = head_softmax(s_t[:BAND], r[0:1, cols], bias[:BAND], sink_e_ref[hk])
            p_o, d_o = head_softmax(s_t[BAND:], r[1:2, cols], bias[BAND:], sink_o_ref[hk])
            v_h = v_t[hk * HEAD_DIM:(hk + 1) * HEAD_DIM, :BAND].astype(BF16)
            o_e = jnp.dot(v_h, p_e, preferred_element_type=F32) / d_e
            o_o = jnp.dot(v_h, p_o, preferred_element_type=F32) / d_o
            o = jnp.concatenate([o_e, o_o], axis=0).T
            for pr in range(PAIRS_PER_KV):
                pair = hk * PAIRS_PER_KV + pr
                o_ref[rows, pair * LANES:(pair + 1) * LANES] = o[pr * CHUNK:(pr + 1) * CHUNK].astype(BF16)


def _swa_kernel(q_ref, *refs, prompt):
    n_band = SWA_CHUNKS * BAND_BLOCKS
    k_refs, v_refs = refs[:n_band], refs[n_band:2 * n_band]
    qg_ref, bias_ref, sink_e_ref, sink_o_ref, o_ref = refs[2 * n_band:]
    for ci in range(SWA_CHUNKS):
        table = jnp.minimum(pl.program_id(0) * SWA_CHUNKS + ci, BAND_BLOCKS - 1) if prompt else 0
        band = slice(ci * BAND_BLOCKS, (ci + 1) * BAND_BLOCKS)
        _swa_chunk(q_ref, slice(ci * CHUNK, (ci + 1) * CHUNK), k_refs[band], v_refs[band],
                   qg_ref, bias_ref, table, sink_e_ref, sink_o_ref, o_ref)


def _swa_call(prompt, n_rows, q_block0, band_k, band_v, operands, q_gain2, bias, sink_e, sink_o):
    step_rows = SWA_CHUNKS * CHUNK
    const = lambda a: pl.BlockSpec(a.shape, lambda s: (0,) * a.ndim)
    vmem = (4 * step_rows * ATTN_WIDTH * 4 + 8 * SWA_CHUNKS * BAND * KV_WIDTH * 4 + 2 * bias.size * 4
            + SWA_CHUNKS * (12 << 20))
    return pl.pallas_call(
        functools.partial(_swa_kernel, prompt=prompt),
        grid=(n_rows // step_rows,),
        in_specs=[pl.BlockSpec((step_rows, ATTN_WIDTH), lambda s: (q_block0 + s, 0))] + band_k + band_v + [
            pl.BlockSpec((1, LANES), lambda s: (0, 0)), const(bias), const(sink_e), const(sink_o)],
        out_specs=pl.BlockSpec((step_rows, ATTN_WIDTH), lambda s: (s, 0)),
        out_shape=jax.ShapeDtypeStruct((n_rows, ATTN_WIDTH), BF16),
        compiler_params=_params(vmem, 1),
        name="swa_prompt" if prompt else "swa_sample",
    )(*operands, q_gain2, bias, sink_e, sink_o)


def _swa_prompt(z, kn, q_gain2, bias, sink_e, sink_o):
    def spec(ci, j, col):
        return pl.BlockSpec((CHUNK, KV_WIDTH),
                            lambda s: (jnp.maximum(s * SWA_CHUNKS + ci - (BAND_BLOCKS - 1) + j, 0), col))
    slots = [(ci, j) for ci in range(SWA_CHUNKS) for j in range(BAND_BLOCKS)]
    band_k = [spec(ci, j, 0) for ci, j in slots]
    band_v = [spec(ci, j, Z_V_BLOCK) for ci, j in slots]
    operands = [z] + [kn] * len(slots) + [z] * len(slots)
    return _swa_call(True, M_PROMPT, 0, band_k, band_v, operands, q_gain2, bias, sink_e, sink_o)


def _swa_sample(z, kn, cache_k, cache_v, q_gain2, bias, sink_e, sink_o):
    cache_blocks = WINDOW // CHUNK
    first_chunk = N_CHUNKS_PROMPT

    def specs(new_col):
        out = []
        for ci in range(SWA_CHUNKS):
            out += [pl.BlockSpec((None, CHUNK, KV_WIDTH), functools.partial(lambda s, ci, j: (s * SWA_CHUNKS + ci, j, 0),
                                                                            ci=ci, j=j))
                    for j in range(cache_blocks)]
            out.append(pl.BlockSpec((CHUNK, KV_WIDTH),
                                    functools.partial(lambda s, ci: (first_chunk + s * SWA_CHUNKS + ci, new_col), ci=ci)))
        return out

    operands = [z] + ([cache_k] * cache_blocks + [kn]) * SWA_CHUNKS + ([cache_v] * cache_blocks + [z]) * SWA_CHUNKS
    return _swa_call(False, M_SAMPLE, M_PROMPT // (SWA_CHUNKS * CHUNK), specs(0), specs(Z_V_BLOCK), operands,
                     q_gain2, bias[BAND_BLOCKS - 1:], sink_e, sink_o)


def _swa_constants(sinks):
    slopes = 2.0 ** (-8.0 * jnp.arange(1, N_HEADS + 1, dtype=F32) / N_HEADS)
    key = jnp.arange(BAND)
    dist = jnp.abs(jnp.arange(CHUNK)[None, :] + WINDOW - key[:, None]).astype(F32)
    head = (jnp.arange(N_KV_HEADS)[:, None, None] * (2 * PAIRS_PER_KV)
            + jnp.arange(2)[None, :, None] + 2 * jnp.arange(PAIRS_PER_KV)[None, None, :])
    bias = slopes[head][:, :, None, :, None] * dist[None, None, :, None, :]
    first_valid = jnp.maximum(WINDOW - CHUNK * jnp.arange(BAND_BLOCKS), 0)
    valid = key[None, :] >= first_valid[:, None]
    tables = jnp.where(valid[:, None, None, :, None, None], bias[None] * LOG2_E, -NEG_INF)
    tables = tables.reshape(BAND_BLOCKS, N_KV_HEADS, 2 * BAND, QUERIES_PER_KV)
    per_pair = sinks.astype(F32).reshape(N_KV_HEADS, PAIRS_PER_KV, 2) * LOG2_E
    rows = jnp.repeat(per_pair, CHUNK, axis=1)
    return tables, rows[:, None, :, 0], rows[:, None, :, 1]


POOL_TILES_PROMPT = M_PROMPT // TP_POOL
BATCH_PER_POOL_TILE = TP_POOL // DEC_SEQ


def _trailing_sum(ext, w):
    s, k = ext, 1
    while k < w:
        s = s + pltpu.roll(s, k, axis=0)
        k *= 2
    return s


def _pool_kernel(u_ref, halo_ref, state_ref, w_ref, scale_ref, o_ref, d_ref):
    i = pl.program_id(0)

    def window_diffs(ext, cur, cnt, w):
        return _trailing_sum(ext, w)[HALO:] / cnt - cur

    @pl.when(i < POOL_TILES_PROMPT)
    def _():
        pos = i * TP_POOL + lax.broadcasted_iota(jnp.int32, (TP_POOL, 1), 0)
        for g, w in enumerate(POOL_WINDOWS):
            cols = slice(g * POOL_GROUP_WIDTH, (g + 1) * POOL_GROUP_WIDTH)
            cur = u_ref[:, cols]
            halo = jnp.where(i > 0, halo_ref[:, cols], 0.0)
            ext = jnp.concatenate([halo, cur], axis=0)
            cnt = jnp.minimum(pos + 1, w).astype(F32)
            d_ref[:, cols] = window_diffs(ext, cur, cnt, w)

    @pl.when(i >= POOL_TILES_PROMPT)
    def _():
        for g, w in enumerate(POOL_WINDOWS):
            cols = slice(g * POOL_GROUP_WIDTH, (g + 1) * POOL_GROUP_WIDTH)
            for b in range(BATCH_PER_POOL_TILE):
                rows = slice(b * DEC_SEQ, (b + 1) * DEC_SEQ)
                cur = u_ref[rows, cols]
                ext = jnp.concatenate([state_ref[b, :, cols], cur], axis=0)
                d_ref[rows, cols] = window_diffs(ext, cur, float(w), w)

    for g in range(len(POOL_WINDOWS)):
        cols = slice(g * POOL_GROUP_WIDTH, (g + 1) * POOL_GROUP_WIDTH)
        y = jnp.dot(d_ref[:, cols].astype(BF16), w_ref[cols, :], preferred_element_type=F32)
        o_ref[:, cols] = (y * scale_ref[:, cols]).astype(BF16)


def _pool(z, state_halo, w_pool, scale, layer):
    halo_blocks_per_tile = TP_POOL // HALO
    vmem = (2 * TP_POOL * POOL_WIDTH * 4 + 2 * HALO * POOL_WIDTH * 4 + 2 * BATCH_PER_POOL_TILE * HALO * POOL_WIDTH * 4
            + 2 * POOL_WIDTH * POOL_GROUP_WIDTH * 2 + 2 * TP_POOL * POOL_WIDTH * 2 + TP_POOL * POOL_WIDTH * 4
            + (8 << 20))
    return pl.pallas_call(
        _pool_kernel,
        grid=(M_ALL // TP_POOL,),
        in_specs=[
            pl.BlockSpec((TP_POOL, POOL_WIDTH), lambda i: (i, Z_U0 // POOL_WIDTH)),
            pl.BlockSpec((HALO, POOL_WIDTH),
                         lambda i: (jnp.maximum(i * halo_blocks_per_tile - 1, 0), Z_U0 // POOL_WIDTH)),
            pl.BlockSpec((BATCH_PER_POOL_TILE, HALO, POOL_WIDTH),
                         lambda i: (jnp.maximum(i - POOL_TILES_PROMPT, 0), 0, 0)),
            pl.BlockSpec((None, POOL_WIDTH, POOL_GROUP_WIDTH), lambda i: (layer, 0, 0)),
            pl.BlockSpec((1, POOL_WIDTH), lambda i: (0, 0)),
        ],
        out_specs=pl.BlockSpec((TP_POOL, POOL_WIDTH), lambda i: (i, 0)),
        out_shape=jax.ShapeDtypeStruct((M_ALL, POOL_WIDTH), BF16),
        scratch_shapes=[pltpu.VMEM((TP_POOL, POOL_WIDTH), F32)],
        compiler_params=_params(vmem, 1),
        name="pool",
    )(z, z, state_halo, w_pool, scale)


def _mixer_out_kernel(ap_ref, as_ref, p_ref, xp_ref, xs_ref, wa_ref, wp_ref, o_ref):
    prompt = pl.program_id(0) < PROMPT_TILES
    a = jnp.where(prompt, ap_ref[...], as_ref[...])
    y = jnp.dot(a, wa_ref[...], preferred_element_type=F32)
    y = y + jnp.dot(p_ref[...], wp_ref[...], preferred_element_type=F32)
    o_ref[...] = jnp.where(prompt, xp_ref[...], xs_ref[...]) + y


def _mixer_out(attn_p, attn_s, pooled, x_p, x_s, w_out, layer):
    vmem = 2 * (3 * TM * ATTN_WIDTH * 2 + D_MODEL * TN_OUT * 2 + 3 * TM * TN_OUT * 4) + TM * ATTN_WIDTH * 2
    return pl.pallas_call(
        _mixer_out_kernel,
        grid=(ROW_TILES, D_MODEL // TN_OUT),
        in_specs=_x_specs(ATTN_WIDTH, lambda i, j: 0) + [
            pl.BlockSpec((TM, POOL_WIDTH), lambda i, j: (i, 0)),
        ] + _x_specs(TN_OUT, lambda i, j: j) + [
            pl.BlockSpec((None, ATTN_WIDTH, TN_OUT), lambda i, j: (layer, 0, j)),
            pl.BlockSpec((None, POOL_WIDTH, TN_OUT), lambda i, j: (layer, 1, j)),
        ],
        out_specs=pl.BlockSpec((TM, TN_OUT), lambda i, j: (i, j)),
        out_shape=jax.ShapeDtypeStruct((M_ALL, D_MODEL), F32),
        compiler_params=_params(vmem, 2),
        name="mixer_out",
    )(attn_p, attn_s, pooled, x_p, x_s, w_out, w_out)


def _mem_kv_kernel(m_ref, g_ref, wk_ref, wv_ref, kg_ref, k_ref, v_ref, h_ref):
    _norm_rows_to(m_ref, g_ref, h_ref, N_MEM)
    h = h_ref[...]
    k = jnp.dot(h, wk_ref[...], preferred_element_type=F32)
    for hd in range(MEM_HEADS):
        cols = slice(hd * MEM_HEAD_DIM, (hd + 1) * MEM_HEAD_DIM)
        k_ref[:, cols] = _rms_rows(k[:, cols], kg_ref[...])
    v_ref[...] = jnp.dot(h, wv_ref[...], preferred_element_type=F32)


def _mem_kv(mem, g, w_k, w_v, k_gain, layer):
    vmem = 2 * (N_MEM * D_MODEL * 4 + 2 * D_MODEL * MEM_WIDTH * 2 + 2 * N_MEM * MEM_WIDTH * 4) + N_MEM * D_MODEL * 2
    full = lambda shape: pl.BlockSpec(shape, lambda i: (0,) * len(shape))
    weight = pl.BlockSpec((None, D_MODEL, MEM_WIDTH), lambda i: (layer, 0, 0))
    return pl.pallas_call(
        _mem_kv_kernel,
        grid=(1,),
        in_specs=[full((N_MEM, D_MODEL)), full((1, D_MODEL)), weight, weight, full((1, MEM_HEAD_DIM))],
        out_specs=[full((N_MEM, MEM_WIDTH)), full((N_MEM, MEM_WIDTH))],
        out_shape=[jax.ShapeDtypeStruct((N_MEM, MEM_WIDTH), F32)] * 2,
        scratch_shapes=[pltpu.VMEM((N_MEM, D_MODEL), BF16)],
        compiler_params=_params(vmem, 1),
        name="mem_kv",
    )(mem, g, w_k, w_v, k_gain)


CROSS_TILES_PROMPT = M_PROMPT // TM_CROSS
BATCH_PER_CROSS_TILE = TM_CROSS // DEC_SEQ


def _cross_heads(q, mk_ref, mv_ref, b, qg, oc_ref, rows):
    for hd in range(MEM_HEADS):
        cols = slice(hd * MEM_HEAD_DIM, (hd + 1) * MEM_HEAD_DIM)
        qn = _rms_rows(q[:, cols], qg).astype(BF16)
        s = lax.dot_general(qn, mk_ref[b, :, cols].astype(BF16), (((1,), (1,)), ((), ())),
                            preferred_element_type=F32) * (MEM_HEAD_DIM ** -0.5)
        p = jnp.exp(s - jnp.max(s, axis=-1, keepdims=True))
        den = jnp.sum(p, axis=-1, keepdims=True)
        o = jnp.dot(p.astype(BF16), mv_ref[b, :, cols].astype(BF16), preferred_element_type=F32) / den
        oc_ref[rows, cols] = o.astype(BF16)


def _cross_kernel(x_ref, g_ref, wq_ref, qg_ref, mkp_ref, mvp_ref, mks_ref, mvs_ref, wo_ref,
                  o_ref, h_ref, q_ref, oc_ref):
    i = pl.program_id(0)
    _norm_rows_to(x_ref, g_ref, h_ref, TM_CROSS)
    q_ref[...] = jnp.dot(h_ref[...], wq_ref[...], preferred_element_type=F32)

    @pl.when(i < CROSS_TILES_PROMPT)
    def _():
        _cross_heads(q_ref[...], mkp_ref, mvp_ref, 0, qg_ref[...], oc_ref, slice(0, TM_CROSS))

    @pl.when(i >= CROSS_TILES_PROMPT)
    def _():
        for b in range(BATCH_PER_CROSS_TILE):
            rows = slice(b * DEC_SEQ, (b + 1) * DEC_SEQ)
            _cross_heads(q_ref[rows, :], mks_ref, mvs_ref, b, qg_ref[...], oc_ref, rows)

    o_ref[...] = x_ref[...] + jnp.dot(oc_ref[...], wo_ref[...], preferred_element_type=F32)


def _cross(x, g, w_q, q_gain, mk_p, mv_p, mk_s, mv_s, w_o, layer):
    const = lambda shape: pl.BlockSpec(shape, lambda i: (0,) * len(shape))
    once = lambda shape: pl.BlockSpec((None,) + shape, lambda i: (layer, 0, 0), pipeline_mode=pl.Buffered(1))
    sample_mem = pl.BlockSpec((BATCH_PER_CROSS_TILE, N_MEM, MEM_WIDTH),
                              lambda i: (jnp.maximum(i - CROSS_TILES_PROMPT, 0), 0, 0))
    vmem = (4 * TM_CROSS * D_MODEL * 4 + 2 * D_MODEL * MEM_WIDTH * 2
            + 4 * (1 + BATCH_PER_CROSS_TILE) * N_MEM * MEM_WIDTH * 4
            + TM_CROSS * D_MODEL * 2 + TM_CROSS * MEM_WIDTH * 6 + (6 << 20))
    return pl.pallas_call(
        _cross_kernel,
        grid=(M_ALL // TM_CROSS,),
        in_specs=[
            pl.BlockSpec((TM_CROSS, D_MODEL), lambda i: (i, 0)),
            const((1, D_MODEL)),
            once((D_MODEL, MEM_WIDTH)),
            const((1, MEM_HEAD_DIM)),
            const(mk_p.shape), const(mv_p.shape), sample_mem, sample_mem,
            once((MEM_WIDTH, D_MODEL)),
        ],
        out_specs=pl.BlockSpec((TM_CROSS, D_MODEL), lambda i: (i, 0)),
        out_shape=jax.ShapeDtypeStruct((M_ALL, D_MODEL), F32),
        scratch_shapes=[pltpu.VMEM((TM_CROSS, D_MODEL), BF16), pltpu.VMEM((TM_CROSS, MEM_WIDTH), F32),
                        pltpu.VMEM((TM_CROSS, MEM_WIDTH), BF16)],
        compiler_params=_params(vmem, 1),
        name="cross",
    )(x, g, w_q, q_gain, mk_p, mv_p, mk_s, mv_s, w_o)


GROUP = TF_SAMPLE
GROUPS_PER_TILE = TF_FFN // GROUP
GROUPS_LAST = TF_LAST // GROUP
N_GROUPS = D_FF // GROUP


def _swiglu_groups(gate_up, n_groups):
    acts = []
    for k in range(n_groups):
        gate = gate_up[:, 2 * k * GROUP:(2 * k + 1) * GROUP]
        up = gate_up[:, (2 * k + 1) * GROUP:(2 * k + 2) * GROUP]
        acts.append(gate * jax.nn.sigmoid(gate) * up)
    return (acts[0] if n_groups == 1 else jnp.concatenate(acts, axis=1)).astype(BF16)


def _ffn_prompt_kernel(x_ref, g_ref, wgu_ref, wd_ref, o_ref, h_ref):
    f = pl.program_id(1)

    @pl.when(f == 0)
    def _():
        _norm_rows_to(x_ref, g_ref, h_ref, TM, copy_ref=o_ref)

    def accumulate(n_groups):
        gate_up = jnp.dot(h_ref[...], wgu_ref[:, :2 * n_groups * GROUP], preferred_element_type=F32)
        act = _swiglu_groups(gate_up, n_groups)
        o_ref[...] += jnp.dot(act, wd_ref[:n_groups * GROUP, :], preferred_element_type=F32)

    pl.when(f < NF_FFN - 1)(functools.partial(accumulate, GROUPS_PER_TILE))
    pl.when(f == NF_FFN - 1)(functools.partial(accumulate, GROUPS_LAST))


def _ffn_prompt(x, g, w_gate_up, w_down):
    vmem = 3 * TM * D_MODEL * 4 + TM * D_MODEL * 2 + 2 * 3 * D_MODEL * TF_FFN * 2 + 4 * TM * TF_FFN * 4
    return pl.pallas_call(
        _ffn_prompt_kernel,
        grid=(PROMPT_TILES, NF_FFN),
        in_specs=[
            pl.BlockSpec((TM, D_MODEL), lambda i, f: (i, 0)),
            pl.BlockSpec((1, D_MODEL), lambda i, f: (0, 0)),
            pl.BlockSpec((D_MODEL, 2 * TF_FFN), lambda i, f: (0, f)),
            pl.BlockSpec((TF_FFN, D_MODEL), lambda i, f: (f, 0)),
        ],
        out_specs=pl.BlockSpec((TM, D_MODEL), lambda i, f: (i, 0), pipeline_mode=pl.Buffered(1)),
        out_shape=jax.ShapeDtypeStruct((M_PROMPT, D_MODEL), F32),
        scratch_shapes=[pltpu.VMEM((TM, D_MODEL), BF16)],
        compiler_params=_params(vmem, 2),
        name="ffn_prompt",
    )(x, g, w_gate_up, w_down)


def _ffn_sample_kernel(x_ref, g_ref, wg_ref, wu_ref, wd_ref, o_ref, wgu_out, wd_out, h_ref):
    @pl.when(pl.program_id(0) == 0)
    def _():
        _norm_rows_to(x_ref, g_ref, h_ref, TM, copy_ref=o_ref)

    wgu_out[:, :GROUP] = wg_ref[...].astype(BF16)
    wgu_out[:, GROUP:] = wu_ref[...].astype(BF16)
    wd_out[...] = wd_ref[...].astype(BF16)
    act = _swiglu_groups(jnp.dot(h_ref[...], wgu_out[...], preferred_element_type=F32), 1)
    o_ref[...] += jnp.dot(act, wd_out[...], preferred_element_type=F32)


def _ffn_sample(x, g, w_gate, w_up, w_down, layer):
    vmem = 2 * TM * D_MODEL * 4 + TM * D_MODEL * 2 + 2 * 3 * D_MODEL * GROUP * 6 + 4 * TM * GROUP * 4
    once = pl.Buffered(1)
    col_f32 = pl.BlockSpec((None, D_MODEL, GROUP), lambda f: (layer, 0, f))
    return pl.pallas_call(
        _ffn_sample_kernel,
        grid=(N_GROUPS,),
        in_specs=[
            pl.BlockSpec((TM, D_MODEL), lambda f: (PROMPT_TILES, 0), pipeline_mode=once),
            pl.BlockSpec((1, D_MODEL), lambda f: (0, 0)),
            col_f32, col_f32,
            pl.BlockSpec((None, GROUP, D_MODEL), lambda f: (layer, f, 0)),
        ],
        out_specs=[
            pl.BlockSpec((TM, D_MODEL), lambda f: (0, 0), pipeline_mode=once),
            pl.BlockSpec((D_MODEL, 2 * GROUP), lambda f: (0, f)),
            pl.BlockSpec((GROUP, D_MODEL), lambda f: (f, 0)),
        ],
        out_shape=[jax.ShapeDtypeStruct((M_SAMPLE, D_MODEL), F32),
                   jax.ShapeDtypeStruct((D_MODEL, 2 * D_FF), BF16),
                   jax.ShapeDtypeStruct((D_FF, D_MODEL), BF16)],
        scratch_shapes=[pltpu.VMEM((TM, D_MODEL), BF16)],
        compiler_params=_params(vmem, 1),
        name="ffn_sample",
    )(x, g, w_gate, w_up, w_down)


def _row(v):
    return v.astype(F32).reshape(1, -1)


def kernel(x_prompt, x_sample, cache_attn_k, cache_attn_v, state_pool, cache_mem_k, cache_mem_v, mem_prompt,
           g_mix, w_in, q_norm, k_norm, attn_sinks, w_pool, pool_scale, w_out,
           g_cross, g_mem, w_q_mem, w_k_mem, w_v_mem, q_norm_mem, k_norm_mem, w_o_mem,
           g_ffn, w_gate, w_up, w_down):
    x_p, x_s = x_prompt.reshape(M_PROMPT, D_MODEL), x_sample.reshape(M_SAMPLE, D_MODEL)
    mem = mem_prompt.reshape(N_MEM, D_MODEL)

    w_in_b = _cast_w_in(w_in)
    w_out_b, w_q_b, w_k_b, w_v_b, w_o_b = (_cast_rows(w) for w in (w_out, w_q_mem, w_k_mem, w_v_mem, w_o_mem))
    w_pool_b = _cast_rows(w_pool.reshape(DEPTH, POOL_WIDTH, POOL_GROUP_WIDTH))

    ak_p, av_p, pl_p, mk_p, mv_p, ak_s, av_s, pl_s = ([] for _ in range(8))
    for l in range(DEPTH):
        z = _mixer_in(x_p, x_s, _row(g_mix[l]), w_in_b, l)
        kn = _k_norm(z, _row(jnp.tile(k_norm[l], 2)))
        cache_k = cache_attn_k[l].astype(F32).reshape(DEC_BATCH, WINDOW, KV_WIDTH)
        cache_v = cache_attn_v[l].astype(F32).reshape(DEC_BATCH, WINDOW, KV_WIDTH)
        bias, sink_e, sink_o = _swa_constants(attn_sinks[l])
        q_gain2 = _row(jnp.tile(q_norm[l], 2)) * (HEAD_DIM ** -0.5)
        attn_p = _swa_prompt(z, kn, q_gain2, bias, sink_e, sink_o)
        attn_s = _swa_sample(z, kn, cache_k, cache_v, q_gain2, bias, sink_e, sink_o)

        state_halo = jnp.pad(state_pool[l].astype(F32), ((0, 0), (HALO - POOL_STATE, 0), (0, 0)))
        pooled = _pool(z, state_halo, w_pool_b, _row(pool_scale[l]), l)
        x = _mixer_out(attn_p, attn_s, pooled, x_p, x_s, w_out_b, l)

        v = z[:, Z_V0:Z_V0 + KV_WIDTH]
        kn_s = kn[M_PROMPT:].reshape(DEC_BATCH, DEC_SEQ, KV_WIDTH)
        v_s = v[M_PROMPT:].reshape(DEC_BATCH, DEC_SEQ, KV_WIDTH)
        ak_p.append(kn[M_PROMPT - WINDOW:M_PROMPT].reshape(1, WINDOW, N_KV_HEADS, HEAD_DIM))
        av_p.append(v[M_PROMPT - WINDOW:M_PROMPT].reshape(1, WINDOW, N_KV_HEADS, HEAD_DIM))
        pl_p.append(z[M_PROMPT - POOL_STATE:M_PROMPT, Z_U0:Z_U0 + POOL_WIDTH].reshape(1, POOL_STATE, POOL_WIDTH))
        ak_s.append(jnp.concatenate([cache_k, kn_s], axis=1)[:, -WINDOW:]
                    .reshape(DEC_BATCH, WINDOW, N_KV_HEADS, HEAD_DIM))
        av_s.append(jnp.concatenate([cache_v, v_s], axis=1)[:, -WINDOW:]
                    .reshape(DEC_BATCH, WINDOW, N_KV_HEADS, HEAD_DIM))
        u_s = z[M_PROMPT:, Z_U0:Z_U0 + POOL_WIDTH].reshape(DEC_BATCH, DEC_SEQ, POOL_WIDTH)
        pl_s.append(jnp.concatenate([state_pool[l].astype(F32), u_s], axis=1)[:, -POOL_STATE:])

        mk, mv = _mem_kv(mem, _row(g_mem[l]), w_k_b, w_v_b, _row(k_norm_mem[l]), l)
        mk_p.append(mk.reshape(1, N_MEM, MEM_HEADS, MEM_HEAD_DIM))
        mv_p.append(mv.reshape(1, N_MEM, MEM_HEADS, MEM_HEAD_DIM))
        x = _cross(x, _row(g_cross[l]), w_q_b, _row(q_norm_mem[l]),
                   mk.reshape(1, N_MEM, MEM_WIDTH), mv.reshape(1, N_MEM, MEM_WIDTH),
                   cache_mem_k[l].reshape(DEC_BATCH, N_MEM, MEM_WIDTH).astype(F32),
                   cache_mem_v[l].reshape(DEC_BATCH, N_MEM, MEM_WIDTH).astype(F32),
                   w_o_b, l)

        x_s, w_gate_up_b, w_down_b = _ffn_sample(x, _row(g_ffn[l]), w_gate, w_up, w_down, l)
        x_p = _ffn_prompt(x, _row(g_ffn[l]), w_gate_up_b, w_down_b)

    return (x_p.reshape(1, SEQ, D_MODEL), x_s.reshape(DEC_BATCH, DEC_SEQ, D_MODEL),
            jnp.stack(ak_p), jnp.stack(av_p), jnp.stack(pl_p), jnp.stack(mk_p), jnp.stack(mv_p),
            jnp.stack(ak_s), jnp.stack(av_s), jnp.stack(pl_s))
```

```python
import functools

import jax
import jax.numpy as jnp
from jax import lax
from jax.experimental import pallas as pl
from jax.experimental.pallas import tpu as pltpu

F32 = jnp.float32
BF16 = jnp.bfloat16

D_MODEL = 4096
SEQ = 8192
DEPTH = 2
DEC_BATCH = 8
DEC_SEQ = 64
PAST_LEN = 4096
CHUNK = 64
WINDOW = 128
ATTN_WIDTH = D_MODEL // 2
POOL_WIDTH = D_MODEL - ATTN_WIDTH
HEAD_DIM = 64
N_HEADS = ATTN_WIDTH // HEAD_DIM
N_KV_HEADS = N_HEADS // 8
KV_WIDTH = N_KV_HEADS * HEAD_DIM
POOL_WINDOWS = (2, 4, 8, 16)
POOL_GROUP_WIDTH = POOL_WIDTH // len(POOL_WINDOWS)
POOL_STATE = max(POOL_WINDOWS) - 1
IN_WIDTH = ATTN_WIDTH + 2 * KV_WIDTH + POOL_WIDTH
N_MEM = 256
MEM_HEADS = 4
MEM_HEAD_DIM = 128
MEM_WIDTH = MEM_HEADS * MEM_HEAD_DIM
D_FF = -(-8 * D_MODEL // (3 * 256)) * 256
EPS = 1e-6
NEG_INF = -1e30

M_PROMPT = SEQ
M_SAMPLE = DEC_BATCH * DEC_SEQ
M_ALL = M_PROMPT + M_SAMPLE
N_CHUNKS_PROMPT = M_PROMPT // CHUNK
N_CHUNKS = M_ALL // CHUNK
BAND = WINDOW + CHUNK
BAND_BLOCKS = BAND // CHUNK

Z_U0 = ATTN_WIDTH
Z_K0 = ATTN_WIDTH + POOL_WIDTH
Z_V0 = Z_K0 + KV_WIDTH

LANES = 128
SUBLANES = 8
BF16_TILE_ROWS = 16
HALO = BF16_TILE_ROWS
VMEM_LIMIT_CAP = 60000 * 1024
CAST_BLOCK_BYTES = 8 << 20

TM = 512
PROMPT_TILES = M_PROMPT // TM
ROW_TILES = M_ALL // TM
TN_IN = 768
TN_OUT = 1024
TF_FFN = 512
NF_FFN = -(-D_FF // TF_FFN)
TF_LAST = D_FF - (NF_FFN - 1) * TF_FFN
TF_SAMPLE = 256
TM_CROSS = 256
TM_KN = 512
TP_POOL = 512
NORM_ROWS = BF16_TILE_ROWS
NORM_UNROLL = 4
W_IN_COL_BLOCK = KV_WIDTH


def _params(vmem_bytes, n_axes):
    limit = min(int(vmem_bytes * 1.2) + (6 << 20), VMEM_LIMIT_CAP)
    return pltpu.CompilerParams(dimension_semantics=("arbitrary",) * n_axes, vmem_limit_bytes=limit)


def _rms_rows(x, g):
    return x * lax.rsqrt(jnp.mean(x * x, axis=-1, keepdims=True) + EPS) * g


def _norm_rows_to(x_ref, g_ref, h_ref, n_rows, copy_ref=None):
    def body(r, carry):
        rows = pl.ds(pl.multiple_of(r * NORM_ROWS, NORM_ROWS), NORM_ROWS)
        x = x_ref[rows, :]
        h_ref[rows, :] = _rms_rows(x, g_ref[...]).astype(BF16)
        if copy_ref is not None:
            copy_ref[rows, :] = x
        return carry

    lax.fori_loop(0, n_rows // NORM_ROWS, body, 0, unroll=NORM_UNROLL)


def _norm_rows_inline(load_x, g_ref, h_ref, n_rows):
    chunks = [slice(c * LANES, (c + 1) * LANES) for c in range(D_MODEL // LANES)]
    for r0 in range(0, n_rows, NORM_ROWS):
        rows = slice(r0, r0 + NORM_ROWS)
        acc = None
        for cols in chunks:
            x = load_x(rows, cols)
            acc = x * x if acc is None else acc + x * x
        r = lax.rsqrt(jnp.sum(acc, axis=-1, keepdims=True) / D_MODEL + EPS)
        for cols in chunks:
            h_ref[rows, cols] = (load_x(rows, cols) * r * g_ref[:, cols]).astype(BF16)


def _half_lane_rms(x, gain2):
    lo = lax.broadcasted_iota(jnp.int32, (1, LANES), 1) < HEAD_DIM
    x2 = x * x
    ss_lo = jnp.sum(jnp.where(lo, x2, 0.0), axis=-1, keepdims=True)
    ss_hi = jnp.sum(jnp.where(lo, 0.0, x2), axis=-1, keepdims=True)
    r = jnp.where(lo, lax.rsqrt(ss_lo / HEAD_DIM + EPS), lax.rsqrt(ss_hi / HEAD_DIM + EPS))
    return x * r * gain2


def _cast_kernel(w_ref, o_ref):
    o_ref[...] = w_ref[...].astype(BF16)


def _cast_rows(w):
    depth, rows, cols = w.shape
    tr = max(t for t in range(BF16_TILE_ROWS, rows + 1, BF16_TILE_ROWS)
             if rows % t == 0 and t * cols * 4 <= CAST_BLOCK_BYTES)
    spec = pl.BlockSpec((None, tr, cols), lambda l, i: (l, i, 0))
    return pl.pallas_call(
        _cast_kernel, grid=(depth, rows // tr), in_specs=[spec], out_specs=spec,
        out_shape=jax.ShapeDtypeStruct(w.shape, BF16),
        compiler_params=_params(2 * tr * cols * 6, 2), name="cast_rows",
    )(w)


def _w_in_source_block(j):
    q_blocks, u_blocks = ATTN_WIDTH // W_IN_COL_BLOCK, POOL_WIDTH // W_IN_COL_BLOCK
    kv_blocks = 2 * KV_WIDTH // W_IN_COL_BLOCK
    return jnp.where(j < q_blocks, j, jnp.where(j < q_blocks + u_blocks, j + kv_blocks, j - u_blocks))


def _cast_w_in(w):
    depth = w.shape[0]
    return pl.pallas_call(
        _cast_kernel, grid=(depth, IN_WIDTH // W_IN_COL_BLOCK),
        in_specs=[pl.BlockSpec((None, D_MODEL, W_IN_COL_BLOCK), lambda l, j: (l, 0, _w_in_source_block(j)))],
        out_specs=pl.BlockSpec((None, D_MODEL, W_IN_COL_BLOCK), lambda l, j: (l, 0, j)),
        out_shape=jax.ShapeDtypeStruct(w.shape, BF16),
        compiler_params=_params(2 * D_MODEL * W_IN_COL_BLOCK * 6, 2), name="cast_w_in",
    )(w)


def _mixer_in_kernel(xp_ref, xs_ref, g_ref, w_ref, z_ref, h_ref):
    i, j = pl.program_id(0), pl.program_id(1)

    @pl.when((j == 0) & (i < PROMPT_TILES))
    def _():
        _norm_rows_to(xp_ref, g_ref, h_ref, TM)

    @pl.when((j == 0) & (i >= PROMPT_TILES))
    def _():
        _norm_rows_to(xs_ref, g_ref, h_ref, TM)

    z_ref[...] = jnp.dot(h_ref[...], w_ref[...], preferred_element_type=F32)


def _x_specs(width, col):
    return [
        pl.BlockSpec((TM, width), lambda i, j: (jnp.minimum(i, PROMPT_TILES - 1), col(i, j))),
        pl.BlockSpec((TM, width), lambda i, j: (0, jnp.where(i >= PROMPT_TILES, col(i, j), 0))),
    ]


def _mixer_in(x_p, x_s, g, w, layer):
    vmem = 4 * TM * D_MODEL * 4 + TM * D_MODEL * 2 + 2 * D_MODEL * TN_IN * 2 + 2 * TM * TN_IN * 4
    return pl.pallas_call(
        _mixer_in_kernel,
        grid=(ROW_TILES, IN_WIDTH // TN_IN),
        in_specs=_x_specs(D_MODEL, lambda i, j: 0) + [
            pl.BlockSpec((1, D_MODEL), lambda i, j: (0, 0)),
            pl.BlockSpec((None, D_MODEL, TN_IN), lambda i, j: (layer, 0, j)),
        ],
        out_specs=pl.BlockSpec((TM, TN_IN), lambda i, j: (i, j)),
        out_shape=jax.ShapeDtypeStruct((M_ALL, IN_WIDTH), F32),
        scratch_shapes=[pltpu.VMEM((TM, D_MODEL), BF16)],
        compiler_params=_params(vmem, 2),
        name="mixer_in",
    )(x_p, x_s, g, w)


def _k_norm_kernel(k_ref, g_ref, o_ref):
    for t in range(KV_WIDTH // LANES):
        cols = slice(t * LANES, (t + 1) * LANES)
        o_ref[:, cols] = _half_lane_rms(k_ref[:, cols], g_ref[...])


def _k_norm(z, gain2):
    vmem = 4 * TM_KN * KV_WIDTH * 4
    return pl.pallas_call(
        _k_norm_kernel,
        grid=(M_ALL // TM_KN,),
        in_specs=[
            pl.BlockSpec((TM_KN, KV_WIDTH), lambda i: (i, Z_K0 // KV_WIDTH)),
            pl.BlockSpec((1, LANES), lambda i: (0, 0)),
        ],
        out_specs=pl.BlockSpec((TM_KN, KV_WIDTH), lambda i: (i, 0)),
        out_shape=jax.ShapeDtypeStruct((M_ALL, KV_WIDTH), F32),
        compiler_params=_params(vmem, 1),
        name="k_norm",
    )(z, gain2)


PAIRS_PER_KV = N_HEADS // N_KV_HEADS // 2
QUERIES_PER_KV = PAIRS_PER_KV * CHUNK
N_PAIRS = N_HEADS // 2
SWA_CHUNKS = 4
LOG2_E = 1.4426950408889634
Z_V_BLOCK = Z_V0 // KV_WIDTH


def _swa_chunk(q_ref, rows, k_refs, v_refs, qg_ref, bias_ref, table, sink_e_ref, sink_o_ref, o_ref):
    lo = lax.broadcasted_iota(jnp.int32, (1, LANES), 1) < HEAD_DIM

    q = jnp.concatenate([q_ref[rows, p * LANES:(p + 1) * LANES] for p in range(N_PAIRS)], axis=0)

    sel_row = lax.broadcasted_iota(jnp.int32, (SUBLANES, LANES), 0)
    sel_lo = lax.broadcasted_iota(jnp.int32, (SUBLANES, LANES), 1) < HEAD_DIM
    sel = jnp.where(((sel_row == 0) & sel_lo) | ((sel_row == 1) & ~sel_lo), 1.0, 0.0).astype(BF16)
    q2 = q * q
    q2_hi = q2.astype(BF16)
    q2_lo = (q2 - q2_hi.astype(F32)).astype(BF16)
    nt = (((1,), (1,)), ((), ()))
    ss = (lax.dot_general(sel, q2_hi, nt, preferred_element_type=F32)
          + lax.dot_general(sel, q2_lo, nt, preferred_element_type=F32))
    r = lax.rsqrt(ss / HEAD_DIM + EPS) * LOG2_E
    qg = (q * qg_ref[...]).astype(BF16)

    kb = jnp.concatenate([ref[...] for ref in k_refs], axis=0)
    vb = jnp.concatenate([ref[...] for ref in v_refs] + [jnp.zeros((KV_WIDTH - BAND, KV_WIDTH), F32)], axis=0)
    v_t = vb.T

    def head_softmax(s_raw, r_row, bias, sink):
        s = s_raw * r_row - bias
        m = jnp.maximum(jnp.max(s, axis=0, keepdims=True), sink)
        p = jnp.exp2(s - m)
        return p.astype(BF16), jnp.sum(p, axis=0, keepdims=True) + jnp.exp2(sink - m)

    for kv_tile in range(KV_WIDTH // LANES):
        kt = kb[:, kv_tile * LANES:(kv_tile + 1) * LANES]
        kt_r = pltpu.roll(kt, HEAD_DIM, axis=1)
        for par in range(2):
            hk = 2 * kv_tile + par
            k_e = jnp.where(lo, kt_r if par else kt, 0.0)
            k_o = jnp.where(lo, 0.0, kt if par else kt_r)
            kk = jnp.concatenate([k_e, k_o], axis=0).astype(BF16)
            cols = slice(hk * QUERIES_PER_KV, (hk + 1) * QUERIES_PER_KV)
            s_t = lax.dot_general(kk, qg[cols], nt, preferred_element_type=F32)
            bias = bias_ref[table, hk]
            p_e, d_e = head_softmax(s_t[:BAND], r[0:1, cols], bias[:BAND], sink_e_ref[hk])
            p_o, d_o = head_softmax(s_t[BAND:], r[1:2, cols], bias[BAND:], sink_o_ref[hk])
            v_h = v_t[hk * HEAD_DIM:(hk + 1) * HEAD_DIM, :BAND].astype(BF16)
            o_e = jnp.dot(v_h, p_e, preferred_element_type=F32) / d_e
            o_o = jnp.dot(v_h, p_o, preferred_element_type=F32) / d_o
            o = jnp.concatenate([o_e, o_o], axis=0).T
            for pr in range(PAIRS_PER_KV):
                pair = hk * PAIRS_PER_KV + pr
                o_ref[rows, pair * LANES:(pair + 1) * LANES] = o[pr * CHUNK:(pr + 1) * CHUNK].astype(BF16)


def _swa_kernel(q_ref, *refs, prompt):
    n_band = SWA_CHUNKS * BAND_BLOCKS
    k_refs, v_refs = refs[:n_band], refs[n_band:2 * n_band]
    qg_ref, bias_ref, sink_e_ref, sink_o_ref, o_ref = refs[2 * n_band:]
    for ci in range(SWA_CHUNKS):
        table = jnp.minimum(pl.program_id(0) * SWA_CHUNKS + ci, BAND_BLOCKS - 1) if prompt else 0
        band = slice(ci * BAND_BLOCKS, (ci + 1) * BAND_BLOCKS)
        _swa_chunk(q_ref, slice(ci * CHUNK, (ci + 1) * CHUNK), k_refs[band], v_refs[band],
                   qg_ref, bias_ref, table, sink_e_ref, sink_o_ref, o_ref)


def _swa_call(prompt, n_rows, q_block0, band_k, band_v, operands, q_gain2, bias, sink_e, sink_o):
    step_rows = SWA_CHUNKS * CHUNK
    const = lambda a: pl.BlockSpec(a.shape, lambda s: (0,) * a.ndim)
    vmem = (4 * step_rows * ATTN_WIDTH * 4 + 8 * SWA_CHUNKS * BAND * KV_WIDTH * 4 + 2 * bias.size * 4
            + SWA_CHUNKS * (12 << 20))
    return pl.pallas_call(
        functools.partial(_swa_kernel, prompt=prompt),
        grid=(n_rows // step_rows,),
        in_specs=[pl.BlockSpec((step_rows, ATTN_WIDTH), lambda s: (q_block0 + s, 0))] + band_k + band_v + [
            pl.BlockSpec((1, LANES), lambda s: (0, 0)), const(bias), const(sink_e), const(sink_o)],
        out_specs=pl.BlockSpec((step_rows, ATTN_WIDTH), lambda s: (s, 0)),
        out_shape=jax.ShapeDtypeStruct((n_rows, ATTN_WIDTH), BF16),
        compiler_params=_params(vmem, 1),
        name="swa_prompt" if prompt else "swa_sample",
    )(*operands, q_gain2, bias, sink_e, sink_o)


def _swa_prompt(z, kn, q_gain2, bias, sink_e, sink_o):
    def spec(ci, j, col):
        return pl.BlockSpec((CHUNK, KV_WIDTH),
                            lambda s: (jnp.maximum(s * SWA_CHUNKS + ci - (BAND_BLOCKS - 1) + j, 0), col))
    slots = [(ci, j) for ci in range(SWA_CHUNKS) for j in range(BAND_BLOCKS)]
    band_k = [spec(ci, j, 0) for ci, j in slots]
    band_v = [spec(ci, j, Z_V_BLOCK) for ci, j in slots]
    operands = [z] + [kn] * len(slots) + [z] * len(slots)
    return _swa_call(True, M_PROMPT, 0, band_k, band_v, operands, q_gain2, bias, sink_e, sink_o)


def _swa_sample(z, kn, cache_k, cache_v, q_gain2, bias, sink_e, sink_o):
    cache_blocks = WINDOW // CHUNK
    first_chunk = N_CHUNKS_PROMPT

    def specs(new_col):
        out = []
        for ci in range(SWA_CHUNKS):
            out += [pl.BlockSpec((None, CHUNK, KV_WIDTH), functools.partial(lambda s, ci, j: (s * SWA_CHUNKS + ci, j, 0),
                                                                            ci=ci, j=j))
                    for j in range(cache_blocks)]
            out.append(pl.BlockSpec((CHUNK, KV_WIDTH),
                                    functools.partial(lambda s, ci: (first_chunk + s * SWA_CHUNKS + ci, new_col), ci=ci)))
        return out

    operands = [z] + ([cache_k] * cache_blocks + [kn]) * SWA_CHUNKS + ([cache_v] * cache_blocks + [z]) * SWA_CHUNKS
    return _swa_call(False, M_SAMPLE, M_PROMPT // (SWA_CHUNKS * CHUNK), specs(0), specs(Z_V_BLOCK), operands,
                     q_gain2, bias[BAND_BLOCKS - 1:], sink_e, sink_o)


def _swa_constants(sinks):
    slopes = 2.0 ** (-8.0 * jnp.arange(1, N_HEADS + 1, dtype=F32) / N_HEADS)
    key = jnp.arange(BAND)
    dist = jnp.abs(jnp.arange(CHUNK)[None, :] + WINDOW - key[:, None]).astype(F32)
    head = (jnp.arange(N_KV_HEADS)[:, None, None] * (2 * PAIRS_PER_KV)
            + jnp.arange(2)[None, :, None] + 2 * jnp.arange(PAIRS_PER_KV)[None, None, :])
    bias = slopes[head][:, :, None, :, None] * dist[None, None, :, None, :]
    first_valid = jnp.maximum(WINDOW - CHUNK * jnp.arange(BAND_BLOCKS), 0)
    valid = key[None, :] >= first_valid[:, None]
    tables = jnp.where(valid[:, None, None, :, None, None], bias[None] * LOG2_E, -NEG_INF)
    tables = tables.reshape(BAND_BLOCKS, N_KV_HEADS, 2 * BAND, QUERIES_PER_KV)
    per_pair = sinks.astype(F32).reshape(N_KV_HEADS, PAIRS_PER_KV, 2) * LOG2_E
    rows = jnp.repeat(per_pair, CHUNK, axis=1)
    return tables, rows[:, None, :, 0], rows[:, None, :, 1]


POOL_TILES_PROMPT = M_PROMPT // TP_POOL
BATCH_PER_POOL_TILE = TP_POOL // DEC_SEQ


def _trailing_sum(ext, w):
    s, k = ext, 1
    while k < w:
        s = s + pltpu.roll(s, k, axis=0)
        k *= 2
    return s


def _pool_kernel(u_ref, halo_ref, state_ref, w_ref, scale_ref, o_ref, d_ref):
    i = pl.program_id(0)

    def window_diffs(ext, cur, cnt, w):
        return _trailing_sum(ext, w)[HALO:] / cnt - cur

    @pl.when(i < POOL_TILES_PROMPT)
    def _():
        pos = i * TP_POOL + lax.broadcasted_iota(jnp.int32, (TP_POOL, 1), 0)
        for g, w in enumerate(POOL_WINDOWS):
            cols = slice(g * POOL_GROUP_WIDTH, (g + 1) * POOL_GROUP_WIDTH)
            cur = u_ref[:, cols]
            halo = jnp.where(i > 0, halo_ref[:, cols], 0.0)
            ext = jnp.concatenate([halo, cur], axis=0)
            cnt = jnp.minimum(pos + 1, w).astype(F32)
            d_ref[:, cols] = window_diffs(ext, cur, cnt, w)

    @pl.when(i >= POOL_TILES_PROMPT)
    def _():
        for g, w in enumerate(POOL_WINDOWS):
            cols = slice(g * POOL_GROUP_WIDTH, (g + 1) * POOL_GROUP_WIDTH)
            for b in range(BATCH_PER_POOL_TILE):
                rows = slice(b * DEC_SEQ, (b + 1) * DEC_SEQ)
                cur = u_ref[rows, cols]
                ext = jnp.concatenate([state_ref[b, :, cols], cur], axis=0)
                d_ref[rows, cols] = window_diffs(ext, cur, float(w), w)

    for g in range(len(POOL_WINDOWS)):
        cols = slice(g * POOL_GROUP_WIDTH, (g + 1) * POOL_GROUP_WIDTH)
        y = jnp.dot(d_ref[:, cols].astype(BF16), w_ref[cols, :], preferred_element_type=F32)
        o_ref[:, cols] = (y * scale_ref[:, cols]).astype(BF16)


def _pool(z, state_halo, w_pool, scale, layer):
    halo_blocks_per_tile = TP_POOL // HALO
    vmem = (2 * TP_POOL * POOL_WIDTH * 4 + 2 * HALO * POOL_WIDTH * 4 + 2 * BATCH_PER_POOL_TILE * HALO * POOL_WIDTH * 4
            + 2 * POOL_WIDTH * POOL_GROUP_WIDTH * 2 + 2 * TP_POOL * POOL_WIDTH * 2 + TP_POOL * POOL_WIDTH * 4
            + (8 << 20))
    return pl.pallas_call(
        _pool_kernel,
        grid=(M_ALL // TP_POOL,),
        in_specs=[
            pl.BlockSpec((TP_POOL, POOL_WIDTH), lambda i: (i, Z_U0 // POOL_WIDTH)),
            pl.BlockSpec((HALO, POOL_WIDTH),
                         lambda i: (jnp.maximum(i * halo_blocks_per_tile - 1, 0), Z_U0 // POOL_WIDTH)),
            pl.BlockSpec((BATCH_PER_POOL_TILE, HALO, POOL_WIDTH),
                         lambda i: (jnp.maximum(i - POOL_TILES_PROMPT, 0), 0, 0)),
            pl.BlockSpec((None, POOL_WIDTH, POOL_GROUP_WIDTH), lambda i: (layer, 0, 0)),
            pl.BlockSpec((1, POOL_WIDTH), lambda i: (0, 0)),
        ],
        out_specs=pl.BlockSpec((TP_POOL, POOL_WIDTH), lambda i: (i, 0)),
        out_shape=jax.ShapeDtypeStruct((M_ALL, POOL_WIDTH), BF16),
        scratch_shapes=[pltpu.VMEM((TP_POOL, POOL_WIDTH), F32)],
        compiler_params=_params(vmem, 1),
        name="pool",
    )(z, z, state_halo, w_pool, scale)


def _mixer_out_kernel(ap_ref, as_ref, p_ref, xp_ref, xs_ref, wa_ref, wp_ref, o_ref):
    prompt = pl.program_id(0) < PROMPT_TILES
    a = jnp.where(prompt, ap_ref[...], as_ref[...])
    y = jnp.dot(a, wa_ref[...], preferred_element_type=F32)
    y = y + jnp.dot(p_ref[...], wp_ref[...], preferred_element_type=F32)
    o_ref[...] = jnp.where(prompt, xp_ref[...], xs_ref[...]) + y


def _mixer_out(attn_p, attn_s, pooled, x_p, x_s, w_out, layer):
    vmem = 2 * (3 * TM * ATTN_WIDTH * 2 + D_MODEL * TN_OUT * 2 + 3 * TM * TN_OUT * 4) + TM * ATTN_WIDTH * 2
    return pl.pallas_call(
        _mixer_out_kernel,
        grid=(ROW_TILES, D_MODEL // TN_OUT),
        in_specs=_x_specs(ATTN_WIDTH, lambda i, j: 0) + [
            pl.BlockSpec((TM, POOL_WIDTH), lambda i, j: (i, 0)),
        ] + _x_specs(TN_OUT, lambda i, j: j) + [
            pl.BlockSpec((None, ATTN_WIDTH, TN_OUT), lambda i, j: (layer, 0, j)),
            pl.BlockSpec((None, POOL_WIDTH, TN_OUT), lambda i, j: (layer, 1, j)),
        ],
        out_specs=pl.BlockSpec((TM, TN_OUT), lambda i, j: (i, j)),
        out_shape=jax.ShapeDtypeStruct((M_ALL, D_MODEL), F32),
        compiler_params=_params(vmem, 2),
        name="mixer_out",
    )(attn_p, attn_s, pooled, x_p, x_s, w_out, w_out)


def _mem_kv_kernel(m_ref, g_ref, wk_ref, wv_ref, kg_ref, k_ref, v_ref, h_ref):
    _norm_rows_to(m_ref, g_ref, h_ref, N_MEM)
    h = h_ref[...]
    k = jnp.dot(h, wk_ref[...], preferred_element_type=F32)
    for hd in range(MEM_HEADS):
        cols = slice(hd * MEM_HEAD_DIM, (hd + 1) * MEM_HEAD_DIM)
        k_ref[:, cols] = _rms_rows(k[:, cols], kg_ref[...])
    v_ref[...] = jnp.dot(h, wv_ref[...], preferred_element_type=F32)


def _mem_kv(mem, g, w_k, w_v, k_gain, layer):
    vmem = 2 * (N_MEM * D_MODEL * 4 + 2 * D_MODEL * MEM_WIDTH * 2 + 2 * N_MEM * MEM_WIDTH * 4) + N_MEM * D_MODEL * 2
    full = lambda shape: pl.BlockSpec(shape, lambda i: (0,) * len(shape))
    weight = pl.BlockSpec((None, D_MODEL, MEM_WIDTH), lambda i: (layer, 0, 0))
    return pl.pallas_call(
        _mem_kv_kernel,
        grid=(1,),
        in_specs=[full((N_MEM, D_MODEL)), full((1, D_MODEL)), weight, weight, full((1, MEM_HEAD_DIM))],
        out_specs=[full((N_MEM, MEM_WIDTH)), full((N_MEM, MEM_WIDTH))],
        out_shape=[jax.ShapeDtypeStruct((N_MEM, MEM_WIDTH), F32)] * 2,
        scratch_shapes=[pltpu.VMEM((N_MEM, D_MODEL), BF16)],
        compiler_params=_params(vmem, 1),
        name="mem_kv",
    )(mem, g, w_k, w_v, k_gain)


CROSS_TILES_PROMPT = M_PROMPT // TM_CROSS
BATCH_PER_CROSS_TILE = TM_CROSS // DEC_SEQ


def _cross_heads(q, mk_ref, mv_ref, b, qg, oc_ref, rows):
    for hd in range(MEM_HEADS):
        cols = slice(hd * MEM_HEAD_DIM, (hd + 1) * MEM_HEAD_DIM)
        qn = _rms_rows(q[:, cols], qg).astype(BF16)
        s = lax.dot_general(qn, mk_ref[b, :, cols].astype(BF16), (((1,), (1,)), ((), ())),
                            preferred_element_type=F32) * (MEM_HEAD_DIM ** -0.5)
        p = jnp.exp(s - jnp.max(s, axis=-1, keepdims=True))
        den = jnp.sum(p, axis=-1, keepdims=True)
        o = jnp.dot(p.astype(BF16), mv_ref[b, :, cols].astype(BF16), preferred_element_type=F32) / den
        oc_ref[rows, cols] = o.astype(BF16)


def _cross_kernel(xc_ref, xn_ref, g_ref, wq_ref, qg_ref, mkp_ref, mvp_ref, mks_ref, mvs_ref, wo_ref,
                  o_ref, h0_ref, h1_ref, q_ref, oc_ref):
    i = pl.program_id(0)

    @pl.when(i == 0)
    def _():
        _norm_rows_to(xc_ref, g_ref, h0_ref, TM_CROSS)

    def project(h_cur, h_next):
        q_ref[...] = jnp.dot(h_cur[...], wq_ref[...], preferred_element_type=F32)
        _norm_rows_inline(lambda rows, cols: xn_ref[rows, cols], g_ref, h_next, TM_CROSS)

    pl.when(i % 2 == 0)(functools.partial(project, h0_ref, h1_ref))
    pl.when(i % 2 == 1)(functools.partial(project, h1_ref, h0_ref))

    @pl.when(i < CROSS_TILES_PROMPT)
    def _():
        _cross_heads(q_ref[...], mkp_ref, mvp_ref, 0, qg_ref[...], oc_ref, slice(0, TM_CROSS))

    @pl.when(i >= CROSS_TILES_PROMPT)
    def _():
        for b in range(BATCH_PER_CROSS_TILE):
            rows = slice(b * DEC_SEQ, (b + 1) * DEC_SEQ)
            _cross_heads(q_ref[rows, :], mks_ref, mvs_ref, b, qg_ref[...], oc_ref, rows)

    o_ref[...] = xc_ref[...] + jnp.dot(oc_ref[...], wo_ref[...], preferred_element_type=F32)


def _cross(x, g, w_q, q_gain, mk_p, mv_p, mk_s, mv_s, w_o, layer):
    const = lambda shape: pl.BlockSpec(shape, lambda i: (0,) * len(shape))
    once = lambda shape: pl.BlockSpec((None,) + shape, lambda i: (layer, 0, 0), pipeline_mode=pl.Buffered(1))
    sample_mem = pl.BlockSpec((BATCH_PER_CROSS_TILE, N_MEM, MEM_WIDTH),
                              lambda i: (jnp.maximum(i - CROSS_TILES_PROMPT, 0), 0, 0))
    n_tiles = M_ALL // TM_CROSS
    vmem = (6 * TM_CROSS * D_MODEL * 4 + 2 * D_MODEL * MEM_WIDTH * 2
            + 4 * (1 + BATCH_PER_CROSS_TILE) * N_MEM * MEM_WIDTH * 4
            + 2 * TM_CROSS * D_MODEL * 2 + TM_CROSS * MEM_WIDTH * 6 + (6 << 20))
    return pl.pallas_call(
        _cross_kernel,
        grid=(n_tiles,),
        in_specs=[
            pl.BlockSpec((TM_CROSS, D_MODEL), lambda i: (i, 0)),
            pl.BlockSpec((TM_CROSS, D_MODEL), lambda i: (jnp.minimum(i + 1, n_tiles - 1), 0)),
            const((1, D_MODEL)),
            once((D_MODEL, MEM_WIDTH)),
            const((1, MEM_HEAD_DIM)),
            const(mk_p.shape), const(mv_p.shape), sample_mem, sample_mem,
            once((MEM_WIDTH, D_MODEL)),
        ],
        out_specs=pl.BlockSpec((TM_CROSS, D_MODEL), lambda i: (i, 0)),
        out_shape=jax.ShapeDtypeStruct((M_ALL, D_MODEL), F32),
        scratch_shapes=[pltpu.VMEM((TM_CROSS, D_MODEL), BF16)] * 2 + [pltpu.VMEM((TM_CROSS, MEM_WIDTH), F32),
                        pltpu.VMEM((TM_CROSS, MEM_WIDTH), BF16)],
        compiler_params=_params(vmem, 1),
        name="cross",
    )(x, x, g, w_q, q_gain, mk_p, mv_p, mk_s, mv_s, w_o)


GROUP = TF_SAMPLE
GROUPS_PER_TILE = TF_FFN // GROUP
GROUPS_LAST = TF_LAST // GROUP
N_GROUPS = D_FF // GROUP


def _swiglu_groups(gate_up, n_groups):
    acts = []
    for k in range(n_groups):
        gate = gate_up[:, 2 * k * GROUP:(2 * k + 1) * GROUP]
        up = gate_up[:, (2 * k + 1) * GROUP:(2 * k + 2) * GROUP]
        acts.append(gate * jax.nn.sigmoid(gate) * up)
    return (acts[0] if n_groups == 1 else jnp.concatenate(acts, axis=1)).astype(BF16)


def _ffn_prompt_kernel(x_ref, g_ref, wgu_ref, wd_ref, o_ref, h_ref):
    f = pl.program_id(1)

    @pl.when(f == 0)
    def _():
        _norm_rows_to(x_ref, g_ref, h_ref, TM, copy_ref=o_ref)

    def accumulate(n_groups):
        gate_up = jnp.dot(h_ref[...], wgu_ref[:, :2 * n_groups * GROUP], preferred_element_type=F32)
        act = _swiglu_groups(gate_up, n_groups)
        o_ref[...] += jnp.dot(act, wd_ref[:n_groups * GROUP, :], preferred_element_type=F32)

    pl.when(f < NF_FFN - 1)(functools.partial(accumulate, GROUPS_PER_TILE))
    pl.when(f == NF_FFN - 1)(functools.partial(accumulate, GROUPS_LAST))


def _ffn_prompt(x, g, w_gate_up, w_down):
    vmem = 3 * TM * D_MODEL * 4 + TM * D_MODEL * 2 + 2 * 3 * D_MODEL * TF_FFN * 2 + 4 * TM * TF_FFN * 4
    return pl.pallas_call(
        _ffn_prompt_kernel,
        grid=(PROMPT_TILES, NF_FFN),
        in_specs=[
            pl.BlockSpec((TM, D_MODEL), lambda i, f: (i, 0)),
            pl.BlockSpec((1, D_MODEL), lambda i, f: (0, 0)),
            pl.BlockSpec((D_MODEL, 2 * TF_FFN), lambda i, f: (0, f)),
            pl.BlockSpec((TF_FFN, D_MODEL), lambda i, f: (f, 0)),
        ],
        out_specs=pl.BlockSpec((TM, D_MODEL), lambda i, f: (i, 0), pipeline_mode=pl.Buffered(1)),
        out_shape=jax.ShapeDtypeStruct((M_PROMPT, D_MODEL), F32),
        scratch_shapes=[pltpu.VMEM((TM, D_MODEL), BF16)],
        compiler_params=_params(vmem, 2),
        name="ffn_prompt",
    )(x, g, w_gate_up, w_down)


def _ffn_sample_kernel(x_ref, g_ref, wg_ref, wu_ref, wd_ref, o_ref, wgu_out, wd_out, h_ref):
    @pl.when(pl.program_id(0) == 0)
    def _():
        _norm_rows_to(x_ref, g_ref, h_ref, TM, copy_ref=o_ref)

    wgu_out[:, :GROUP] = wg_ref[...].astype(BF16)
    wgu_out[:, GROUP:] = wu_ref[...].astype(BF16)
    wd_out[...] = wd_ref[...].astype(BF16)
    act = _swiglu_groups(jnp.dot(h_ref[...], wgu_out[...], preferred_element_type=F32), 1)
    o_ref[...] += jnp.dot(act, wd_out[...], preferred_element_type=F32)


def _ffn_sample(x, g, w_gate, w_up, w_down, layer):
    vmem = 2 * TM * D_MODEL * 4 + TM * D_MODEL * 2 + 2 * 3 * D_MODEL * GROUP * 6 + 4 * TM * GROUP * 4
    once = pl.Buffered(1)
    col_f32 = pl.BlockSpec((None, D_MODEL, GROUP), lambda f: (layer, 0, f))
    return pl.pallas_call(
        _ffn_sample_kernel,
        grid=(N_GROUPS,),
        in_specs=[
            pl.BlockSpec((TM, D_MODEL), lambda f: (PROMPT_TILES, 0), pipeline_mode=once),
            pl.BlockSpec((1, D_MODEL), lambda f: (0, 0)),
            col_f32, col_f32,
            pl.BlockSpec((None, GROUP, D_MODEL), lambda f: (layer, f, 0)),
        ],
        out_specs=[
            pl.BlockSpec((TM, D_MODEL), lambda f: (0, 0), pipeline_mode=once),
            pl.BlockSpec((D_MODEL, 2 * GROUP), lambda f: (0, f)),
            pl.BlockSpec((GROUP, D_MODEL), lambda f: (f, 0)),
        ],
        out_shape=[jax.ShapeDtypeStruct((M_SAMPLE, D_MODEL), F32),
                   jax.ShapeDtypeStruct((D_MODEL, 2 * D_FF), BF16),
                   jax.ShapeDtypeStruct((D_FF, D_MODEL), BF16)],
        scratch_shapes=[pltpu.VMEM((TM, D_MODEL), BF16)],
        compiler_params=_params(vmem, 1),
        name="ffn_sample",
    )(x, g, w_gate, w_up, w_down)


def _row(v):
    return v.astype(F32).reshape(1, -1)


def kernel(x_prompt, x_sample, cache_attn_k, cache_attn_v, state_pool, cache_mem_k, cache_mem_v, mem_prompt,
           g_mix, w_in, q_norm, k_norm, attn_sinks, w_pool, pool_scale, w_out,
           g_cross, g_mem, w_q_mem, w_k_mem, w_v_mem, q_norm_mem, k_norm_mem, w_o_mem,
           g_ffn, w_gate, w_up, w_down):
    x_p, x_s = x_prompt.reshape(M_PROMPT, D_MODEL), x_sample.reshape(M_SAMPLE, D_MODEL)
    mem = mem_prompt.reshape(N_MEM, D_MODEL)

    w_in_b = _cast_w_in(w_in)
    w_out_b, w_q_b, w_k_b, w_v_b, w_o_b = (_cast_rows(w) for w in (w_out, w_q_mem, w_k_mem, w_v_mem, w_o_mem))
    w_pool_b = _cast_rows(w_pool.reshape(DEPTH, POOL_WIDTH, POOL_GROUP_WIDTH))

    ak_p, av_p, pl_p, mk_p, mv_p, ak_s, av_s, pl_s = ([] for _ in range(8))
    for l in range(DEPTH):
        z = _mixer_in(x_p, x_s, _row(g_mix[l]), w_in_b, l)
        kn = _k_norm(z, _row(jnp.tile(k_norm[l], 2)))
        cache_k = cache_attn_k[l].astype(F32).reshape(DEC_BATCH, WINDOW, KV_WIDTH)
        cache_v = cache_attn_v[l].astype(F32).reshape(DEC_BATCH, WINDOW, KV_WIDTH)
        bias, sink_e, sink_o = _swa_constants(attn_sinks[l])
        q_gain2 = _row(jnp.tile(q_norm[l], 2)) * (HEAD_DIM ** -0.5)
        attn_p = _swa_prompt(z, kn, q_gain2, bias, sink_e, sink_o)
        attn_s = _swa_sample(z, kn, cache_k, cache_v, q_gain2, bias, sink_e, sink_o)

        state_halo = jnp.pad(state_pool[l].astype(F32), ((0, 0), (HALO - POOL_STATE, 0), (0, 0)))
        pooled = _pool(z, state_halo, w_pool_b, _row(pool_scale[l]), l)
        x = _mixer_out(attn_p, attn_s, pooled, x_p, x_s, w_out_b, l)

        v = z[:, Z_V0:Z_V0 + KV_WIDTH]
        kn_s = kn[M_PROMPT:].reshape(DEC_BATCH, DEC_SEQ, KV_WIDTH)
        v_s = v[M_PROMPT:].reshape(DEC_BATCH, DEC_SEQ, KV_WIDTH)
        ak_p.append(kn[M_PROMPT - WINDOW:M_PROMPT].reshape(1, WINDOW, N_KV_HEADS, HEAD_DIM))
        av_p.append(v[M_PROMPT - WINDOW:M_PROMPT].reshape(1, WINDOW, N_KV_HEADS, HEAD_DIM))
        pl_p.append(z[M_PROMPT - POOL_STATE:M_PROMPT, Z_U0:Z_U0 + POOL_WIDTH].reshape(1, POOL_STATE, POOL_WIDTH))
        ak_s.append(jnp.concatenate([cache_k, kn_s], axis=1)[:, -WINDOW:]
                    .reshape(DEC_BATCH, WINDOW, N_KV_HEADS, HEAD_DIM))
        av_s.append(jnp.concatenate([cache_v, v_s], axis=1)[:, -WINDOW:]
                    .reshape(DEC_BATCH, WINDOW, N_KV_HEADS, HEAD_DIM))
        u_s = z[M_PROMPT:, Z_U0:Z_U0 + POOL_WIDTH].reshape(DEC_BATCH, DEC_SEQ, POOL_WIDTH)
        pl_s.append(jnp.concatenate([state_pool[l].astype(F32), u_s], axis=1)[:, -POOL_STATE:])

        mk, mv = _mem_kv(mem, _row(g_mem[l]), w_k_b, w_v_b, _row(k_norm_mem[l]), l)
        mk_p.append(mk.reshape(1, N_MEM, MEM_HEADS, MEM_HEAD_DIM))
        mv_p.append(mv.reshape(1, N_MEM, MEM_HEADS, MEM_HEAD_DIM))
        x = _cross(x, _row(g_cross[l]), w_q_b, _row(q_norm_mem[l]),
                   mk.reshape(1, N_MEM, MEM_WIDTH), mv.reshape(1, N_MEM, MEM_WIDTH),
                   cache_mem_k[l].reshape(DEC_BATCH, N_MEM, MEM_WIDTH).astype(F32),
                   cache_mem_v[l].reshape(DEC_BATCH, N_MEM, MEM_WIDTH).astype(F32),
                   w_o_b, l)

        x_s, w_gate_up_b, w_down_b = _ffn_sample(x, _row(g_ffn[l]), w_gate, w_up, w_down, l)
        x_p = _ffn_prompt(x, _row(g_ffn[l]), w_gate_up_b, w_down_b)

    return (x_p.reshape(1, SEQ, D_MODEL), x_s.reshape(DEC_BATCH, DEC_SEQ, D_MODEL),
            jnp.stack(ak_p), jnp.stack(av_p), jnp.stack(pl_p), jnp.stack(mk_p), jnp.stack(mv_p),
            jnp.stack(ak_s), jnp.stack(av_s), jnp.stack(pl_s))
```

```python
import functools

import jax
import jax.numpy as jnp
from jax import lax
from jax.experimental import pallas as pl
from jax.experimental.pallas import tpu as pltpu

F32 = jnp.float32
BF16 = jnp.bfloat16

D_MODEL = 4096
SEQ = 8192
DEPTH = 2
DEC_BATCH = 8
DEC_SEQ = 64
PAST_LEN = 4096
CHUNK = 64
WINDOW = 128
ATTN_WIDTH = D_MODEL // 2
POOL_WIDTH = D_MODEL - ATTN_WIDTH
HEAD_DIM = 64
N_HEADS = ATTN_WIDTH // HEAD_DIM
N_KV_HEADS = N_HEADS // 8
KV_WIDTH = N_KV_HEADS * HEAD_DIM
POOL_WINDOWS = (2, 4, 8, 16)
POOL_GROUP_WIDTH = POOL_WIDTH // len(POOL_WINDOWS)
POOL_STATE = max(POOL_WINDOWS) - 1
IN_WIDTH = ATTN_WIDTH + 2 * KV_WIDTH + POOL_WIDTH
N_MEM = 256
MEM_HEADS = 4
MEM_HEAD_DIM = 128
MEM_WIDTH = MEM_HEADS * MEM_HEAD_DIM
D_FF = -(-8 * D_MODEL // (3 * 256)) * 256
EPS = 1e-6
NEG_INF = -1e30

M_PROMPT = SEQ
M_SAMPLE = DEC_BATCH * DEC_SEQ
M_ALL = M_PROMPT + M_SAMPLE
N_CHUNKS_PROMPT = M_PROMPT // CHUNK
N_CHUNKS = M_ALL // CHUNK
BAND = WINDOW + CHUNK
BAND_BLOCKS = BAND // CHUNK

Z_U0 = ATTN_WIDTH
Z_K0 = ATTN_WIDTH + POOL_WIDTH
Z_V0 = Z_K0 + KV_WIDTH

LANES = 128
SUBLANES = 8
BF16_TILE_ROWS = 16
HALO = BF16_TILE_ROWS
VMEM_LIMIT_CAP = 60000 * 1024
CAST_BLOCK_BYTES = 8 << 20

TM = 512
PROMPT_TILES = M_PROMPT // TM
ROW_TILES = M_ALL // TM
TN_IN = 768
TN_OUT = 1024
TF_FFN = 512
NF_FFN = -(-D_FF // TF_FFN)
TF_LAST = D_FF - (NF_FFN - 1) * TF_FFN
TF_SAMPLE = 256
TM_CROSS = 256
TM_KN = 512
TP_POOL = 512
NORM_ROWS = BF16_TILE_ROWS
NORM_UNROLL = 8
W_IN_COL_BLOCK = KV_WIDTH


def _params(vmem_bytes, n_axes):
    limit = min(int(vmem_bytes * 1.2) + (6 << 20), VMEM_LIMIT_CAP)
    return pltpu.CompilerParams(dimension_semantics=("arbitrary",) * n_axes, vmem_limit_bytes=limit)


def _rms_rows(x, g):
    return x * lax.rsqrt(jnp.mean(x * x, axis=-1, keepdims=True) + EPS) * g


def _norm_rows_to(x_ref, g_ref, h_ref, n_rows, copy_ref=None):
    def body(r, carry):
        rows = pl.ds(pl.multiple_of(r * NORM_ROWS, NORM_ROWS), NORM_ROWS)
        x = x_ref[rows, :]
        h_ref[rows, :] = _rms_rows(x, g_ref[...]).astype(BF16)
        if copy_ref is not None:
            copy_ref[rows, :] = x
        return carry

    lax.fori_loop(0, n_rows // NORM_ROWS, body, 0, unroll=NORM_UNROLL)


def _half_lane_rms(x, gain2):
    lo = lax.broadcasted_iota(jnp.int32, (1, LANES), 1) < HEAD_DIM
    x2 = x * x
    ss_lo = jnp.sum(jnp.where(lo, x2, 0.0), axis=-1, keepdims=True)
    ss_hi = jnp.sum(jnp.where(lo, 0.0, x2), axis=-1, keepdims=True)
    r = jnp.where(lo, lax.rsqrt(ss_lo / HEAD_DIM + EPS), lax.rsqrt(ss_hi / HEAD_DIM + EPS))
    return x * r * gain2


def _cast_kernel(w_ref, o_ref):
    o_ref[...] = w_ref[...].astype(BF16)


def _cast_rows(w):
    depth, rows, cols = w.shape
    tr = max(t for t in range(BF16_TILE_ROWS, rows + 1, BF16_TILE_ROWS)
             if rows % t == 0 and t * cols * 4 <= CAST_BLOCK_BYTES)
    spec = pl.BlockSpec((None, tr, cols), lambda l, i: (l, i, 0))
    return pl.pallas_call(
        _cast_kernel, grid=(depth, rows // tr), in_specs=[spec], out_specs=spec,
        out_shape=jax.ShapeDtypeStruct(w.shape, BF16),
        compiler_params=_params(2 * tr * cols * 6, 2), name="cast_rows",
    )(w)


def _w_in_source_block(j):
    q_blocks, u_blocks = ATTN_WIDTH // W_IN_COL_BLOCK, POOL_WIDTH // W_IN_COL_BLOCK
    kv_blocks = 2 * KV_WIDTH // W_IN_COL_BLOCK
    return jnp.where(j < q_blocks, j, jnp.where(j < q_blocks + u_blocks, j + kv_blocks, j - u_blocks))


def _cast_w_in(w):
    depth = w.shape[0]
    return pl.pallas_call(
        _cast_kernel, grid=(depth, IN_WIDTH // W_IN_COL_BLOCK),
        in_specs=[pl.BlockSpec((None, D_MODEL, W_IN_COL_BLOCK), lambda l, j: (l, 0, _w_in_source_block(j)))],
        out_specs=pl.BlockSpec((None, D_MODEL, W_IN_COL_BLOCK), lambda l, j: (l, 0, j)),
        out_shape=jax.ShapeDtypeStruct(w.shape, BF16),
        compiler_params=_params(2 * D_MODEL * W_IN_COL_BLOCK * 6, 2), name="cast_w_in",
    )(w)


def _mixer_in_kernel(xp_ref, xs_ref, g_ref, w_ref, z_ref, h_ref):
    i, j = pl.program_id(0), pl.program_id(1)

    @pl.when((j == 0) & (i < PROMPT_TILES))
    def _():
        _norm_rows_to(xp_ref, g_ref, h_ref, TM)

    @pl.when((j == 0) & (i >= PROMPT_TILES))
    def _():
        _norm_rows_to(xs_ref, g_ref, h_ref, TM)

    z_ref[...] = jnp.dot(h_ref[...], w_ref[...], preferred_element_type=F32)


def _x_specs(width, col):
    return [
        pl.BlockSpec((TM, width), lambda i, j: (jnp.minimum(i, PROMPT_TILES - 1), col(i, j))),
        pl.BlockSpec((TM, width), lambda i, j: (0, jnp.where(i >= PROMPT_TILES, col(i, j), 0))),
    ]


def _mixer_in(x_p, x_s, g, w, layer):
    vmem = 4 * TM * D_MODEL * 4 + TM * D_MODEL * 2 + 2 * D_MODEL * TN_IN * 2 + 2 * TM * TN_IN * 4
    return pl.pallas_call(
        _mixer_in_kernel,
        grid=(ROW_TILES, IN_WIDTH // TN_IN),
        in_specs=_x_specs(D_MODEL, lambda i, j: 0) + [
            pl.BlockSpec((1, D_MODEL), lambda i, j: (0, 0)),
            pl.BlockSpec((None, D_MODEL, TN_IN), lambda i, j: (layer, 0, j)),
        ],
        out_specs=pl.BlockSpec((TM, TN_IN), lambda i, j: (i, j)),
        out_shape=jax.ShapeDtypeStruct((M_ALL, IN_WIDTH), F32),
        scratch_shapes=[pltpu.VMEM((TM, D_MODEL), BF16)],
        compiler_params=_params(vmem, 2),
        name="mixer_in",
    )(x_p, x_s, g, w)


def _k_norm_kernel(k_ref, g_ref, o_ref):
    for t in range(KV_WIDTH // LANES):
        cols = slice(t * LANES, (t + 1) * LANES)
        o_ref[:, cols] = _half_lane_rms(k_ref[:, cols], g_ref[...])


def _k_norm(z, gain2):
    vmem = 4 * TM_KN * KV_WIDTH * 4
    return pl.pallas_call(
        _k_norm_kernel,
        grid=(M_ALL // TM_KN,),
        in_specs=[
            pl.BlockSpec((TM_KN, KV_WIDTH), lambda i: (i, Z_K0 // KV_WIDTH)),
            pl.BlockSpec((1, LANES), lambda i: (0, 0)),
        ],
        out_specs=pl.BlockSpec((TM_KN, KV_WIDTH), lambda i: (i, 0)),
        out_shape=jax.ShapeDtypeStruct((M_ALL, KV_WIDTH), F32),
        compiler_params=_params(vmem, 1),
        name="k_norm",
    )(z, gain2)


PAIRS_PER_KV = N_HEADS // N_KV_HEADS // 2
QUERIES_PER_KV = PAIRS_PER_KV * CHUNK
N_PAIRS = N_HEADS // 2
SWA_CHUNKS = 4
LOG2_E = 1.4426950408889634
Z_V_BLOCK = Z_V0 // KV_WIDTH


def _swa_chunk(q_ref, rows, k_refs, v_refs, qg_ref, bias_ref, table, sink_e_ref, sink_o_ref, o_ref):
    lo = lax.broadcasted_iota(jnp.int32, (1, LANES), 1) < HEAD_DIM

    q = jnp.concatenate([q_ref[rows, p * LANES:(p + 1) * LANES] for p in range(N_PAIRS)], axis=0)

    sel_row = lax.broadcasted_iota(jnp.int32, (SUBLANES, LANES), 0)
    sel_lo = lax.broadcasted_iota(jnp.int32, (SUBLANES, LANES), 1) < HEAD_DIM
    sel = jnp.where(((sel_row == 0) & sel_lo) | ((sel_row == 1) & ~sel_lo), 1.0, 0.0).astype(BF16)
    q2 = q * q
    q2_hi = q2.astype(BF16)
    q2_lo = (q2 - q2_hi.astype(F32)).astype(BF16)
    nt = (((1,), (1,)), ((), ()))
    ss = (lax.dot_general(sel, q2_hi, nt, preferred_element_type=F32)
          + lax.dot_general(sel, q2_lo, nt, preferred_element_type=F32))
    r = lax.rsqrt(ss / HEAD_DIM + EPS) * LOG2_E
    qg = (q * qg_ref[...]).astype(BF16)

    kb = jnp.concatenate([ref[...] for ref in k_refs], axis=0)
    vb = jnp.concatenate([ref[...] for ref in v_refs] + [jnp.zeros((KV_WIDTH - BAND, KV_WIDTH), F32)], axis=0)
    v_t = vb.T

    def head_softmax(s_raw, r_row, bias, sink):
        s = s_raw * r_row - bias
        m = jnp.maximum(jnp.max(s, axis=0, keepdims=True), sink)
        p = jnp.exp2(s - m)
        return p.astype(BF16), jnp.sum(p, axis=0, keepdims=True) + jnp.exp2(sink - m)

    for kv_tile in range(KV_WIDTH // LANES):
        kt = kb[:, kv_tile * LANES:(kv_tile + 1) * LANES]
        kt_r = pltpu.roll(kt, HEAD_DIM, axis=1)
        for par in range(2):
            hk = 2 * kv_tile + par
            k_e = jnp.where(lo, kt_r if par else kt, 0.0)
            k_o = jnp.where(lo, 0.0, kt if par else kt_r)
            kk = jnp.concatenate([k_e, k_o], axis=0).astype(BF16)
            cols = slice(hk * QUERIES_PER_KV, (hk + 1) * QUERIES_PER_KV)
            s_t = lax.dot_general(kk, qg[cols], nt, preferred_element_type=F32)
            bias = bias_ref[table, hk]
            p_e, d_e = head_softmax(s_t[:BAND], r[0:1, cols], bias[:BAND], sink_e_ref[hk])
            p_o, d_o = head_softmax(s_t[BAND:], r[1:2, cols], bias[BAND:], sink_o_ref[hk])
            v_h = v_t[hk * HEAD_DIM:(hk + 1) * HEAD_DIM, :BAND].astype(BF16)
            o_e = jnp.dot(v_h, p_e, preferred_element_type=F32) / d_e
            o_o = jnp.dot(v_h, p_o, preferred_element_type=F32) / d_o
            o = jnp.concatenate([o_e, o_o], axis=0).T
            for pr in range(PAIRS_PER_KV):
                pair = hk * PAIRS_PER_KV + pr
                o_ref[rows, pair * LANES:(pair + 1) * LANES] = o[pr * CHUNK:(pr + 1) * CHUNK].astype(BF16)


def _swa_kernel(q_ref, *refs, prompt):
    n_band = SWA_CHUNKS * BAND_BLOCKS
    k_refs, v_refs = refs[:n_band], refs[n_band:2 * n_band]
    qg_ref, bias_ref, sink_e_ref, sink_o_ref, o_ref = refs[2 * n_band:]
    for ci in range(SWA_CHUNKS):
        table = jnp.minimum(pl.program_id(0) * SWA_CHUNKS + ci, BAND_BLOCKS - 1) if prompt else 0
        band = slice(ci * BAND_BLOCKS, (ci + 1) * BAND_BLOCKS)
        _swa_chunk(q_ref, slice(ci * CHUNK, (ci + 1) * CHUNK), k_refs[band], v_refs[band],
                   qg_ref, bias_ref, table, sink_e_ref, sink_o_ref, o_ref)


def _swa_call(prompt, n_rows, q_block0, band_k, band_v, operands, q_gain2, bias, sink_e, sink_o):
    step_rows = SWA_CHUNKS * CHUNK
    const = lambda a: pl.BlockSpec(a.shape, lambda s: (0,) * a.ndim)
    vmem = (4 * step_rows * ATTN_WIDTH * 4 + 8 * SWA_CHUNKS * BAND * KV_WIDTH * 4 + 2 * bias.size * 4
            + SWA_CHUNKS * (12 << 20))
    return pl.pallas_call(
        functools.partial(_swa_kernel, prompt=prompt),
        grid=(n_rows // step_rows,),
        in_specs=[pl.BlockSpec((step_rows, ATTN_WIDTH), lambda s: (q_block0 + s, 0))] + band_k + band_v + [
            pl.BlockSpec((1, LANES), lambda s: (0, 0)), const(bias), const(sink_e), const(sink_o)],
        out_specs=pl.BlockSpec((step_rows, ATTN_WIDTH), lambda s: (s, 0)),
        out_shape=jax.ShapeDtypeStruct((n_rows, ATTN_WIDTH), BF16),
        compiler_params=_params(vmem, 1),
        name="swa_prompt" if prompt else "swa_sample",
    )(*operands, q_gain2, bias, sink_e, sink_o)


def _swa_prompt(z, kn, q_gain2, bias, sink_e, sink_o):
    def spec(ci, j, col):
        return pl.BlockSpec((CHUNK, KV_WIDTH),
                            lambda s: (jnp.maximum(s * SWA_CHUNKS + ci - (BAND_BLOCKS - 1) + j, 0), col))
    slots = [(ci, j) for ci in range(SWA_CHUNKS) for j in range(BAND_BLOCKS)]
    band_k = [spec(ci, j, 0) for ci, j in slots]
    band_v = [spec(ci, j, Z_V_BLOCK) for ci, j in slots]
    operands = [z] + [kn] * len(slots) + [z] * len(slots)
    return _swa_call(True, M_PROMPT, 0, band_k, band_v, operands, q_gain2, bias, sink_e, sink_o)


def _swa_sample(z, kn, cache_k, cache_v, q_gain2, bias, sink_e, sink_o):
    cache_blocks = WINDOW // CHUNK
    first_chunk = N_CHUNKS_PROMPT

    def specs(new_col):
        out = []
        for ci in range(SWA_CHUNKS):
            out += [pl.BlockSpec((None, CHUNK, KV_WIDTH), functools.partial(lambda s, ci, j: (s * SWA_CHUNKS + ci, j, 0),
                                                                            ci=ci, j=j))
                    for j in range(cache_blocks)]
            out.append(pl.BlockSpec((CHUNK, KV_WIDTH),
                                    functools.partial(lambda s, ci: (first_chunk + s * SWA_CHUNKS + ci, new_col), ci=ci)))
        return out

    operands = [z] + ([cache_k] * cache_blocks + [kn]) * SWA_CHUNKS + ([cache_v] * cache_blocks + [z]) * SWA_CHUNKS
    return _swa_call(False, M_SAMPLE, M_PROMPT // (SWA_CHUNKS * CHUNK), specs(0), specs(Z_V_BLOCK), operands,
                     q_gain2, bias[BAND_BLOCKS - 1:], sink_e, sink_o)


def _swa_bias_tables():
    slopes = 2.0 ** (-8.0 * jnp.arange(1, N_HEADS + 1, dtype=F32) / N_HEADS)
    key = jnp.arange(BAND)
    dist = jnp.abs(jnp.arange(CHUNK)[None, :] + WINDOW - key[:, None]).astype(F32)
    head = (jnp.arange(N_KV_HEADS)[:, None, None] * (2 * PAIRS_PER_KV)
            + jnp.arange(2)[None, :, None] + 2 * jnp.arange(PAIRS_PER_KV)[None, None, :])
    bias = slopes[head][:, :, None, :, None] * dist[None, None, :, None, :]
    first_valid = jnp.maximum(WINDOW - CHUNK * jnp.arange(BAND_BLOCKS), 0)
    valid = key[None, :] >= first_valid[:, None]
    tables = jnp.where(valid[:, None, None, :, None, None], bias[None] * LOG2_E, -NEG_INF)
    return tables.reshape(BAND_BLOCKS, N_KV_HEADS, 2 * BAND, QUERIES_PER_KV)


def _swa_sink_rows(sinks):
    per_pair = sinks.astype(F32).reshape(N_KV_HEADS, PAIRS_PER_KV, 2) * LOG2_E
    rows = jnp.repeat(per_pair, CHUNK, axis=1)
    return rows[:, None, :, 0], rows[:, None, :, 1]


POOL_TILES_PROMPT = M_PROMPT // TP_POOL
BATCH_PER_POOL_TILE = TP_POOL // DEC_SEQ


def _trailing_sum(ext, w):
    s, k = ext, 1
    while k < w:
        s = s + pltpu.roll(s, k, axis=0)
        k *= 2
    return s


def _pool_kernel(u_ref, halo_ref, state_ref, w_ref, scale_ref, o_ref, d_ref):
    i = pl.program_id(0)

    def window_diffs(ext, cur, cnt, w):
        return _trailing_sum(ext, w)[HALO:] / cnt - cur

    @pl.when(i < POOL_TILES_PROMPT)
    def _():
        pos = i * TP_POOL + lax.broadcasted_iota(jnp.int32, (TP_POOL, 1), 0)
        for g, w in enumerate(POOL_WINDOWS):
            cols = slice(g * POOL_GROUP_WIDTH, (g + 1) * POOL_GROUP_WIDTH)
            cur = u_ref[:, cols]
            halo = jnp.where(i > 0, halo_ref[:, cols], 0.0)
            ext = jnp.concatenate([halo, cur], axis=0)
            cnt = jnp.minimum(pos + 1, w).astype(F32)
            d_ref[:, cols] = window_diffs(ext, cur, cnt, w)

    @pl.when(i >= POOL_TILES_PROMPT)
    def _():
        for g, w in enumerate(POOL_WINDOWS):
            cols = slice(g * POOL_GROUP_WIDTH, (g + 1) * POOL_GROUP_WIDTH)
            for b in range(BATCH_PER_POOL_TILE):
                rows = slice(b * DEC_SEQ, (b + 1) * DEC_SEQ)
                cur = u_ref[rows, cols]
                ext = jnp.concatenate([state_ref[b, :, cols], cur], axis=0)
                d_ref[rows, cols] = window_diffs(ext, cur, float(w), w)

    for g in range(len(POOL_WINDOWS)):
        cols = slice(g * POOL_GROUP_WIDTH, (g + 1) * POOL_GROUP_WIDTH)
        y = jnp.dot(d_ref[:, cols].astype(BF16), w_ref[cols, :], preferred_element_type=F32)
        o_ref[:, cols] = (y * scale_ref[:, cols]).astype(BF16)


def _pool(z, state_halo, w_pool, scale, layer):
    halo_blocks_per_tile = TP_POOL // HALO
    vmem = (2 * TP_POOL * POOL_WIDTH * 4 + 2 * HALO * POOL_WIDTH * 4 + 2 * BATCH_PER_POOL_TILE * HALO * POOL_WIDTH * 4
            + 2 * POOL_WIDTH * POOL_GROUP_WIDTH * 2 + 2 * TP_POOL * POOL_WIDTH * 2 + TP_POOL * POOL_WIDTH * 4
            + (8 << 20))
    return pl.pallas_call(
        _pool_kernel,
        grid=(M_ALL // TP_POOL,),
        in_specs=[
            pl.BlockSpec((TP_POOL, POOL_WIDTH), lambda i: (i, Z_U0 // POOL_WIDTH)),
            pl.BlockSpec((HALO, POOL_WIDTH),
                         lambda i: (jnp.maximum(i * halo_blocks_per_tile - 1, 0), Z_U0 // POOL_WIDTH)),
            pl.BlockSpec((BATCH_PER_POOL_TILE, HALO, POOL_WIDTH),
                         lambda i: (jnp.maximum(i - POOL_TILES_PROMPT, 0), 0, 0)),
            pl.BlockSpec((None, POOL_WIDTH, POOL_GROUP_WIDTH), lambda i: (layer, 0, 0)),
            pl.BlockSpec((1, POOL_WIDTH), lambda i: (0, 0)),
        ],
        out_specs=pl.BlockSpec((TP_POOL, POOL_WIDTH), lambda i: (i, 0)),
        out_shape=jax.ShapeDtypeStruct((M_ALL, POOL_WIDTH), BF16),
        scratch_shapes=[pltpu.VMEM((TP_POOL, POOL_WIDTH), F32)],
        compiler_params=_params(vmem, 1),
        name="pool",
    )(z, z, state_halo, w_pool, scale)


def _mixer_out_kernel(ap_ref, as_ref, p_ref, xp_ref, xs_ref, wa_ref, wp_ref, o_ref):
    prompt = pl.program_id(0) < PROMPT_TILES
    a = jnp.where(prompt, ap_ref[...], as_ref[...])
    y = jnp.dot(a, wa_ref[...], preferred_element_type=F32)
    y = y + jnp.dot(p_ref[...], wp_ref[...], preferred_element_type=F32)
    o_ref[...] = jnp.where(prompt, xp_ref[...], xs_ref[...]) + y


def _mixer_out(attn_p, attn_s, pooled, x_p, x_s, w_out, layer):
    vmem = 2 * (3 * TM * ATTN_WIDTH * 2 + D_MODEL * TN_OUT * 2 + 3 * TM * TN_OUT * 4) + TM * ATTN_WIDTH * 2
    return pl.pallas_call(
        _mixer_out_kernel,
        grid=(ROW_TILES, D_MODEL // TN_OUT),
        in_specs=_x_specs(ATTN_WIDTH, lambda i, j: 0) + [
            pl.BlockSpec((TM, POOL_WIDTH), lambda i, j: (i, 0)),
        ] + _x_specs(TN_OUT, lambda i, j: j) + [
            pl.BlockSpec((None, ATTN_WIDTH, TN_OUT), lambda i, j: (layer, 0, j)),
            pl.BlockSpec((None, POOL_WIDTH, TN_OUT), lambda i, j: (layer, 1, j)),
        ],
        out_specs=pl.BlockSpec((TM, TN_OUT), lambda i, j: (i, j)),
        out_shape=jax.ShapeDtypeStruct((M_ALL, D_MODEL), F32),
        compiler_params=_params(vmem, 2),
        name="mixer_out",
    )(attn_p, attn_s, pooled, x_p, x_s, w_out, w_out)


def _mem_kv_kernel(m_ref, g_ref, wk_ref, wv_ref, kg_ref, k_ref, v_ref, h_ref):
    _norm_rows_to(m_ref, g_ref, h_ref, N_MEM)
    h = h_ref[...]
    k = jnp.dot(h, wk_ref[...], preferred_element_type=F32)
    for hd in range(MEM_HEADS):
        cols = slice(hd * MEM_HEAD_DIM, (hd + 1) * MEM_HEAD_DIM)
        k_ref[:, cols] = _rms_rows(k[:, cols], kg_ref[...])
    v_ref[...] = jnp.dot(h, wv_ref[...], preferred_element_type=F32)


def _mem_kv(mem, g, w_k, w_v, k_gain, layer):
    vmem = 2 * (N_MEM * D_MODEL * 4 + 2 * D_MODEL * MEM_WIDTH * 2 + 2 * N_MEM * MEM_WIDTH * 4) + N_MEM * D_MODEL * 2
    full = lambda shape: pl.BlockSpec(shape, lambda i: (0,) * len(shape))
    weight = pl.BlockSpec((None, D_MODEL, MEM_WIDTH), lambda i: (layer, 0, 0))
    return pl.pallas_call(
        _mem_kv_kernel,
        grid=(1,),
        in_specs=[full((N_MEM, D_MODEL)), full((1, D_MODEL)), weight, weight, full((1, MEM_HEAD_DIM))],
        out_specs=[full((N_MEM, MEM_WIDTH)), full((N_MEM, MEM_WIDTH))],
        out_shape=[jax.ShapeDtypeStruct((N_MEM, MEM_WIDTH), F32)] * 2,
        scratch_shapes=[pltpu.VMEM((N_MEM, D_MODEL), BF16)],
        compiler_params=_params(vmem, 1),
        name="mem_kv",
    )(mem, g, w_k, w_v, k_gain)


CROSS_TILES_PROMPT = M_PROMPT // TM_CROSS
BATCH_PER_CROSS_TILE = TM_CROSS // DEC_SEQ


def _cross_heads(q, mk_ref, mv_ref, b, qg, oc_ref, rows):
    for hd in range(MEM_HEADS):
        cols = slice(hd * MEM_HEAD_DIM, (hd + 1) * MEM_HEAD_DIM)
        qn = _rms_rows(q[:, cols], qg).astype(BF16)
        s = lax.dot_general(qn, mk_ref[b, :, cols].astype(BF16), (((1,), (1,)), ((), ())),
                            preferred_element_type=F32) * (MEM_HEAD_DIM ** -0.5)
        p = jnp.exp(s - jnp.max(s, axis=-1, keepdims=True))
        den = jnp.sum(p, axis=-1, keepdims=True)
        o = jnp.dot(p.astype(BF16), mv_ref[b, :, cols].astype(BF16), preferred_element_type=F32) / den
        oc_ref[rows, cols] = o.astype(BF16)


def _cross_kernel(x_ref, g_ref, wq_ref, qg_ref, mkp_ref, mvp_ref, mks_ref, mvs_ref, wo_ref,
                  o_ref, h_ref, q_ref, oc_ref):
    i = pl.program_id(0)
    _norm_rows_to(x_ref, g_ref, h_ref, TM_CROSS)
    q_ref[...] = jnp.dot(h_ref[...], wq_ref[...], preferred_element_type=F32)

    @pl.when(i < CROSS_TILES_PROMPT)
    def _():
        _cross_heads(q_ref[...], mkp_ref, mvp_ref, 0, qg_ref[...], oc_ref, slice(0, TM_CROSS))

    @pl.when(i >= CROSS_TILES_PROMPT)
    def _():
        for b in range(BATCH_PER_CROSS_TILE):
            rows = slice(b * DEC_SEQ, (b + 1) * DEC_SEQ)
            _cross_heads(q_ref[rows, :], mks_ref, mvs_ref, b, qg_ref[...], oc_ref, rows)

    o_ref[...] = x_ref[...] + jnp.dot(oc_ref[...], wo_ref[...], preferred_element_type=F32)


def _cross(x, g, w_q, q_gain, mk_p, mv_p, mk_s, mv_s, w_o, layer):
    const = lambda shape: pl.BlockSpec(shape, lambda i: (0,) * len(shape))
    once = lambda shape: pl.BlockSpec((None,) + shape, lambda i: (layer, 0, 0), pipeline_mode=pl.Buffered(1))
    sample_mem = pl.BlockSpec((BATCH_PER_CROSS_TILE, N_MEM, MEM_WIDTH),
                              lambda i: (jnp.maximum(i - CROSS_TILES_PROMPT, 0), 0, 0))
    vmem = (4 * TM_CROSS * D_MODEL * 4 + 2 * D_MODEL * MEM_WIDTH * 2
            + 4 * (1 + BATCH_PER_CROSS_TILE) * N_MEM * MEM_WIDTH * 4
            + TM_CROSS * D_MODEL * 2 + TM_CROSS * MEM_WIDTH * 6 + (6 << 20))
    return pl.pallas_call(
        _cross_kernel,
        grid=(M_ALL // TM_CROSS,),
        in_specs=[
            pl.BlockSpec((TM_CROSS, D_MODEL), lambda i: (i, 0)),
            const((1, D_MODEL)),
            once((D_MODEL, MEM_WIDTH)),
            const((1, MEM_HEAD_DIM)),
            const(mk_p.shape), const(mv_p.shape), sample_mem, sample_mem,
            once((MEM_WIDTH, D_MODEL)),
        ],
        out_specs=pl.BlockSpec((TM_CROSS, D_MODEL), lambda i: (i, 0)),
        out_shape=jax.ShapeDtypeStruct((M_ALL, D_MODEL), F32),
        scratch_shapes=[pltpu.VMEM((TM_CROSS, D_MODEL), BF16), pltpu.VMEM((TM_CROSS, MEM_WIDTH), F32),
                        pltpu.VMEM((TM_CROSS, MEM_WIDTH), BF16)],
        compiler_params=_params(vmem, 1),
        name="cross",
    )(x, g, w_q, q_gain, mk_p, mv_p, mk_s, mv_s, w_o)


GROUP = TF_SAMPLE
GROUPS_PER_TILE = TF_FFN // GROUP
GROUPS_LAST = TF_LAST // GROUP
N_GROUPS = D_FF // GROUP


def _swiglu_groups(gate_up, n_groups):
    acts = []
    for k in range(n_groups):
        gate = gate_up[:, 2 * k * GROUP:(2 * k + 1) * GROUP]
        up = gate_up[:, (2 * k + 1) * GROUP:(2 * k + 2) * GROUP]
        acts.append(gate * jax.nn.sigmoid(gate) * up)
    return (acts[0] if n_groups == 1 else jnp.concatenate(acts, axis=1)).astype(BF16)


def _ffn_prompt_kernel(x_ref, g_ref, wgu_ref, wd_ref, o_ref, h_ref):
    f = pl.program_id(1)

    @pl.when(f == 0)
    def _():
        _norm_rows_to(x_ref, g_ref, h_ref, TM)

    def accumulate(n_groups, first):
        gate_up = jnp.dot(h_ref[...], wgu_ref[:, :2 * n_groups * GROUP], preferred_element_type=F32)
        act = _swiglu_groups(gate_up, n_groups)
        base = x_ref if first else o_ref
        o_ref[...] = base[...] + jnp.dot(act, wd_ref[:n_groups * GROUP, :], preferred_element_type=F32)

    pl.when(f == 0)(functools.partial(accumulate, GROUPS_PER_TILE, True))
    pl.when((f > 0) & (f < NF_FFN - 1))(functools.partial(accumulate, GROUPS_PER_TILE, False))
    pl.when(f == NF_FFN - 1)(functools.partial(accumulate, GROUPS_LAST, False))


def _ffn_prompt(x, g, w_gate_up, w_down):
    vmem = 3 * TM * D_MODEL * 4 + TM * D_MODEL * 2 + 2 * 3 * D_MODEL * TF_FFN * 2 + 4 * TM * TF_FFN * 4
    return pl.pallas_call(
        _ffn_prompt_kernel,
        grid=(PROMPT_TILES, NF_FFN),
        in_specs=[
            pl.BlockSpec((TM, D_MODEL), lambda i, f: (i, 0)),
            pl.BlockSpec((1, D_MODEL), lambda i, f: (0, 0)),
            pl.BlockSpec((D_MODEL, 2 * TF_FFN), lambda i, f: (0, f)),
            pl.BlockSpec((TF_FFN, D_MODEL), lambda i, f: (f, 0)),
        ],
        out_specs=pl.BlockSpec((TM, D_MODEL), lambda i, f: (i, 0), pipeline_mode=pl.Buffered(1)),
        out_shape=jax.ShapeDtypeStruct((M_PROMPT, D_MODEL), F32),
        scratch_shapes=[pltpu.VMEM((TM, D_MODEL), BF16)],
        compiler_params=_params(vmem, 2),
        name="ffn_prompt",
    )(x, g, w_gate_up, w_down)


def _ffn_sample_kernel(x_ref, g_ref, wg_ref, wu_ref, wd_ref, o_ref, wgu_out, wd_out, h_ref):
    @pl.when(pl.program_id(0) == 0)
    def _():
        _norm_rows_to(x_ref, g_ref, h_ref, TM, copy_ref=o_ref)

    wgu_out[:, :GROUP] = wg_ref[...].astype(BF16)
    wgu_out[:, GROUP:] = wu_ref[...].astype(BF16)
    wd_out[...] = wd_ref[...].astype(BF16)
    act = _swiglu_groups(jnp.dot(h_ref[...], wgu_out[...], preferred_element_type=F32), 1)
    o_ref[...] += jnp.dot(act, wd_out[...], preferred_element_type=F32)


def _ffn_sample(x, g, w_gate, w_up, w_down, layer):
    vmem = 2 * TM * D_MODEL * 4 + TM * D_MODEL * 2 + 2 * 3 * D_MODEL * GROUP * 6 + 4 * TM * GROUP * 4
    once = pl.Buffered(1)
    col_f32 = pl.BlockSpec((None, D_MODEL, GROUP), lambda f: (layer, 0, f))
    return pl.pallas_call(
        _ffn_sample_kernel,
        grid=(N_GROUPS,),
        in_specs=[
            pl.BlockSpec((TM, D_MODEL), lambda f: (PROMPT_TILES, 0), pipeline_mode=once),
            pl.BlockSpec((1, D_MODEL), lambda f: (0, 0)),
            col_f32, col_f32,
            pl.BlockSpec((None, GROUP, D_MODEL), lambda f: (layer, f, 0)),
        ],
        out_specs=[
            pl.BlockSpec((TM, D_MODEL), lambda f: (0, 0), pipeline_mode=once),
            pl.BlockSpec((D_MODEL, 2 * GROUP), lambda f: (0, f)),
            pl.BlockSpec((GROUP, D_MODEL), lambda f: (f, 0)),
        ],
        out_shape=[jax.ShapeDtypeStruct((M_SAMPLE, D_MODEL), F32),
                   jax.ShapeDtypeStruct((D_MODEL, 2 * D_FF), BF16),
                   jax.ShapeDtypeStruct((D_FF, D_MODEL), BF16)],
        scratch_shapes=[pltpu.VMEM((TM, D_MODEL), BF16)],
        compiler_params=_params(vmem, 1),
        name="ffn_sample",
    )(x, g, w_gate, w_up, w_down)


def _row(v):
    return v.astype(F32).reshape(1, -1)


def kernel(x_prompt, x_sample, cache_attn_k, cache_attn_v, state_pool, cache_mem_k, cache_mem_v, mem_prompt,
           g_mix, w_in, q_norm, k_norm, attn_sinks, w_pool, pool_scale, w_out,
           g_cross, g_mem, w_q_mem, w_k_mem, w_v_mem, q_norm_mem, k_norm_mem, w_o_mem,
           g_ffn, w_gate, w_up, w_down):
    x_p, x_s = x_prompt.reshape(M_PROMPT, D_MODEL), x_sample.reshape(M_SAMPLE, D_MODEL)
    mem = mem_prompt.reshape(N_MEM, D_MODEL)

    w_in_b = _cast_w_in(w_in)
    w_out_b, w_q_b, w_k_b, w_v_b, w_o_b = (_cast_rows(w) for w in (w_out, w_q_mem, w_k_mem, w_v_mem, w_o_mem))
    w_pool_b = _cast_rows(w_pool.reshape(DEPTH, POOL_WIDTH, POOL_GROUP_WIDTH))
    bias = _swa_bias_tables()

    ak_p, av_p, pl_p, mk_p, mv_p, ak_s, av_s, pl_s = ([] for _ in range(8))
    for l in range(DEPTH):
        z = _mixer_in(x_p, x_s, _row(g_mix[l]), w_in_b, l)
        kn = _k_norm(z, _row(jnp.tile(k_norm[l], 2)))
        cache_k = cache_attn_k[l].astype(F32).reshape(DEC_BATCH, WINDOW, KV_WIDTH)
        cache_v = cache_attn_v[l].astype(F32).reshape(DEC_BATCH, WINDOW, KV_WIDTH)
        sink_e, sink_o = _swa_sink_rows(attn_sinks[l])
        q_gain2 = _row(jnp.tile(q_norm[l], 2)) * (HEAD_DIM ** -0.5)
        attn_p = _swa_prompt(z, kn, q_gain2, bias, sink_e, sink_o)
        attn_s = _swa_sample(z, kn, cache_k, cache_v, q_gain2, bias, sink_e, sink_o)

        state_halo = jnp.pad(state_pool[l].astype(F32), ((0, 0), (HALO - POOL_STATE, 0), (0, 0)))
        pooled = _pool(z, state_halo, w_pool_b, _row(pool_scale[l]), l)
        x = _mixer_out(attn_p, attn_s, pooled, x_p, x_s, w_out_b, l)

        v = z[:, Z_V0:Z_V0 + KV_WIDTH]
        kn_s = kn[M_PROMPT:].reshape(DEC_BATCH, DEC_SEQ, KV_WIDTH)
        v_s = v[M_PROMPT:].reshape(DEC_BATCH, DEC_SEQ, KV_WIDTH)
        ak_p.append(kn[M_PROMPT - WINDOW:M_PROMPT].reshape(1, WINDOW, N_KV_HEADS, HEAD_DIM))
        av_p.append(v[M_PROMPT - WINDOW:M_PROMPT].reshape(1, WINDOW, N_KV_HEADS, HEAD_DIM))
        pl_p.append(z[M_PROMPT - POOL_STATE:M_PROMPT, Z_U0:Z_U0 + POOL_WIDTH].reshape(1, POOL_STATE, POOL_WIDTH))
        ak_s.append(jnp.concatenate([cache_k, kn_s], axis=1)[:, -WINDOW:]
                    .reshape(DEC_BATCH, WINDOW, N_KV_HEADS, HEAD_DIM))
        av_s.append(jnp.concatenate([cache_v, v_s], axis=1)[:, -WINDOW:]
                    .reshape(DEC_BATCH, WINDOW, N_KV_HEADS, HEAD_DIM))
        u_s = z[M_PROMPT:, Z_U0:Z_U0 + POOL_WIDTH].reshape(DEC_BATCH, DEC_SEQ, POOL_WIDTH)
        pl_s.append(jnp.concatenate([state_pool[l].astype(F32), u_s], axis=1)[:, -POOL_STATE:])

        mk, mv = _mem_kv(mem, _row(g_mem[l]), w_k_b, w_v_b, _row(k_norm_mem[l]), l)
        mk_p.append(mk.reshape(1, N_MEM, MEM_HEADS, MEM_HEAD_DIM))
        mv_p.append(mv.reshape(1, N_MEM, MEM_HEADS, MEM_HEAD_DIM))
        x = _cross(x, _row(g_cross[l]), w_q_b, _row(q_norm_mem[l]),
                   mk.reshape(1, N_MEM, MEM_WIDTH), mv.reshape(1, N_MEM, MEM_WIDTH),
                   cache_mem_k[l].reshape(DEC_BATCH, N_MEM, MEM_WIDTH).astype(F32),
                   cache_mem_v[l].reshape(DEC_BATCH, N_MEM, MEM_WIDTH).astype(F32),
                   w_o_b, l)

        x_s, w_gate_up_b, w_down_b = _ffn_sample(x, _row(g_ffn[l]), w_gate, w_up, w_down, l)
        x_p = _ffn_prompt(x, _row(g_ffn[l]), w_gate_up_b, w_down_b)

    return (x_p.reshape(1, SEQ, D_MODEL), x_s.reshape(DEC_BATCH, DEC_SEQ, D_MODEL),
            jnp.stack(ak_p), jnp.stack(av_p), jnp.stack(pl_p), jnp.stack(mk_p), jnp.stack(mv_p),
            jnp.stack(ak_s), jnp.stack(av_s), jnp.stack(pl_s))
```

```python
import functools

import jax
import jax.numpy as jnp
from jax import lax
from jax.experimental import pallas as pl
from jax.experimental.pallas import tpu as pltpu

F32 = jnp.float32
BF16 = jnp.bfloat16

D_MODEL = 4096
SEQ = 8192
DEPTH = 2
DEC_BATCH = 8
DEC_SEQ = 64
CHUNK = 64
WINDOW = 128
ATTN_WIDTH = D_MODEL // 2
POOL_WIDTH = D_MODEL - ATTN_WIDTH
HEAD_DIM = 64
N_HEADS = ATTN_WIDTH // HEAD_DIM
N_KV_HEADS = N_HEADS // 8
KV_WIDTH = N_KV_HEADS * HEAD_DIM
POOL_WINDOWS = (2, 4, 8, 16)
POOL_GROUP_WIDTH = POOL_WIDTH // len(POOL_WINDOWS)
POOL_STATE = max(POOL_WINDOWS) - 1
IN_WIDTH = ATTN_WIDTH + 2 * KV_WIDTH + POOL_WIDTH
N_MEM = 256
MEM_HEADS = 4
MEM_HEAD_DIM = 128
MEM_WIDTH = MEM_HEADS * MEM_HEAD_DIM
D_FF = -(-8 * D_MODEL // (3 * 256)) * 256
EPS = 1e-6
LOG2_E = 1.4426950408889634
NEG_INF = -1e30

M_PROMPT = SEQ
M_SAMPLE = DEC_BATCH * DEC_SEQ
M_ALL = M_PROMPT + M_SAMPLE
N_CHUNKS_PROMPT = M_PROMPT // CHUNK
BAND = WINDOW + CHUNK
BAND_BLOCKS = BAND // CHUNK

Z_U0 = ATTN_WIDTH
Z_K0 = ATTN_WIDTH + POOL_WIDTH
Z_V0 = Z_K0 + KV_WIDTH

LANES = 128
SUBLANES = 8
BF16_TILE_ROWS = 16
HALO = BF16_TILE_ROWS
VMEM_LIMIT_CAP = 60000 * 1024
CAST_BLOCK_BYTES = 8 << 20

TM = 512
PROMPT_TILES = M_PROMPT // TM
ROW_TILES = M_ALL // TM
TN_IN = 768
TN_OUT = 1024
TF_FFN = 512
NF_FFN = -(-D_FF // TF_FFN)
TF_LAST = D_FF - (NF_FFN - 1) * TF_FFN
TF_SAMPLE = 256
TM_CROSS = 256
TM_KN = 512
TP_POOL = 512
NORM_ROWS = BF16_TILE_ROWS
NORM_UNROLL = 8
W_IN_COL_BLOCK = KV_WIDTH


def _params(vmem_bytes, n_axes):
    limit = min(int(vmem_bytes * 1.2) + (6 << 20), VMEM_LIMIT_CAP)
    return pltpu.CompilerParams(dimension_semantics=("arbitrary",) * n_axes, vmem_limit_bytes=limit)


def _rms_rows(x, g):
    return x * lax.rsqrt(jnp.mean(x * x, axis=-1, keepdims=True) + EPS) * g


def _norm_rows_to(x_ref, g_ref, h_ref, n_rows, copy_ref=None):
    def body(r, carry):
        rows = pl.ds(pl.multiple_of(r * NORM_ROWS, NORM_ROWS), NORM_ROWS)
        x = x_ref[rows, :]
        h_ref[rows, :] = _rms_rows(x, g_ref[...]).astype(BF16)
        if copy_ref is not None:
            copy_ref[rows, :] = x
        return carry

    lax.fori_loop(0, n_rows // NORM_ROWS, body, 0, unroll=NORM_UNROLL)


def _half_lane_rms(x, gain2):
    lo = lax.broadcasted_iota(jnp.int32, (1, LANES), 1) < HEAD_DIM
    x2 = x * x
    ss_lo = jnp.sum(jnp.where(lo, x2, 0.0), axis=-1, keepdims=True)
    ss_hi = jnp.sum(jnp.where(lo, 0.0, x2), axis=-1, keepdims=True)
    r = jnp.where(lo, lax.rsqrt(ss_lo / HEAD_DIM + EPS), lax.rsqrt(ss_hi / HEAD_DIM + EPS))
    return x * r * gain2


def _cast_kernel(w_ref, o_ref):
    o_ref[...] = w_ref[...].astype(BF16)


def _cast_rows(w, layers=None):
    depth, rows, cols = w.shape
    depth = depth if layers is None else layers
    tr = max(t for t in range(BF16_TILE_ROWS, rows + 1, BF16_TILE_ROWS)
             if rows % t == 0 and t * cols * 4 <= CAST_BLOCK_BYTES)
    spec = pl.BlockSpec((None, tr, cols), lambda l, i: (l, i, 0))
    return pl.pallas_call(
        _cast_kernel, grid=(depth, rows // tr), in_specs=[spec], out_specs=spec,
        out_shape=jax.ShapeDtypeStruct((depth, rows, cols), BF16),
        compiler_params=_params(2 * tr * cols * 6, 2), name="cast_rows",
    )(w)


def _w_in_source_block(j):
    q_blocks, u_blocks = ATTN_WIDTH // W_IN_COL_BLOCK, POOL_WIDTH // W_IN_COL_BLOCK
    kv_blocks = 2 * KV_WIDTH // W_IN_COL_BLOCK
    return jnp.where(j < q_blocks, j, jnp.where(j < q_blocks + u_blocks, j + kv_blocks, j - u_blocks))


def _cast_w_in(w, layers):
    return pl.pallas_call(
        _cast_kernel, grid=(layers, IN_WIDTH // W_IN_COL_BLOCK),
        in_specs=[pl.BlockSpec((None, D_MODEL, W_IN_COL_BLOCK), lambda l, j: (l, 0, _w_in_source_block(j)))],
        out_specs=pl.BlockSpec((None, D_MODEL, W_IN_COL_BLOCK), lambda l, j: (l, 0, j)),
        out_shape=jax.ShapeDtypeStruct((layers,) + w.shape[1:], BF16),
        compiler_params=_params(2 * D_MODEL * W_IN_COL_BLOCK * 6, 2), name="cast_w_in",
    )(w)


def _mixer_in_kernel(xp_ref, xs_ref, g_ref, w_ref, z_ref, h_ref):
    i, j = pl.program_id(0), pl.program_id(1)

    @pl.when((j == 0) & (i < PROMPT_TILES))
    def _():
        _norm_rows_to(xp_ref, g_ref, h_ref, TM)

    @pl.when((j == 0) & (i >= PROMPT_TILES))
    def _():
        _norm_rows_to(xs_ref, g_ref, h_ref, TM)

    z_ref[...] = jnp.dot(h_ref[...], w_ref[...], preferred_element_type=F32)


def _x_specs(width, col):
    return [
        pl.BlockSpec((TM, width), lambda i, j: (jnp.minimum(i, PROMPT_TILES - 1), col(i, j))),
        pl.BlockSpec((TM, width), lambda i, j: (0, jnp.where(i >= PROMPT_TILES, col(i, j), 0))),
    ]


def _mixer_in(x_p, x_s, g, w, layer):
    vmem = 4 * TM * D_MODEL * 4 + TM * D_MODEL * 2 + 2 * D_MODEL * TN_IN * 2 + 2 * TM * TN_IN * 4
    return pl.pallas_call(
        _mixer_in_kernel,
        grid=(ROW_TILES, IN_WIDTH // TN_IN),
        in_specs=_x_specs(D_MODEL, lambda i, j: 0) + [
            pl.BlockSpec((1, D_MODEL), lambda i, j: (0, 0)),
            pl.BlockSpec((None, D_MODEL, TN_IN), lambda i, j: (layer, 0, j)),
        ],
        out_specs=pl.BlockSpec((TM, TN_IN), lambda i, j: (i, j)),
        out_shape=jax.ShapeDtypeStruct((M_ALL, IN_WIDTH), F32),
        scratch_shapes=[pltpu.VMEM((TM, D_MODEL), BF16)],
        compiler_params=_params(vmem, 2),
        name="mixer_in",
    )(x_p, x_s, g, w)


def _k_norm_kernel(k_ref, g_ref, o_ref):
    for t in range(KV_WIDTH // LANES):
        cols = slice(t * LANES, (t + 1) * LANES)
        o_ref[:, cols] = _half_lane_rms(k_ref[:, cols], g_ref[...])


def _k_norm(z, gain2):
    vmem = 4 * TM_KN * KV_WIDTH * 4
    return pl.pallas_call(
        _k_norm_kernel,
        grid=(M_ALL // TM_KN,),
        in_specs=[
            pl.BlockSpec((TM_KN, KV_WIDTH), lambda i: (i, Z_K0 // KV_WIDTH)),
            pl.BlockSpec((1, LANES), lambda i: (0, 0)),
        ],
        out_specs=pl.BlockSpec((TM_KN, KV_WIDTH), lambda i: (i, 0)),
        out_shape=jax.ShapeDtypeStruct((M_ALL, KV_WIDTH), F32),
        compiler_params=_params(vmem, 1),
        name="k_norm",
    )(z, gain2)


PAIRS_PER_KV = N_HEADS // N_KV_HEADS // 2
QUERIES_PER_KV = PAIRS_PER_KV * CHUNK
N_PAIRS = N_HEADS // 2
SWA_CHUNKS = 4
Z_V_BLOCK = Z_V0 // KV_WIDTH


def _swa_chunk(q_ref, rows, k_refs, v_refs, qg_ref, bias_ref, table, sink_e_ref, sink_o_ref, o_ref):
    lo = lax.broadcasted_iota(jnp.int32, (1, LANES), 1) < HEAD_DIM

    q = jnp.concatenate([q_ref[rows, p * LANES:(p + 1) * LANES] for p in range(N_PAIRS)], axis=0)

    sel_row = lax.broadcasted_iota(jnp.int32, (SUBLANES, LANES), 0)
    sel_lo = lax.broadcasted_iota(jnp.int32, (SUBLANES, LANES), 1) < HEAD_DIM
    sel = jnp.where(((sel_row == 0) & sel_lo) | ((sel_row == 1) & ~sel_lo), 1.0, 0.0).astype(BF16)
    q2 = q * q
    q2_hi = q2.astype(BF16)
    q2_lo = (q2 - q2_hi.astype(F32)).astype(BF16)
    nt = (((1,), (1,)), ((), ()))
    ss = (lax.dot_general(sel, q2_hi, nt, preferred_element_type=F32)
          + lax.dot_general(sel, q2_lo, nt, preferred_element_type=F32))
    r = lax.rsqrt(ss / HEAD_DIM + EPS) * LOG2_E
    qg = (q * qg_ref[...]).astype(BF16)

    kb = jnp.concatenate([ref[...] for ref in k_refs], axis=0)
    vb = jnp.concatenate([ref[...] for ref in v_refs] + [jnp.zeros((KV_WIDTH - BAND, KV_WIDTH), F32)], axis=0)
    v_t = vb.T

    def head_softmax(s_raw, r_row, bias, sink):
        s = s_raw * r_row - bias
        m = jnp.maximum(jnp.max(s, axis=0, keepdims=True), sink)
        p = jnp.exp2(s - m)
        return p.astype(BF16), jnp.sum(p, axis=0, keepdims=True) + jnp.exp2(sink - m)

    for kv_tile in range(KV_WIDTH // LANES):
        kt = kb[:, kv_tile * LANES:(kv_tile + 1) * LANES]
        kt_r = pltpu.roll(kt, HEAD_DIM, axis=1)
        for par in range(2):
            hk = 2 * kv_tile + par
            k_e = jnp.where(lo, kt_r if par else kt, 0.0)
            k_o = jnp.where(lo, 0.0, kt if par else kt_r)
            kk = jnp.concatenate([k_e, k_o], axis=0).astype(BF16)
            cols = slice(hk * QUERIES_PER_KV, (hk + 1) * QUERIES_PER_KV)
            s_t = lax.dot_general(kk, qg[cols], nt, preferred_element_type=F32)
            bias = bias_ref[table, hk]
            p_e, d_e = head_softmax(s_t[:BAND], r[0:1, cols], bias[:BAND], sink_e_ref[hk])
            p_o, d_o = head_softmax(s_t[BAND:], r[1:2, cols], bias[BAND:], sink_o_ref[hk])
            v_h = v_t[hk * HEAD_DIM:(hk + 1) * HEAD_DIM, :BAND].astype(BF16)
            o_e = jnp.dot(v_h, p_e, preferred_element_type=F32) / d_e
            o_o = jnp.dot(v_h, p_o, preferred_element_type=F32) / d_o
            o = jnp.concatenate([o_e, o_o], axis=0).T
            for pr in range(PAIRS_PER_KV):
                pair = hk * PAIRS_PER_KV + pr
                o_ref[rows, pair * LANES:(pair + 1) * LANES] = o[pr * CHUNK:(pr + 1) * CHUNK].astype(BF16)


def _swa_kernel(q_ref, *refs, prompt):
    n_band = SWA_CHUNKS * BAND_BLOCKS
    k_refs, v_refs = refs[:n_band], refs[n_band:2 * n_band]
    qg_ref, bias_ref, sink_e_ref, sink_o_ref, o_ref = refs[2 * n_band:]
    for ci in range(SWA_CHUNKS):
        table = jnp.minimum(pl.program_id(0) * SWA_CHUNKS + ci, BAND_BLOCKS - 1) if prompt else 0
        band = slice(ci * BAND_BLOCKS, (ci + 1) * BAND_BLOCKS)
        _swa_chunk(q_ref, slice(ci * CHUNK, (ci + 1) * CHUNK), k_refs[band], v_refs[band],
                   qg_ref, bias_ref, table, sink_e_ref, sink_o_ref, o_ref)


def _swa_call(prompt, n_rows, q_block0, band_k, band_v, operands, q_gain2, bias, sink_e, sink_o):
    step_rows = SWA_CHUNKS * CHUNK
    const = lambda a: pl.BlockSpec(a.shape, lambda s: (0,) * a.ndim)
    vmem = (4 * step_rows * ATTN_WIDTH * 4 + 8 * SWA_CHUNKS * BAND * KV_WIDTH * 4 + 2 * bias.size * 4
            + SWA_CHUNKS * (12 << 20))
    return pl.pallas_call(
        functools.partial(_swa_kernel, prompt=prompt),
        grid=(n_rows // step_rows,),
        in_specs=[pl.BlockSpec((step_rows, ATTN_WIDTH), lambda s: (q_block0 + s, 0))] + band_k + band_v + [
            pl.BlockSpec((1, LANES), lambda s: (0, 0)), const(bias), const(sink_e), const(sink_o)],
        out_specs=pl.BlockSpec((step_rows, ATTN_WIDTH), lambda s: (s, 0)),
        out_shape=jax.ShapeDtypeStruct((n_rows, ATTN_WIDTH), BF16),
        compiler_params=_params(vmem, 1),
        name="swa_prompt" if prompt else "swa_sample",
    )(*operands, q_gain2, bias, sink_e, sink_o)


def _swa_prompt(z, kn, q_gain2, bias, sink_e, sink_o):
    def spec(ci, j, col):
        return pl.BlockSpec((CHUNK, KV_WIDTH),
                            lambda s: (jnp.maximum(s * SWA_CHUNKS + ci - (BAND_BLOCKS - 1) + j, 0), col))
    slots = [(ci, j) for ci in range(SWA_CHUNKS) for j in range(BAND_BLOCKS)]
    band_k = [spec(ci, j, 0) for ci, j in slots]
    band_v = [spec(ci, j, Z_V_BLOCK) for ci, j in slots]
    operands = [z] + [kn] * len(slots) + [z] * len(slots)
    return _swa_call(True, M_PROMPT, 0, band_k, band_v, operands, q_gain2, bias, sink_e, sink_o)


def _swa_sample(z, kn, cache_k, cache_v, q_gain2, bias, sink_e, sink_o):
    cache_blocks = WINDOW // CHUNK
    first_chunk = N_CHUNKS_PROMPT

    def specs(new_col):
        out = []
        for ci in range(SWA_CHUNKS):
            out += [pl.BlockSpec((None, CHUNK, KV_WIDTH), functools.partial(lambda s, ci, j: (s * SWA_CHUNKS + ci, j, 0),
                                                                            ci=ci, j=j))
                    for j in range(cache_blocks)]
            out.append(pl.BlockSpec((CHUNK, KV_WIDTH),
                                    functools.partial(lambda s, ci: (first_chunk + s * SWA_CHUNKS + ci, new_col), ci=ci)))
        return out

    operands = [z] + ([cache_k] * cache_blocks + [kn]) * SWA_CHUNKS + ([cache_v] * cache_blocks + [z]) * SWA_CHUNKS
    return _swa_call(False, M_SAMPLE, M_PROMPT // (SWA_CHUNKS * CHUNK), specs(0), specs(Z_V_BLOCK), operands,
                     q_gain2, bias[BAND_BLOCKS - 1:], sink_e, sink_o)


def _swa_bias_tables():
    slopes = 2.0 ** (-8.0 * jnp.arange(1, N_HEADS + 1, dtype=F32) / N_HEADS)
    key = jnp.arange(BAND)
    dist = jnp.abs(jnp.arange(CHUNK)[None, :] + WINDOW - key[:, None]).astype(F32)
    head = (jnp.arange(N_KV_HEADS)[:, None, None] * (2 * PAIRS_PER_KV)
            + jnp.arange(2)[None, :, None] + 2 * jnp.arange(PAIRS_PER_KV)[None, None, :])
    bias = slopes[head][:, :, None, :, None] * dist[None, None, :, None, :]
    first_valid = jnp.maximum(WINDOW - CHUNK * jnp.arange(BAND_BLOCKS), 0)
    valid = key[None, :] >= first_valid[:, None]
    tables = jnp.where(valid[:, None, None, :, None, None], bias[None] * LOG2_E, -NEG_INF)
    return tables.reshape(BAND_BLOCKS, N_KV_HEADS, 2 * BAND, QUERIES_PER_KV)


def _swa_sink_rows(sinks):
    per_pair = sinks.astype(F32).reshape(N_KV_HEADS, PAIRS_PER_KV, 2) * LOG2_E
    rows = jnp.repeat(per_pair, CHUNK, axis=1)
    return rows[:, None, :, 0], rows[:, None, :, 1]


POOL_TILES_PROMPT = M_PROMPT // TP_POOL
BATCH_PER_POOL_TILE = TP_POOL // DEC_SEQ


def _trailing_sum(ext, w):
    s, k = ext, 1
    while k < w:
        s = s + pltpu.roll(s, k, axis=0)
        k *= 2
    return s


def _pool_kernel(u_ref, halo_ref, state_ref, w_ref, scale_ref, o_ref, d_ref):
    i = pl.program_id(0)

    def window_diffs(ext, cur, cnt, w):
        return _trailing_sum(ext, w)[HALO:] / cnt - cur

    @pl.when(i < POOL_TILES_PROMPT)
    def _():
        pos = i * TP_POOL + lax.broadcasted_iota(jnp.int32, (TP_POOL, 1), 0)
        for g, w in enumerate(POOL_WINDOWS):
            cols = slice(g * POOL_GROUP_WIDTH, (g + 1) * POOL_GROUP_WIDTH)
            cur = u_ref[:, cols]
            halo = jnp.where(i > 0, halo_ref[:, cols], 0.0)
            ext = jnp.concatenate([halo, cur], axis=0)
            cnt = jnp.minimum(pos + 1, w).astype(F32)
            d_ref[:, cols] = window_diffs(ext, cur, cnt, w)

    @pl.when(i >= POOL_TILES_PROMPT)
    def _():
        for g, w in enumerate(POOL_WINDOWS):
            cols = slice(g * POOL_GROUP_WIDTH, (g + 1) * POOL_GROUP_WIDTH)
            for b in range(BATCH_PER_POOL_TILE):
                rows = slice(b * DEC_SEQ, (b + 1) * DEC_SEQ)
                cur = u_ref[rows, cols]
                ext = jnp.concatenate([state_ref[b, :, cols], cur], axis=0)
                d_ref[rows, cols] = window_diffs(ext, cur, float(w), w)

    for g in range(len(POOL_WINDOWS)):
        cols = slice(g * POOL_GROUP_WIDTH, (g + 1) * POOL_GROUP_WIDTH)
        y = jnp.dot(d_ref[:, cols].astype(BF16), w_ref[cols, :], preferred_element_type=F32)
        o_ref[:, cols] = (y * scale_ref[:, cols]).astype(BF16)


def _pool(z, state_halo, w_pool, scale, layer):
    halo_blocks_per_tile = TP_POOL // HALO
    vmem = (2 * TP_POOL * POOL_WIDTH * 4 + 2 * HALO * POOL_WIDTH * 4 + 2 * BATCH_PER_POOL_TILE * HALO * POOL_WIDTH * 4
            + 2 * POOL_WIDTH * POOL_GROUP_WIDTH * 2 + 2 * TP_POOL * POOL_WIDTH * 2 + TP_POOL * POOL_WIDTH * 4
            + (8 << 20))
    return pl.pallas_call(
        _pool_kernel,
        grid=(M_ALL // TP_POOL,),
        in_specs=[
            pl.BlockSpec((TP_POOL, POOL_WIDTH), lambda i: (i, Z_U0 // POOL_WIDTH)),
            pl.BlockSpec((HALO, POOL_WIDTH),
                         lambda i: (jnp.maximum(i * halo_blocks_per_tile - 1, 0), Z_U0 // POOL_WIDTH)),
            pl.BlockSpec((BATCH_PER_POOL_TILE, HALO, POOL_WIDTH),
                         lambda i: (jnp.maximum(i - POOL_TILES_PROMPT, 0), 0, 0)),
            pl.BlockSpec((None, POOL_WIDTH, POOL_GROUP_WIDTH), lambda i: (layer, 0, 0)),
            pl.BlockSpec((1, POOL_WIDTH), lambda i: (0, 0)),
        ],
        out_specs=pl.BlockSpec((TP_POOL, POOL_WIDTH), lambda i: (i, 0)),
        out_shape=jax.ShapeDtypeStruct((M_ALL, POOL_WIDTH), BF16),
        scratch_shapes=[pltpu.VMEM((TP_POOL, POOL_WIDTH), F32)],
        compiler_params=_params(vmem, 1),
        name="pool",
    )(z, z, state_halo, w_pool, scale)


def _mixer_out_kernel(ap_ref, as_ref, p_ref, xp_ref, xs_ref, wa_ref, wp_ref, o_ref):
    prompt = pl.program_id(0) < PROMPT_TILES
    a = jnp.where(prompt, ap_ref[...], as_ref[...])
    y = jnp.dot(a, wa_ref[...], preferred_element_type=F32)
    y = y + jnp.dot(p_ref[...], wp_ref[...], preferred_element_type=F32)
    o_ref[...] = jnp.where(prompt, xp_ref[...], xs_ref[...]) + y


def _mixer_out(attn_p, attn_s, pooled, x_p, x_s, w_out, layer):
    vmem = 2 * (3 * TM * ATTN_WIDTH * 2 + D_MODEL * TN_OUT * 2 + 3 * TM * TN_OUT * 4) + TM * ATTN_WIDTH * 2
    return pl.pallas_call(
        _mixer_out_kernel,
        grid=(ROW_TILES, D_MODEL // TN_OUT),
        in_specs=_x_specs(ATTN_WIDTH, lambda i, j: 0) + [
            pl.BlockSpec((TM, POOL_WIDTH), lambda i, j: (i, 0)),
        ] + _x_specs(TN_OUT, lambda i, j: j) + [
            pl.BlockSpec((None, ATTN_WIDTH, TN_OUT), lambda i, j: (layer, 0, j)),
            pl.BlockSpec((None, POOL_WIDTH, TN_OUT), lambda i, j: (layer, 1, j)),
        ],
        out_specs=pl.BlockSpec((TM, TN_OUT), lambda i, j: (i, j)),
        out_shape=jax.ShapeDtypeStruct((M_ALL, D_MODEL), F32),
        compiler_params=_params(vmem, 2),
        name="mixer_out",
    )(attn_p, attn_s, pooled, x_p, x_s, w_out, w_out)


def _mem_kv_kernel(m_ref, g_ref, wk_ref, wv_ref, kg_ref, k_ref, v_ref, h_ref):
    _norm_rows_to(m_ref, g_ref, h_ref, N_MEM)
    h = h_ref[...]
    k = jnp.dot(h, wk_ref[...], preferred_element_type=F32)
    for hd in range(MEM_HEADS):
        cols = slice(hd * MEM_HEAD_DIM, (hd + 1) * MEM_HEAD_DIM)
        k_ref[:, cols] = _rms_rows(k[:, cols], kg_ref[...])
    v_ref[...] = jnp.dot(h, wv_ref[...], preferred_element_type=F32)


def _mem_kv(mem, g, w_k, w_v, k_gain, layer):
    vmem = 2 * (N_MEM * D_MODEL * 4 + 2 * D_MODEL * MEM_WIDTH * 2 + 2 * N_MEM * MEM_WIDTH * 4) + N_MEM * D_MODEL * 2
    full = lambda shape: pl.BlockSpec(shape, lambda i: (0,) * len(shape))
    weight = pl.BlockSpec((None, D_MODEL, MEM_WIDTH), lambda i: (layer, 0, 0))
    return pl.pallas_call(
        _mem_kv_kernel,
        grid=(1,),
        in_specs=[full((N_MEM, D_MODEL)), full((1, D_MODEL)), weight, weight, full((1, MEM_HEAD_DIM))],
        out_specs=[full((N_MEM, MEM_WIDTH)), full((N_MEM, MEM_WIDTH))],
        out_shape=[jax.ShapeDtypeStruct((N_MEM, MEM_WIDTH), F32)] * 2,
        scratch_shapes=[pltpu.VMEM((N_MEM, D_MODEL), BF16)],
        compiler_params=_params(vmem, 1),
        name="mem_kv",
    )(mem, g, w_k, w_v, k_gain)


CROSS_TILES_PROMPT = M_PROMPT // TM_CROSS
BATCH_PER_CROSS_TILE = TM_CROSS // DEC_SEQ


def _cross_heads(q, mk_ref, mv_ref, b, qg, oc_ref, rows):
    for hd in range(MEM_HEADS):
        cols = slice(hd * MEM_HEAD_DIM, (hd + 1) * MEM_HEAD_DIM)
        qn = _rms_rows(q[:, cols], qg).astype(BF16)
        s = lax.dot_general(qn, mk_ref[b, :, cols].astype(BF16), (((1,), (1,)), ((), ())),
                            preferred_element_type=F32) * (MEM_HEAD_DIM ** -0.5)
        p = jnp.exp(s - jnp.max(s, axis=-1, keepdims=True))
        den = jnp.sum(p, axis=-1, keepdims=True)
        o = jnp.dot(p.astype(BF16), mv_ref[b, :, cols].astype(BF16), preferred_element_type=F32) / den
        oc_ref[rows, cols] = o.astype(BF16)


def _cross_kernel(x_ref, g_ref, wq_ref, qg_ref, mkp_ref, mvp_ref, mks_ref, mvs_ref, wo_ref,
                  o_ref, h_ref, q_ref, oc_ref):
    i = pl.program_id(0)
    _norm_rows_to(x_ref, g_ref, h_ref, TM_CROSS)
    q_ref[...] = jnp.dot(h_ref[...], wq_ref[...], preferred_element_type=F32)

    @pl.when(i < CROSS_TILES_PROMPT)
    def _():
        _cross_heads(q_ref[...], mkp_ref, mvp_ref, 0, qg_ref[...], oc_ref, slice(0, TM_CROSS))

    @pl.when(i >= CROSS_TILES_PROMPT)
    def _():
        for b in range(BATCH_PER_CROSS_TILE):
            rows = slice(b * DEC_SEQ, (b + 1) * DEC_SEQ)
            _cross_heads(q_ref[rows, :], mks_ref, mvs_ref, b, qg_ref[...], oc_ref, rows)

    o_ref[...] = x_ref[...] + jnp.dot(oc_ref[...], wo_ref[...], preferred_element_type=F32)


def _cross(x, g, w_q, q_gain, mk_p, mv_p, mk_s, mv_s, w_o, layer):
    const = lambda shape: pl.BlockSpec(shape, lambda i: (0,) * len(shape))
    once = lambda shape: pl.BlockSpec((None,) + shape, lambda i: (layer, 0, 0), pipeline_mode=pl.Buffered(1))
    sample_mem = pl.BlockSpec((BATCH_PER_CROSS_TILE, N_MEM, MEM_WIDTH),
                              lambda i: (jnp.maximum(i - CROSS_TILES_PROMPT, 0), 0, 0))
    vmem = (4 * TM_CROSS * D_MODEL * 4 + 2 * D_MODEL * MEM_WIDTH * 2
            + 4 * (1 + BATCH_PER_CROSS_TILE) * N_MEM * MEM_WIDTH * 4
            + TM_CROSS * D_MODEL * 2 + TM_CROSS * MEM_WIDTH * 6 + (6 << 20))
    return pl.pallas_call(
        _cross_kernel,
        grid=(M_ALL // TM_CROSS,),
        in_specs=[
            pl.BlockSpec((TM_CROSS, D_MODEL), lambda i: (i, 0)),
            const((1, D_MODEL)),
            once((D_MODEL, MEM_WIDTH)),
            const((1, MEM_HEAD_DIM)),
            const(mk_p.shape), const(mv_p.shape), sample_mem, sample_mem,
            once((MEM_WIDTH, D_MODEL)),
        ],
        out_specs=pl.BlockSpec((TM_CROSS, D_MODEL), lambda i: (i, 0)),
        out_shape=jax.ShapeDtypeStruct((M_ALL, D_MODEL), F32),
        scratch_shapes=[pltpu.VMEM((TM_CROSS, D_MODEL), BF16), pltpu.VMEM((TM_CROSS, MEM_WIDTH), F32),
                        pltpu.VMEM((TM_CROSS, MEM_WIDTH), BF16)],
        compiler_params=_params(vmem, 1),
        name="cross",
    )(x, g, w_q, q_gain, mk_p, mv_p, mk_s, mv_s, w_o)


GROUP = TF_SAMPLE
GROUPS_PER_TILE = TF_FFN // GROUP
GROUPS_LAST = TF_LAST // GROUP
N_GROUPS = D_FF // GROUP


def _swiglu_groups(gate_up, n_groups):
    acts = []
    for k in range(n_groups):
        gate = gate_up[:, 2 * k * GROUP:(2 * k + 1) * GROUP]
        up = gate_up[:, (2 * k + 1) * GROUP:(2 * k + 2) * GROUP]
        acts.append(gate * jax.nn.sigmoid(gate) * up)
    return (acts[0] if n_groups == 1 else jnp.concatenate(acts, axis=1)).astype(BF16)


SIDE_ROWS = BF16_TILE_ROWS
SIDE_STEPS = D_MODEL // SIDE_ROWS


def _ffn_prompt_kernel(x_ref, g_ref, wgu_ref, wd_ref, *rest, side_cast):
    f = pl.program_id(1)
    if side_cast:
        w_in_ref, w_out_ref, o_ref, w_in_b_ref, w_out_b_ref, h_ref = rest

        @pl.when(pl.program_id(0) * NF_FFN + f < SIDE_STEPS)
        def _():
            q_end, kv_end = ATTN_WIDTH, ATTN_WIDTH + 2 * KV_WIDTH
            w_in_b_ref[:, :Z_U0] = w_in_ref[:, :q_end].astype(BF16)
            w_in_b_ref[:, Z_U0:Z_K0] = w_in_ref[:, kv_end:].astype(BF16)
            w_in_b_ref[:, Z_K0:] = w_in_ref[:, q_end:kv_end].astype(BF16)
            w_out_b_ref[...] = w_out_ref[...].astype(BF16)
    else:
        o_ref, h_ref = rest

    @pl.when(f == 0)
    def _():
        _norm_rows_to(x_ref, g_ref, h_ref, TM)

    def accumulate(n_groups, first):
        gate_up = jnp.dot(h_ref[...], wgu_ref[:, :2 * n_groups * GROUP], preferred_element_type=F32)
        act = _swiglu_groups(gate_up, n_groups)
        base = x_ref if first else o_ref
        o_ref[...] = base[...] + jnp.dot(act, wd_ref[:n_groups * GROUP, :], preferred_element_type=F32)

    pl.when(f == 0)(functools.partial(accumulate, GROUPS_PER_TILE, True))
    pl.when((f > 0) & (f < NF_FFN - 1))(functools.partial(accumulate, GROUPS_PER_TILE, False))
    pl.when(f == NF_FFN - 1)(functools.partial(accumulate, GROUPS_LAST, False))


def _ffn_prompt(x, g, w_gate_up, w_down, next_layer=None, w_in=None, w_out=None):
    assert PROMPT_TILES * NF_FFN >= SIDE_STEPS
    side_cast = next_layer is not None
    vmem = 3 * TM * D_MODEL * 4 + TM * D_MODEL * 2 + 2 * 3 * D_MODEL * TF_FFN * 2 + 4 * TM * TF_FFN * 4
    in_specs = [
        pl.BlockSpec((TM, D_MODEL), lambda i, f: (i, 0)),
        pl.BlockSpec((1, D_MODEL), lambda i, f: (0, 0)),
        pl.BlockSpec((D_MODEL, 2 * TF_FFN), lambda i, f: (0, f)),
        pl.BlockSpec((TF_FFN, D_MODEL), lambda i, f: (f, 0)),
    ]
    out_specs = [pl.BlockSpec((TM, D_MODEL), lambda i, f: (i, 0), pipeline_mode=pl.Buffered(1))]
    out_shape = [jax.ShapeDtypeStruct((M_PROMPT, D_MODEL), F32)]
    operands = [x, g, w_gate_up, w_down]
    if side_cast:
        side_block = lambda i, f: jnp.minimum(i * NF_FFN + f, SIDE_STEPS - 1)
        for w in (w_in, w_out):
            cols = w.shape[-1]
            in_specs.append(pl.BlockSpec((None, SIDE_ROWS, cols), lambda i, f: (next_layer, side_block(i, f), 0)))
            out_specs.append(pl.BlockSpec((None, SIDE_ROWS, cols), lambda i, f: (0, side_block(i, f), 0)))
            out_shape.append(jax.ShapeDtypeStruct((1, D_MODEL, cols), BF16))
            operands.append(w)
            vmem += 2 * SIDE_ROWS * cols * 6
    return pl.pallas_call(
        functools.partial(_ffn_prompt_kernel, side_cast=side_cast),
        grid=(PROMPT_TILES, NF_FFN),
        in_specs=in_specs,
        out_specs=out_specs,
        out_shape=out_shape,
        scratch_shapes=[pltpu.VMEM((TM, D_MODEL), BF16)],
        compiler_params=_params(vmem, 2),
        name="ffn_prompt",
    )(*operands)


def _ffn_sample_kernel(x_ref, g_ref, wg_ref, wu_ref, wd_ref, o_ref, wgu_out, wd_out, h_ref):
    @pl.when(pl.program_id(0) == 0)
    def _():
        _norm_rows_to(x_ref, g_ref, h_ref, TM, copy_ref=o_ref)

    wgu_out[:, :GROUP] = wg_ref[...].astype(BF16)
    wgu_out[:, GROUP:] = wu_ref[...].astype(BF16)
    wd_out[...] = wd_ref[...].astype(BF16)
    act = _swiglu_groups(jnp.dot(h_ref[...], wgu_out[...], preferred_element_type=F32), 1)
    o_ref[...] += jnp.dot(act, wd_out[...], preferred_element_type=F32)


def _ffn_sample(x, g, w_gate, w_up, w_down, layer):
    vmem = 2 * TM * D_MODEL * 4 + TM * D_MODEL * 2 + 2 * 3 * D_MODEL * GROUP * 6 + 4 * TM * GROUP * 4
    once = pl.Buffered(1)
    col_f32 = pl.BlockSpec((None, D_MODEL, GROUP), lambda f: (layer, 0, f))
    return pl.pallas_call(
        _ffn_sample_kernel,
        grid=(N_GROUPS,),
        in_specs=[
            pl.BlockSpec((TM, D_MODEL), lambda f: (PROMPT_TILES, 0), pipeline_mode=once),
            pl.BlockSpec((1, D_MODEL), lambda f: (0, 0)),
            col_f32, col_f32,
            pl.BlockSpec((None, GROUP, D_MODEL), lambda f: (layer, f, 0)),
        ],
        out_specs=[
            pl.BlockSpec((TM, D_MODEL), lambda f: (0, 0), pipeline_mode=once),
            pl.BlockSpec((D_MODEL, 2 * GROUP), lambda f: (0, f)),
            pl.BlockSpec((GROUP, D_MODEL), lambda f: (f, 0)),
        ],
        out_shape=[jax.ShapeDtypeStruct((M_SAMPLE, D_MODEL), F32),
                   jax.ShapeDtypeStruct((D_MODEL, 2 * D_FF), BF16),
                   jax.ShapeDtypeStruct((D_FF, D_MODEL), BF16)],
        scratch_shapes=[pltpu.VMEM((TM, D_MODEL), BF16)],
        compiler_params=_params(vmem, 1),
        name="ffn_sample",
    )(x, g, w_gate, w_up, w_down)


def _row(v):
    return v.astype(F32).reshape(1, -1)


def kernel(x_prompt, x_sample, cache_attn_k, cache_attn_v, state_pool, cache_mem_k, cache_mem_v, mem_prompt,
           g_mix, w_in, q_norm, k_norm, attn_sinks, w_pool, pool_scale, w_out,
           g_cross, g_mem, w_q_mem, w_k_mem, w_v_mem, q_norm_mem, k_norm_mem, w_o_mem,
           g_ffn, w_gate, w_up, w_down):
    x_p, x_s = x_prompt.reshape(M_PROMPT, D_MODEL), x_sample.reshape(M_SAMPLE, D_MODEL)
    mem = mem_prompt.reshape(N_MEM, D_MODEL)

    w_in_b, w_out_b = _cast_w_in(w_in, 1), _cast_rows(w_out, 1)
    w_q_b, w_k_b, w_v_b, w_o_b = (_cast_rows(w) for w in (w_q_mem, w_k_mem, w_v_mem, w_o_mem))
    w_pool_b = _cast_rows(w_pool.reshape(DEPTH, POOL_WIDTH, POOL_GROUP_WIDTH))
    bias = _swa_bias_tables()

    ak_p, av_p, pl_p, mk_p, mv_p, ak_s, av_s, pl_s = ([] for _ in range(8))
    for l in range(DEPTH):
        z = _mixer_in(x_p, x_s, _row(g_mix[l]), w_in_b, 0)
        kn = _k_norm(z, _row(jnp.tile(k_norm[l], 2)))
        cache_k = cache_attn_k[l].astype(F32).reshape(DEC_BATCH, WINDOW, KV_WIDTH)
        cache_v = cache_attn_v[l].astype(F32).reshape(DEC_BATCH, WINDOW, KV_WIDTH)
        sink_e, sink_o = _swa_sink_rows(attn_sinks[l])
        q_gain2 = _row(jnp.tile(q_norm[l], 2)) * (HEAD_DIM ** -0.5)
        attn_p = _swa_prompt(z, kn, q_gain2, bias, sink_e, sink_o)
        attn_s = _swa_sample(z, kn, cache_k, cache_v, q_gain2, bias, sink_e, sink_o)

        state_halo = jnp.pad(state_pool[l].astype(F32), ((0, 0), (HALO - POOL_STATE, 0), (0, 0)))
        pooled = _pool(z, state_halo, w_pool_b, _row(pool_scale[l]), l)
        x = _mixer_out(attn_p, attn_s, pooled, x_p, x_s, w_out_b, 0)

        v = z[:, Z_V0:Z_V0 + KV_WIDTH]
        kn_s = kn[M_PROMPT:].reshape(DEC_BATCH, DEC_SEQ, KV_WIDTH)
        v_s = v[M_PROMPT:].reshape(DEC_BATCH, DEC_SEQ, KV_WIDTH)
        ak_p.append(kn[M_PROMPT - WINDOW:M_PROMPT].reshape(1, WINDOW, N_KV_HEADS, HEAD_DIM))
        av_p.append(v[M_PROMPT - WINDOW:M_PROMPT].reshape(1, WINDOW, N_KV_HEADS, HEAD_DIM))
        pl_p.append(z[M_PROMPT - POOL_STATE:M_PROMPT, Z_U0:Z_U0 + POOL_WIDTH].reshape(1, POOL_STATE, POOL_WIDTH))
        ak_s.append(jnp.concatenate([cache_k, kn_s], axis=1)[:, -WINDOW:]
                    .reshape(DEC_BATCH, WINDOW, N_KV_HEADS, HEAD_DIM))
        av_s.append(jnp.concatenate([cache_v, v_s], axis=1)[:, -WINDOW:]
                    .reshape(DEC_BATCH, WINDOW, N_KV_HEADS, HEAD_DIM))
        u_s = z[M_PROMPT:, Z_U0:Z_U0 + POOL_WIDTH].reshape(DEC_BATCH, DEC_SEQ, POOL_WIDTH)
        pl_s.append(jnp.concatenate([state_pool[l].astype(F32), u_s], axis=1)[:, -POOL_STATE:])

        mk, mv = _mem_kv(mem, _row(g_mem[l]), w_k_b, w_v_b, _row(k_norm_mem[l]), l)
        mk_p.append(mk.reshape(1, N_MEM, MEM_HEADS, MEM_HEAD_DIM))
        mv_p.append(mv.reshape(1, N_MEM, MEM_HEADS, MEM_HEAD_DIM))
        x = _cross(x, _row(g_cross[l]), w_q_b, _row(q_norm_mem[l]),
                   mk.reshape(1, N_MEM, MEM_WIDTH), mv.reshape(1, N_MEM, MEM_WIDTH),
                   cache_mem_k[l].reshape(DEC_BATCH, N_MEM, MEM_WIDTH).astype(F32),
                   cache_mem_v[l].reshape(DEC_BATCH, N_MEM, MEM_WIDTH).astype(F32),
                   w_o_b, l)

        x_s, w_gate_up_b, w_down_b = _ffn_sample(x, _row(g_ffn[l]), w_gate, w_up, w_down, l)
        if l + 1 < DEPTH:
            x_p, w_in_b, w_out_b = _ffn_prompt(x, _row(g_ffn[l]), w_gate_up_b, w_down_b, l + 1, w_in, w_out)
        else:
            x_p, = _ffn_prompt(x, _row(g_ffn[l]), w_gate_up_b, w_down_b)

    return (x_p.reshape(1, SEQ, D_MODEL), x_s.reshape(DEC_BATCH, DEC_SEQ, D_MODEL),
            jnp.stack(ak_p), jnp.stack(av_p), jnp.stack(pl_p), jnp.stack(mk_p), jnp.stack(mv_p),
            jnp.stack(ak_s), jnp.stack(av_s), jnp.stack(pl_s))
```

```python
import functools

import jax
import jax.numpy as jnp
from jax import lax
from jax.experimental import pallas as pl
from jax.experimental.pallas import tpu as pltpu

F32 = jnp.float32
BF16 = jnp.bfloat16

D_MODEL = 4096
SEQ = 8192
DEPTH = 2
DEC_BATCH = 8
DEC_SEQ = 64
CHUNK = 64
WINDOW = 128
ATTN_WIDTH = D_MODEL // 2
POOL_WIDTH = D_MODEL - ATTN_WIDTH
HEAD_DIM = 64
N_HEADS = ATTN_WIDTH // HEAD_DIM
N_KV_HEADS = N_HEADS // 8
KV_WIDTH = N_KV_HEADS * HEAD_DIM
POOL_WINDOWS = (2, 4, 8, 16)
POOL_GROUP_WIDTH = POOL_WIDTH // len(POOL_WINDOWS)
POOL_STATE = max(POOL_WINDOWS) - 1
IN_WIDTH = ATTN_WIDTH + 2 * KV_WIDTH + POOL_WIDTH
N_MEM = 256
MEM_HEADS = 4
MEM_HEAD_DIM = 128
MEM_WIDTH = MEM_HEADS * MEM_HEAD_DIM
D_FF = -(-8 * D_MODEL // (3 * 256)) * 256
EPS = 1e-6
LOG2_E = 1.4426950408889634
NEG_INF = -1e30

M_PROMPT = SEQ
M_SAMPLE = DEC_BATCH * DEC_SEQ
M_ALL = M_PROMPT + M_SAMPLE
N_CHUNKS_PROMPT = M_PROMPT // CHUNK
BAND = WINDOW + CHUNK
BAND_BLOCKS = BAND // CHUNK

Z_U0 = ATTN_WIDTH
Z_K0 = ATTN_WIDTH + POOL_WIDTH
Z_V0 = Z_K0 + KV_WIDTH

LANES = 128
SUBLANES = 8
BF16_TILE_ROWS = 16
HALO = BF16_TILE_ROWS
VMEM_LIMIT_CAP = 60000 * 1024
CAST_BLOCK_BYTES = 8 << 20

TM = 512
PROMPT_TILES = M_PROMPT // TM
ROW_TILES = M_ALL // TM
TN_IN = 768
TN_OUT = 1024
TF_FFN = 512
NF_FFN = -(-D_FF // TF_FFN)
TF_LAST = D_FF - (NF_FFN - 1) * TF_FFN
TF_SAMPLE = 256
TM_CROSS = 256
TM_KN = 512
TP_POOL = 512
NORM_ROWS = BF16_TILE_ROWS
NORM_UNROLL = 8
W_IN_COL_BLOCK = KV_WIDTH


def _params(vmem_bytes, n_axes):
    limit = min(int(vmem_bytes * 1.2) + (6 << 20), VMEM_LIMIT_CAP)
    return pltpu.CompilerParams(dimension_semantics=("arbitrary",) * n_axes, vmem_limit_bytes=limit)


def _rms_rows(x, g):
    return x * lax.rsqrt(jnp.mean(x * x, axis=-1, keepdims=True) + EPS) * g


def _norm_rows_to(x_ref, g_ref, h_ref, n_rows, copy_ref=None):
    def body(r, carry):
        rows = pl.ds(pl.multiple_of(r * NORM_ROWS, NORM_ROWS), NORM_ROWS)
        x = x_ref[rows, :]
        h_ref[rows, :] = _rms_rows(x, g_ref[...]).astype(BF16)
        if copy_ref is not None:
            copy_ref[rows, :] = x
        return carry

    lax.fori_loop(0, n_rows // NORM_ROWS, body, 0, unroll=NORM_UNROLL)


def _half_lane_rms(x, gain2):
    lo = lax.broadcasted_iota(jnp.int32, (1, LANES), 1) < HEAD_DIM
    x2 = x * x
    ss_lo = jnp.sum(jnp.where(lo, x2, 0.0), axis=-1, keepdims=True)
    ss_hi = jnp.sum(jnp.where(lo, 0.0, x2), axis=-1, keepdims=True)
    r = jnp.where(lo, lax.rsqrt(ss_lo / HEAD_DIM + EPS), lax.rsqrt(ss_hi / HEAD_DIM + EPS))
    return x * r * gain2


def _cast_kernel(w_ref, o_ref):
    o_ref[...] = w_ref[...].astype(BF16)


def _cast_rows(w):
    depth, rows, cols = w.shape
    tr = max(t for t in range(BF16_TILE_ROWS, rows + 1, BF16_TILE_ROWS)
             if rows % t == 0 and t * cols * 4 <= CAST_BLOCK_BYTES)
    spec = pl.BlockSpec((None, tr, cols), lambda l, i: (l, i, 0))
    return pl.pallas_call(
        _cast_kernel, grid=(depth, rows // tr), in_specs=[spec], out_specs=spec,
        out_shape=jax.ShapeDtypeStruct(w.shape, BF16),
        compiler_params=_params(2 * tr * cols * 6, 2), name="cast_rows",
    )(w)


def _w_in_source_block(j):
    q_blocks, u_blocks = ATTN_WIDTH // W_IN_COL_BLOCK, POOL_WIDTH // W_IN_COL_BLOCK
    kv_blocks = 2 * KV_WIDTH // W_IN_COL_BLOCK
    return jnp.where(j < q_blocks, j, jnp.where(j < q_blocks + u_blocks, j + kv_blocks, j - u_blocks))


def _cast_w_in(w):
    depth = w.shape[0]
    return pl.pallas_call(
        _cast_kernel, grid=(depth, IN_WIDTH // W_IN_COL_BLOCK),
        in_specs=[pl.BlockSpec((None, D_MODEL, W_IN_COL_BLOCK), lambda l, j: (l, 0, _w_in_source_block(j)))],
        out_specs=pl.BlockSpec((None, D_MODEL, W_IN_COL_BLOCK), lambda l, j: (l, 0, j)),
        out_shape=jax.ShapeDtypeStruct(w.shape, BF16),
        compiler_params=_params(2 * D_MODEL * W_IN_COL_BLOCK * 6, 2), name="cast_w_in",
    )(w)


def _mixer_in_kernel(xp_ref, xs_ref, g_ref, w_ref, z_ref, h_ref):
    i, j = pl.program_id(0), pl.program_id(1)

    @pl.when((j == 0) & (i < PROMPT_TILES))
    def _():
        _norm_rows_to(xp_ref, g_ref, h_ref, TM)

    @pl.when((j == 0) & (i >= PROMPT_TILES))
    def _():
        _norm_rows_to(xs_ref, g_ref, h_ref, TM)

    z_ref[...] = jnp.dot(h_ref[...], w_ref[...], preferred_element_type=F32)


def _x_specs(width, col):
    return [
        pl.BlockSpec((TM, width), lambda i, j: (jnp.minimum(i, PROMPT_TILES - 1), col(i, j))),
        pl.BlockSpec((TM, width), lambda i, j: (0, jnp.where(i >= PROMPT_TILES, col(i, j), 0))),
    ]


def _mixer_in(x_p, x_s, g, w, layer):
    vmem = 4 * TM * D_MODEL * 4 + TM * D_MODEL * 2 + 2 * D_MODEL * TN_IN * 2 + 2 * TM * TN_IN * 4
    return pl.pallas_call(
        _mixer_in_kernel,
        grid=(ROW_TILES, IN_WIDTH // TN_IN),
        in_specs=_x_specs(D_MODEL, lambda i, j: 0) + [
            pl.BlockSpec((1, D_MODEL), lambda i, j: (0, 0)),
            pl.BlockSpec((None, D_MODEL, TN_IN), lambda i, j: (layer, 0, j)),
        ],
        out_specs=pl.BlockSpec((TM, TN_IN), lambda i, j: (i, j)),
        out_shape=jax.ShapeDtypeStruct((M_ALL, IN_WIDTH), F32),
        scratch_shapes=[pltpu.VMEM((TM, D_MODEL), BF16)],
        compiler_params=_params(vmem, 2),
        name="mixer_in",
    )(x_p, x_s, g, w)


def _k_norm_kernel(k_ref, g_ref, o_ref):
    for t in range(KV_WIDTH // LANES):
        cols = slice(t * LANES, (t + 1) * LANES)
        o_ref[:, cols] = _half_lane_rms(k_ref[:, cols], g_ref[...])


def _k_norm(z, gain2):
    vmem = 4 * TM_KN * KV_WIDTH * 4
    return pl.pallas_call(
        _k_norm_kernel,
        grid=(M_ALL // TM_KN,),
        in_specs=[
            pl.BlockSpec((TM_KN, KV_WIDTH), lambda i: (i, Z_K0 // KV_WIDTH)),
            pl.BlockSpec((1, LANES), lambda i: (0, 0)),
        ],
        out_specs=pl.BlockSpec((TM_KN, KV_WIDTH), lambda i: (i, 0)),
        out_shape=jax.ShapeDtypeStruct((M_ALL, KV_WIDTH), F32),
        compiler_params=_params(vmem, 1),
        name="k_norm",
    )(z, gain2)


PAIRS_PER_KV = N_HEADS // N_KV_HEADS // 2
QUERIES_PER_KV = PAIRS_PER_KV * CHUNK
N_PAIRS = N_HEADS // 2
SWA_CHUNKS = 8
Z_V_BLOCK = Z_V0 // KV_WIDTH


def _swa_chunk(q_ref, rows, k_refs, v_refs, qg_ref, bias_ref, table, sink_e_ref, sink_o_ref, o_ref):
    lo = lax.broadcasted_iota(jnp.int32, (1, LANES), 1) < HEAD_DIM

    q = jnp.concatenate([q_ref[rows, p * LANES:(p + 1) * LANES] for p in range(N_PAIRS)], axis=0)

    sel_row = lax.broadcasted_iota(jnp.int32, (SUBLANES, LANES), 0)
    sel_lo = lax.broadcasted_iota(jnp.int32, (SUBLANES, LANES), 1) < HEAD_DIM
    sel = jnp.where(((sel_row == 0) & sel_lo) | ((sel_row == 1) & ~sel_lo), 1.0, 0.0).astype(BF16)
    q2 = q * q
    q2_hi = q2.astype(BF16)
    q2_lo = (q2 - q2_hi.astype(F32)).astype(BF16)
    nt = (((1,), (1,)), ((), ()))
    ss = (lax.dot_general(sel, q2_hi, nt, preferred_element_type=F32)
          + lax.dot_general(sel, q2_lo, nt, preferred_element_type=F32))
    r = lax.rsqrt(ss / HEAD_DIM + EPS) * LOG2_E
    qg = (q * qg_ref[...]).astype(BF16)

    kb = jnp.concatenate([ref[...] for ref in k_refs], axis=0)
    vb = jnp.concatenate([ref[...] for ref in v_refs] + [jnp.zeros((KV_WIDTH - BAND, KV_WIDTH), F32)], axis=0)
    v_t = vb.T

    def head_softmax(s_raw, r_row, bias, sink):
        s = s_raw * r_row - bias
        m = jnp.maximum(jnp.max(s, axis=0, keepdims=True), sink)
        p = jnp.exp2(s - m)
        return p.astype(BF16), jnp.sum(p, axis=0, keepdims=True) + jnp.exp2(sink - m)

    for kv_tile in range(KV_WIDTH // LANES):
        kt = kb[:, kv_tile * LANES:(kv_tile + 1) * LANES]
        kt_r = pltpu.roll(kt, HEAD_DIM, axis=1)
        for par in range(2):
            hk = 2 * kv_tile + par
            k_e = jnp.where(lo, kt_r if par else kt, 0.0)
            k_o = jnp.where(lo, 0.0, kt if par else kt_r)
            kk = jnp.concatenate([k_e, k_o], axis=0).astype(BF16)
            cols = slice(hk * QUERIES_PER_KV, (hk + 1) * QUERIES_PER_KV)
            s_t = lax.dot_general(kk, qg[cols], nt, preferred_element_type=F32)
            bias = bias_ref[table, hk]
            p_e, d_e = head_softmax(s_t[:BAND], r[0:1, cols], bias[:BAND], sink_e_ref[hk])
            p_o, d_o = head_softmax(s_t[BAND:], r[1:2, cols], bias[BAND:], sink_o_ref[hk])
            v_h = v_t[hk * HEAD_DIM:(hk + 1) * HEAD_DIM, :BAND].astype(BF16)
            o_e = jnp.dot(v_h, p_e, preferred_element_type=F32) / d_e
            o_o = jnp.dot(v_h, p_o, preferred_element_type=F32) / d_o
            o = jnp.concatenate([o_e, o_o], axis=0).T
            for pr in range(PAIRS_PER_KV):
                pair = hk * PAIRS_PER_KV + pr
                o_ref[rows, pair * LANES:(pair + 1) * LANES] = o[pr * CHUNK:(pr + 1) * CHUNK].astype(BF16)


def _swa_kernel(q_ref, *refs, prompt):
    n_band = SWA_CHUNKS * BAND_BLOCKS
    k_refs, v_refs = refs[:n_band], refs[n_band:2 * n_band]
    qg_ref, bias_ref, sink_e_ref, sink_o_ref, o_ref = refs[2 * n_band:]
    for ci in range(SWA_CHUNKS):
        table = jnp.minimum(pl.program_id(0) * SWA_CHUNKS + ci, BAND_BLOCKS - 1) if prompt else 0
        band = slice(ci * BAND_BLOCKS, (ci + 1) * BAND_BLOCKS)
        _swa_chunk(q_ref, slice(ci * CHUNK, (ci + 1) * CHUNK), k_refs[band], v_refs[band],
                   qg_ref, bias_ref, table, sink_e_ref, sink_o_ref, o_ref)


def _swa_call(prompt, n_rows, q_block0, band_k, band_v, operands, q_gain2, bias, sink_e, sink_o):
    step_rows = SWA_CHUNKS * CHUNK
    const = lambda a: pl.BlockSpec(a.shape, lambda s: (0,) * a.ndim)
    vmem = (4 * step_rows * ATTN_WIDTH * 4 + 8 * SWA_CHUNKS * BAND * KV_WIDTH * 4 + 2 * bias.size * 4
            + SWA_CHUNKS * (12 << 20))
    return pl.pallas_call(
        functools.partial(_swa_kernel, prompt=prompt),
        grid=(n_rows // step_rows,),
        in_specs=[pl.BlockSpec((step_rows, ATTN_WIDTH), lambda s: (q_block0 + s, 0))] + band_k + band_v + [
            pl.BlockSpec((1, LANES), lambda s: (0, 0)), const(bias), const(sink_e), const(sink_o)],
        out_specs=pl.BlockSpec((step_rows, ATTN_WIDTH), lambda s: (s, 0)),
        out_shape=jax.ShapeDtypeStruct((n_rows, ATTN_WIDTH), BF16),
        compiler_params=_params(vmem, 1),
        name="swa_prompt" if prompt else "swa_sample",
    )(*operands, q_gain2, bias, sink_e, sink_o)


def _swa_prompt(z, kn, q_gain2, bias, sink_e, sink_o):
    def spec(ci, j, col):
        return pl.BlockSpec((CHUNK, KV_WIDTH),
                            lambda s: (jnp.maximum(s * SWA_CHUNKS + ci - (BAND_BLOCKS - 1) + j, 0), col))
    slots = [(ci, j) for ci in range(SWA_CHUNKS) for j in range(BAND_BLOCKS)]
    band_k = [spec(ci, j, 0) for ci, j in slots]
    band_v = [spec(ci, j, Z_V_BLOCK) for ci, j in slots]
    operands = [z] + [kn] * len(slots) + [z] * len(slots)
    return _swa_call(True, M_PROMPT, 0, band_k, band_v, operands, q_gain2, bias, sink_e, sink_o)


def _swa_sample(z, kn, cache_k, cache_v, q_gain2, bias, sink_e, sink_o):
    cache_blocks = WINDOW // CHUNK
    first_chunk = N_CHUNKS_PROMPT

    def specs(new_col):
        out = []
        for ci in range(SWA_CHUNKS):
            out += [pl.BlockSpec((None, CHUNK, KV_WIDTH), functools.partial(lambda s, ci, j: (s * SWA_CHUNKS + ci, j, 0),
                                                                            ci=ci, j=j))
                    for j in range(cache_blocks)]
            out.append(pl.BlockSpec((CHUNK, KV_WIDTH),
                                    functools.partial(lambda s, ci: (first_chunk + s * SWA_CHUNKS + ci, new_col), ci=ci)))
        return out

    operands = [z] + ([cache_k] * cache_blocks + [kn]) * SWA_CHUNKS + ([cache_v] * cache_blocks + [z]) * SWA_CHUNKS
    return _swa_call(False, M_SAMPLE, M_PROMPT // (SWA_CHUNKS * CHUNK), specs(0), specs(Z_V_BLOCK), operands,
                     q_gain2, bias[BAND_BLOCKS - 1:], sink_e, sink_o)


def _swa_bias_tables():
    slopes = 2.0 ** (-8.0 * jnp.arange(1, N_HEADS + 1, dtype=F32) / N_HEADS)
    key = jnp.arange(BAND)
    dist = jnp.abs(jnp.arange(CHUNK)[None, :] + WINDOW - key[:, None]).astype(F32)
    head = (jnp.arange(N_KV_HEADS)[:, None, None] * (2 * PAIRS_PER_KV)
            + jnp.arange(2)[None, :, None] + 2 * jnp.arange(PAIRS_PER_KV)[None, None, :])
    bias = slopes[head][:, :, None, :, None] * dist[None, None, :, None, :]
    first_valid = jnp.maximum(WINDOW - CHUNK * jnp.arange(BAND_BLOCKS), 0)
    valid = key[None, :] >= first_valid[:, None]
    tables = jnp.where(valid[:, None, None, :, None, None], bias[None] * LOG2_E, -NEG_INF)
    return tables.reshape(BAND_BLOCKS, N_KV_HEADS, 2 * BAND, QUERIES_PER_KV)


def _swa_sink_rows(sinks):
    per_pair = sinks.astype(F32).reshape(N_KV_HEADS, PAIRS_PER_KV, 2) * LOG2_E
    rows = jnp.repeat(per_pair, CHUNK, axis=1)
    return rows[:, None, :, 0], rows[:, None, :, 1]


POOL_TILES_PROMPT = M_PROMPT // TP_POOL
BATCH_PER_POOL_TILE = TP_POOL // DEC_SEQ


def _trailing_sum(ext, w):
    s, k = ext, 1
    while k < w:
        s = s + pltpu.roll(s, k, axis=0)
        k *= 2
    return s


def _pool_kernel(u_ref, halo_ref, state_ref, w_ref, scale_ref, o_ref, d_ref):
    i = pl.program_id(0)

    def window_diffs(ext, cur, cnt, w):
        return _trailing_sum(ext, w)[HALO:] / cnt - cur

    @pl.when(i < POOL_TILES_PROMPT)
    def _():
        pos = i * TP_POOL + lax.broadcasted_iota(jnp.int32, (TP_POOL, 1), 0)
        for g, w in enumerate(POOL_WINDOWS):
            cols = slice(g * POOL_GROUP_WIDTH, (g + 1) * POOL_GROUP_WIDTH)
            cur = u_ref[:, cols]
            halo = jnp.where(i > 0, halo_ref[:, cols], 0.0)
            ext = jnp.concatenate([halo, cur], axis=0)
            cnt = jnp.minimum(pos + 1, w).astype(F32)
            d_ref[:, cols] = window_diffs(ext, cur, cnt, w)

    @pl.when(i >= POOL_TILES_PROMPT)
    def _():
        for g, w in enumerate(POOL_WINDOWS):
            cols = slice(g * POOL_GROUP_WIDTH, (g + 1) * POOL_GROUP_WIDTH)
            for b in range(BATCH_PER_POOL_TILE):
                rows = slice(b * DEC_SEQ, (b + 1) * DEC_SEQ)
                cur = u_ref[rows, cols]
                ext = jnp.concatenate([state_ref[b, :, cols], cur], axis=0)
                d_ref[rows, cols] = window_diffs(ext, cur, float(w), w)

    for g in range(len(POOL_WINDOWS)):
        cols = slice(g * POOL_GROUP_WIDTH, (g + 1) * POOL_GROUP_WIDTH)
        y = jnp.dot(d_ref[:, cols].astype(BF16), w_ref[cols, :], preferred_element_type=F32)
        o_ref[:, cols] = (y * scale_ref[:, cols]).astype(BF16)


def _pool(z, state_halo, w_pool, scale, layer):
    halo_blocks_per_tile = TP_POOL // HALO
    vmem = (2 * TP_POOL * POOL_WIDTH * 4 + 2 * HALO * POOL_WIDTH * 4 + 2 * BATCH_PER_POOL_TILE * HALO * POOL_WIDTH * 4
            + 2 * POOL_WIDTH * POOL_GROUP_WIDTH * 2 + 2 * TP_POOL * POOL_WIDTH * 2 + TP_POOL * POOL_WIDTH * 4
            + (8 << 20))
    return pl.pallas_call(
        _pool_kernel,
        grid=(M_ALL // TP_POOL,),
        in_specs=[
            pl.BlockSpec((TP_POOL, POOL_WIDTH), lambda i: (i, Z_U0 // POOL_WIDTH)),
            pl.BlockSpec((HALO, POOL_WIDTH),
                         lambda i: (jnp.maximum(i * halo_blocks_per_tile - 1, 0), Z_U0 // POOL_WIDTH)),
            pl.BlockSpec((BATCH_PER_POOL_TILE, HALO, POOL_WIDTH),
                         lambda i: (jnp.maximum(i - POOL_TILES_PROMPT, 0), 0, 0)),
            pl.BlockSpec((None, POOL_WIDTH, POOL_GROUP_WIDTH), lambda i: (layer, 0, 0)),
            pl.BlockSpec((1, POOL_WIDTH), lambda i: (0, 0)),
        ],
        out_specs=pl.BlockSpec((TP_POOL, POOL_WIDTH), lambda i: (i, 0)),
        out_shape=jax.ShapeDtypeStruct((M_ALL, POOL_WIDTH), BF16),
        scratch_shapes=[pltpu.VMEM((TP_POOL, POOL_WIDTH), F32)],
        compiler_params=_params(vmem, 1),
        name="pool",
    )(z, z, state_halo, w_pool, scale)


def _mixer_out_kernel(ap_ref, as_ref, p_ref, xp_ref, xs_ref, wa_ref, wp_ref, o_ref):
    prompt = pl.program_id(0) < PROMPT_TILES
    a = jnp.where(prompt, ap_ref[...], as_ref[...])
    y = jnp.dot(a, wa_ref[...], preferred_element_type=F32)
    y = y + jnp.dot(p_ref[...], wp_ref[...], preferred_element_type=F32)
    o_ref[...] = jnp.where(prompt, xp_ref[...], xs_ref[...]) + y


def _mixer_out(attn_p, attn_s, pooled, x_p, x_s, w_out, layer):
    vmem = 2 * (3 * TM * ATTN_WIDTH * 2 + D_MODEL * TN_OUT * 2 + 3 * TM * TN_OUT * 4) + TM * ATTN_WIDTH * 2
    return pl.pallas_call(
        _mixer_out_kernel,
        grid=(ROW_TILES, D_MODEL // TN_OUT),
        in_specs=_x_specs(ATTN_WIDTH, lambda i, j: 0) + [
            pl.BlockSpec((TM, POOL_WIDTH), lambda i, j: (i, 0)),
        ] + _x_specs(TN_OUT, lambda i, j: j) + [
            pl.BlockSpec((None, ATTN_WIDTH, TN_OUT), lambda i, j: (layer, 0, j)),
            pl.BlockSpec((None, POOL_WIDTH, TN_OUT), lambda i, j: (layer, 1, j)),
        ],
        out_specs=pl.BlockSpec((TM, TN_OUT), lambda i, j: (i, j)),
        out_shape=jax.ShapeDtypeStruct((M_ALL, D_MODEL), F32),
        compiler_params=_params(vmem, 2),
        name="mixer_out",
    )(attn_p, attn_s, pooled, x_p, x_s, w_out, w_out)


def _mem_kv_kernel(m_ref, g_ref, wk_ref, wv_ref, kg_ref, k_ref, v_ref, h_ref):
    _norm_rows_to(m_ref, g_ref, h_ref, N_MEM)
    h = h_ref[...]
    k = jnp.dot(h, wk_ref[...], preferred_element_type=F32)
    for hd in range(MEM_HEADS):
        cols = slice(hd * MEM_HEAD_DIM, (hd + 1) * MEM_HEAD_DIM)
        k_ref[:, cols] = _rms_rows(k[:, cols], kg_ref[...])
    v_ref[...] = jnp.dot(h, wv_ref[...], preferred_element_type=F32)


def _mem_kv(mem, g, w_k, w_v, k_gain, layer):
    vmem = 2 * (N_MEM * D_MODEL * 4 + 2 * D_MODEL * MEM_WIDTH * 2 + 2 * N_MEM * MEM_WIDTH * 4) + N_MEM * D_MODEL * 2
    full = lambda shape: pl.BlockSpec(shape, lambda i: (0,) * len(shape))
    weight = pl.BlockSpec((None, D_MODEL, MEM_WIDTH), lambda i: (layer, 0, 0))
    return pl.pallas_call(
        _mem_kv_kernel,
        grid=(1,),
        in_specs=[full((N_MEM, D_MODEL)), full((1, D_MODEL)), weight, weight, full((1, MEM_HEAD_DIM))],
        out_specs=[full((N_MEM, MEM_WIDTH)), full((N_MEM, MEM_WIDTH))],
        out_shape=[jax.ShapeDtypeStruct((N_MEM, MEM_WIDTH), F32)] * 2,
        scratch_shapes=[pltpu.VMEM((N_MEM, D_MODEL), BF16)],
        compiler_params=_params(vmem, 1),
        name="mem_kv",
    )(mem, g, w_k, w_v, k_gain)


CROSS_TILES_PROMPT = M_PROMPT // TM_CROSS
BATCH_PER_CROSS_TILE = TM_CROSS // DEC_SEQ


def _cross_heads(q, mk_ref, mv_ref, b, qg, oc_ref, rows):
    for hd in range(MEM_HEADS):
        cols = slice(hd * MEM_HEAD_DIM, (hd + 1) * MEM_HEAD_DIM)
        qn = _rms_rows(q[:, cols], qg).astype(BF16)
        s = lax.dot_general(qn, mk_ref[b, :, cols].astype(BF16), (((1,), (1,)), ((), ())),
                            preferred_element_type=F32) * (MEM_HEAD_DIM ** -0.5)
        p = jnp.exp(s - jnp.max(s, axis=-1, keepdims=True))
        den = jnp.sum(p, axis=-1, keepdims=True)
        o = jnp.dot(p.astype(BF16), mv_ref[b, :, cols].astype(BF16), preferred_element_type=F32) / den
        oc_ref[rows, cols] = o.astype(BF16)


def _cross_kernel(x_ref, g_ref, wq_ref, qg_ref, mkp_ref, mvp_ref, mks_ref, mvs_ref, wo_ref,
                  o_ref, h_ref, q_ref, oc_ref):
    i = pl.program_id(0)
    _norm_rows_to(x_ref, g_ref, h_ref, TM_CROSS)
    q_ref[...] = jnp.dot(h_ref[...], wq_ref[...], preferred_element_type=F32)

    @pl.when(i < CROSS_TILES_PROMPT)
    def _():
        _cross_heads(q_ref[...], mkp_ref, mvp_ref, 0, qg_ref[...], oc_ref, slice(0, TM_CROSS))

    @pl.when(i >= CROSS_TILES_PROMPT)
    def _():
        for b in range(BATCH_PER_CROSS_TILE):
            rows = slice(b * DEC_SEQ, (b + 1) * DEC_SEQ)
            _cross_heads(q_ref[rows, :], mks_ref, mvs_ref, b, qg_ref[...], oc_ref, rows)

    o_ref[...] = x_ref[...] + jnp.dot(oc_ref[...], wo_ref[...], preferred_element_type=F32)


def _cross(x, g, w_q, q_gain, mk_p, mv_p, mk_s, mv_s, w_o, layer):
    const = lambda shape: pl.BlockSpec(shape, lambda i: (0,) * len(shape))
    once = lambda shape: pl.BlockSpec((None,) + shape, lambda i: (layer, 0, 0), pipeline_mode=pl.Buffered(1))
    sample_mem = pl.BlockSpec((BATCH_PER_CROSS_TILE, N_MEM, MEM_WIDTH),
                              lambda i: (jnp.maximum(i - CROSS_TILES_PROMPT, 0), 0, 0))
    vmem = (4 * TM_CROSS * D_MODEL * 4 + 2 * D_MODEL * MEM_WIDTH * 2
            + 4 * (1 + BATCH_PER_CROSS_TILE) * N_MEM * MEM_WIDTH * 4
            + TM_CROSS * D_MODEL * 2 + TM_CROSS * MEM_WIDTH * 6 + (6 << 20))
    return pl.pallas_call(
        _cross_kernel,
        grid=(M_ALL // TM_CROSS,),
        in_specs=[
            pl.BlockSpec((TM_CROSS, D_MODEL), lambda i: (i, 0)),
            const((1, D_MODEL)),
            once((D_MODEL, MEM_WIDTH)),
            const((1, MEM_HEAD_DIM)),
            const(mk_p.shape), const(mv_p.shape), sample_mem, sample_mem,
            once((MEM_WIDTH, D_MODEL)),
        ],
        out_specs=pl.BlockSpec((TM_CROSS, D_MODEL), lambda i: (i, 0)),
        out_shape=jax.ShapeDtypeStruct((M_ALL, D_MODEL), F32),
        scratch_shapes=[pltpu.VMEM((TM_CROSS, D_MODEL), BF16), pltpu.VMEM((TM_CROSS, MEM_WIDTH), F32),
                        pltpu.VMEM((TM_CROSS, MEM_WIDTH), BF16)],
        compiler_params=_params(vmem, 1),
        name="cross",
    )(x, g, w_q, q_gain, mk_p, mv_p, mk_s, mv_s, w_o)


GROUP = TF_SAMPLE
GROUPS_PER_TILE = TF_FFN // GROUP
GROUPS_LAST = TF_LAST // GROUP
N_GROUPS = D_FF // GROUP


def _swiglu_groups(gate_up, n_groups):
    acts = []
    for k in range(n_groups):
        gate = gate_up[:, 2 * k * GROUP:(2 * k + 1) * GROUP]
        up = gate_up[:, (2 * k + 1) * GROUP:(2 * k + 2) * GROUP]
        acts.append(gate * jax.nn.sigmoid(gate) * up)
    return (acts[0] if n_groups == 1 else jnp.concatenate(acts, axis=1)).astype(BF16)


def _ffn_prompt_kernel(x_ref, g_ref, wgu_ref, wd_ref, o_ref, h_ref):
    f = pl.program_id(1)

    @pl.when(f == 0)
    def _():
        _norm_rows_to(x_ref, g_ref, h_ref, TM)

    def accumulate(n_groups, first):
        gate_up = jnp.dot(h_ref[...], wgu_ref[:, :2 * n_groups * GROUP], preferred_element_type=F32)
        act = _swiglu_groups(gate_up, n_groups)
        base = x_ref if first else o_ref
        o_ref[...] = base[...] + jnp.dot(act, wd_ref[:n_groups * GROUP, :], preferred_element_type=F32)

    pl.when(f == 0)(functools.partial(accumulate, GROUPS_PER_TILE, True))
    pl.when((f > 0) & (f < NF_FFN - 1))(functools.partial(accumulate, GROUPS_PER_TILE, False))
    pl.when(f == NF_FFN - 1)(functools.partial(accumulate, GROUPS_LAST, False))


def _ffn_prompt(x, g, w_gate_up, w_down):
    vmem = 3 * TM * D_MODEL * 4 + TM * D_MODEL * 2 + 2 * 3 * D_MODEL * TF_FFN * 2 + 4 * TM * TF_FFN * 4
    return pl.pallas_call(
        _ffn_prompt_kernel,
        grid=(PROMPT_TILES, NF_FFN),
        in_specs=[
            pl.BlockSpec((TM, D_MODEL), lambda i, f: (i, 0)),
            pl.BlockSpec((1, D_MODEL), lambda i, f: (0, 0)),
            pl.BlockSpec((D_MODEL, 2 * TF_FFN), lambda i, f: (0, f)),
            pl.BlockSpec((TF_FFN, D_MODEL), lambda i, f: (f, 0)),
        ],
        out_specs=pl.BlockSpec((TM, D_MODEL), lambda i, f: (i, 0), pipeline_mode=pl.Buffered(1)),
        out_shape=jax.ShapeDtypeStruct((M_PROMPT, D_MODEL), F32),
        scratch_shapes=[pltpu.VMEM((TM, D_MODEL), BF16)],
        compiler_params=_params(vmem, 2),
        name="ffn_prompt",
    )(x, g, w_gate_up, w_down)


def _ffn_sample_kernel(x_ref, g_ref, wg_ref, wu_ref, wd_ref, o_ref, wgu_out, wd_out, h_ref):
    @pl.when(pl.program_id(0) == 0)
    def _():
        _norm_rows_to(x_ref, g_ref, h_ref, TM, copy_ref=o_ref)

    wgu_out[:, :GROUP] = wg_ref[...].astype(BF16)
    wgu_out[:, GROUP:] = wu_ref[...].astype(BF16)
    wd_out[...] = wd_ref[...].astype(BF16)
    act = _swiglu_groups(jnp.dot(h_ref[...], wgu_out[...], preferred_element_type=F32), 1)
    o_ref[...] += jnp.dot(act, wd_out[...], preferred_element_type=F32)


def _ffn_sample(x, g, w_gate, w_up, w_down, layer):
    vmem = 2 * TM * D_MODEL * 4 + TM * D_MODEL * 2 + 2 * 3 * D_MODEL * GROUP * 6 + 4 * TM * GROUP * 4
    once = pl.Buffered(1)
    col_f32 = pl.BlockSpec((None, D_MODEL, GROUP), lambda f: (layer, 0, f))
    return pl.pallas_call(
        _ffn_sample_kernel,
        grid=(N_GROUPS,),
        in_specs=[
            pl.BlockSpec((TM, D_MODEL), lambda f: (PROMPT_TILES, 0), pipeline_mode=once),
            pl.BlockSpec((1, D_MODEL), lambda f: (0, 0)),
            col_f32, col_f32,
            pl.BlockSpec((None, GROUP, D_MODEL), lambda f: (layer, f, 0)),
        ],
        out_specs=[
            pl.BlockSpec((TM, D_MODEL), lambda f: (0, 0), pipeline_mode=once),
            pl.BlockSpec((D_MODEL, 2 * GROUP), lambda f: (0, f)),
            pl.BlockSpec((GROUP, D_MODEL), lambda f: (f, 0)),
        ],
        out_shape=[jax.ShapeDtypeStruct((M_SAMPLE, D_MODEL), F32),
                   jax.ShapeDtypeStruct((D_MODEL, 2 * D_FF), BF16),
                   jax.ShapeDtypeStruct((D_FF, D_MODEL), BF16)],
        scratch_shapes=[pltpu.VMEM((TM, D_MODEL), BF16)],
        compiler_params=_params(vmem, 1),
        name="ffn_sample",
    )(x, g, w_gate, w_up, w_down)


def _row(v):
    return v.astype(F32).reshape(1, -1)


def kernel(x_prompt, x_sample, cache_attn_k, cache_attn_v, state_pool, cache_mem_k, cache_mem_v, mem_prompt,
           g_mix, w_in, q_norm, k_norm, attn_sinks, w_pool, pool_scale, w_out,
           g_cross, g_mem, w_q_mem, w_k_mem, w_v_mem, q_norm_mem, k_norm_mem, w_o_mem,
           g_ffn, w_gate, w_up, w_down):
    x_p, x_s = x_prompt.reshape(M_PROMPT, D_MODEL), x_sample.reshape(M_SAMPLE, D_MODEL)
    mem = mem_prompt.reshape(N_MEM, D_MODEL)

    w_in_b = _cast_w_in(w_in)
    w_out_b, w_q_b, w_k_b, w_v_b, w_o_b = (_cast_rows(w) for w in (w_out, w_q_mem, w_k_mem, w_v_mem, w_o_mem))
    w_pool_b = _cast_rows(w_pool.reshape(DEPTH, POOL_WIDTH, POOL_GROUP_WIDTH))
    bias = _swa_bias_tables()

    ak_p, av_p, pl_p, mk_p, mv_p, ak_s, av_s, pl_s = ([] for _ in range(8))
    for l in range(DEPTH):
        z = _mixer_in(x_p, x_s, _row(g_mix[l]), w_in_b, l)
        kn = _k_norm(z, _row(jnp.tile(k_norm[l], 2)))
        cache_k = cache_attn_k[l].astype(F32).reshape(DEC_BATCH, WINDOW, KV_WIDTH)
        cache_v = cache_attn_v[l].astype(F32).reshape(DEC_BATCH, WINDOW, KV_WIDTH)
        sink_e, sink_o = _swa_sink_rows(attn_sinks[l])
        q_gain2 = _row(jnp.tile(q_norm[l], 2)) * (HEAD_DIM ** -0.5)
        attn_p = _swa_prompt(z, kn, q_gain2, bias, sink_e, sink_o)
        attn_s = _swa_sample(z, kn, cache_k, cache_v, q_gain2, bias, sink_e, sink_o)

        state_halo = jnp.pad(state_pool[l].astype(F32), ((0, 0), (HALO - POOL_STATE, 0), (0, 0)))
        pooled = _pool(z, state_halo, w_pool_b, _row(pool_scale[l]), l)
        x = _mixer_out(attn_p, attn_s, pooled, x_p, x_s, w_out_b, l)

        v = z[:, Z_V0:Z_V0 + KV_WIDTH]
        kn_s = kn[M_PROMPT:].reshape(DEC_BATCH, DEC_SEQ, KV_WIDTH)
        v_s = v[M_PROMPT:].reshape(DEC_BATCH, DEC_SEQ, KV_WIDTH)
        ak_p.append(kn[M_PROMPT - WINDOW:M_PROMPT].reshape(1, WINDOW, N_KV_HEADS, HEAD_DIM))
        av_p.append(v[M_PROMPT - WINDOW:M_PROMPT].reshape(1, WINDOW, N_KV_HEADS, HEAD_DIM))
        pl_p.append(z[M_PROMPT - POOL_STATE:M_PROMPT, Z_U0:Z_U0 + POOL_WIDTH].reshape(1, POOL_STATE, POOL_WIDTH))
        ak_s.append(jnp.concatenate([cache_k, kn_s], axis=1)[:, -WINDOW:]
                    .reshape(DEC_BATCH, WINDOW, N_KV_HEADS, HEAD_DIM))
        av_s.append(jnp.concatenate([cache_v, v_s], axis=1)[:, -WINDOW:]
                    .reshape(DEC_BATCH, WINDOW, N_KV_HEADS, HEAD_DIM))
        u_s = z[M_PROMPT:, Z_U0:Z_U0 + POOL_WIDTH].reshape(DEC_BATCH, DEC_SEQ, POOL_WIDTH)
        pl_s.append(jnp.concatenate([state_pool[l].astype(F32), u_s], axis=1)[:, -POOL_STATE:])

        mk, mv = _mem_kv(mem, _row(g_mem[l]), w_k_b, w_v_b, _row(k_norm_mem[l]), l)
        mk_p.append(mk.reshape(1, N_MEM, MEM_HEADS, MEM_HEAD_DIM))
        mv_p.append(mv.reshape(1, N_MEM, MEM_HEADS, MEM_HEAD_DIM))
        x = _cross(x, _row(g_cross[l]), w_q_b, _row(q_norm_mem[l]),
                   mk.reshape(1, N_MEM, MEM_WIDTH), mv.reshape(1, N_MEM, MEM_WIDTH),
                   cache_mem_k[l].reshape(DEC_BATCH, N_MEM, MEM_WIDTH).astype(F32),
                   cache_mem_v[l].reshape(DEC_BATCH, N_MEM, MEM_WIDTH).astype(F32),
                   w_o_b, l)

        x_s, w_gate_up_b, w_down_b = _ffn_sample(x, _row(g_ffn[l]), w_gate, w_up, w_down, l)
        x_p = _ffn_prompt(x, _row(g_ffn[l]), w_gate_up_b, w_down_b)

    return (x_p.reshape(1, SEQ, D_MODEL), x_s.reshape(DEC_BATCH, DEC_SEQ, D_MODEL),
            jnp.stack(ak_p), jnp.stack(av_p), jnp.stack(pl_p), jnp.stack(mk_p), jnp.stack(mv_p),
            jnp.stack(ak_s), jnp.stack(av_s), jnp.stack(pl_s))
```

```python
import functools

import jax
import jax.numpy as jnp
from jax import lax
from jax.experimental import pallas as pl
from jax.experimental.pallas import tpu as pltpu

F32 = jnp.float32
BF16 = jnp.bfloat16

D_MODEL = 4096
SEQ = 8192
DEPTH = 2
DEC_BATCH = 8
DEC_SEQ = 64
CHUNK = 64
WINDOW = 128
ATTN_WIDTH = D_MODEL // 2
POOL_WIDTH = D_MODEL - ATTN_WIDTH
HEAD_DIM = 64
N_HEADS = ATTN_WIDTH // HEAD_DIM
N_KV_HEADS = N_HEADS // 8
KV_WIDTH = N_KV_HEADS * HEAD_DIM
POOL_WINDOWS = (2, 4, 8, 16)
POOL_GROUP_WIDTH = POOL_WIDTH // len(POOL_WINDOWS)
POOL_STATE = max(POOL_WINDOWS) - 1
IN_WIDTH = ATTN_WIDTH + 2 * KV_WIDTH + POOL_WIDTH
N_MEM = 256
MEM_HEADS = 4
MEM_HEAD_DIM = 128
MEM_WIDTH = MEM_HEADS * MEM_HEAD_DIM
D_FF = -(-8 * D_MODEL // (3 * 256)) * 256
EPS = 1e-6
LOG2_E = 1.4426950408889634
NEG_INF = -1e30

M_PROMPT = SEQ
M_SAMPLE = DEC_BATCH * DEC_SEQ
M_ALL = M_PROMPT + M_SAMPLE
N_CHUNKS_PROMPT = M_PROMPT // CHUNK
BAND = WINDOW + CHUNK
BAND_BLOCKS = BAND // CHUNK

Z_U0 = ATTN_WIDTH
Z_K0 = ATTN_WIDTH + POOL_WIDTH
Z_V0 = Z_K0 + KV_WIDTH

LANES = 128
SUBLANES = 8
BF16_TILE_ROWS = 16
HALO = BF16_TILE_ROWS
VMEM_LIMIT_CAP = 60000 * 1024
CAST_BLOCK_BYTES = 8 << 20

TM = 512
PROMPT_TILES = M_PROMPT // TM
ROW_TILES = M_ALL // TM
TN_IN = 768
TN_OUT = 1024
TF_FFN = 512
NF_FFN = -(-D_FF // TF_FFN)
TF_LAST = D_FF - (NF_FFN - 1) * TF_FFN
TF_SAMPLE = 256
TM_CROSS = 256
TM_KN = 512
TP_POOL = 512
NORM_ROWS = BF16_TILE_ROWS
NORM_UNROLL = 8
W_IN_COL_BLOCK = KV_WIDTH


def _params(vmem_bytes, n_axes):
    limit = min(int(vmem_bytes * 1.2) + (6 << 20), VMEM_LIMIT_CAP)
    return pltpu.CompilerParams(dimension_semantics=("arbitrary",) * n_axes, vmem_limit_bytes=limit)


def _rms_rows(x, g):
    return x * lax.rsqrt(jnp.mean(x * x, axis=-1, keepdims=True) + EPS) * g


def _norm_rows_to(x_ref, g_ref, h_ref, n_rows, copy_ref=None):
    def body(r, carry):
        rows = pl.ds(pl.multiple_of(r * NORM_ROWS, NORM_ROWS), NORM_ROWS)
        x = x_ref[rows, :]
        h_ref[rows, :] = _rms_rows(x, g_ref[...]).astype(BF16)
        if copy_ref is not None:
            copy_ref[rows, :] = x
        return carry

    lax.fori_loop(0, n_rows // NORM_ROWS, body, 0, unroll=NORM_UNROLL)


def _half_lane_rms(x, gain2):
    lo = lax.broadcasted_iota(jnp.int32, (1, LANES), 1) < HEAD_DIM
    x2 = x * x
    ss_lo = jnp.sum(jnp.where(lo, x2, 0.0), axis=-1, keepdims=True)
    ss_hi = jnp.sum(jnp.where(lo, 0.0, x2), axis=-1, keepdims=True)
    r = jnp.where(lo, lax.rsqrt(ss_lo / HEAD_DIM + EPS), lax.rsqrt(ss_hi / HEAD_DIM + EPS))
    return x * r * gain2


def _cast_kernel(w_ref, o_ref):
    o_ref[...] = w_ref[...].astype(BF16)


def _cast_rows(w):
    depth, rows, cols = w.shape
    tr = max(t for t in range(BF16_TILE_ROWS, rows + 1, BF16_TILE_ROWS)
             if rows % t == 0 and t * cols * 4 <= CAST_BLOCK_BYTES)
    spec = pl.BlockSpec((None, tr, cols), lambda l, i: (l, i, 0))
    return pl.pallas_call(
        _cast_kernel, grid=(depth, rows // tr), in_specs=[spec], out_specs=spec,
        out_shape=jax.ShapeDtypeStruct(w.shape, BF16),
        compiler_params=_params(2 * tr * cols * 6, 2), name="cast_rows",
    )(w)


def _w_in_source_block(j):
    q_blocks, u_blocks = ATTN_WIDTH // W_IN_COL_BLOCK, POOL_WIDTH // W_IN_COL_BLOCK
    kv_blocks = 2 * KV_WIDTH // W_IN_COL_BLOCK
    return jnp.where(j < q_blocks, j, jnp.where(j < q_blocks + u_blocks, j + kv_blocks, j - u_blocks))


def _cast_w_in(w):
    depth = w.shape[0]
    return pl.pallas_call(
        _cast_kernel, grid=(depth, IN_WIDTH // W_IN_COL_BLOCK),
        in_specs=[pl.BlockSpec((None, D_MODEL, W_IN_COL_BLOCK), lambda l, j: (l, 0, _w_in_source_block(j)))],
        out_specs=pl.BlockSpec((None, D_MODEL, W_IN_COL_BLOCK), lambda l, j: (l, 0, j)),
        out_shape=jax.ShapeDtypeStruct(w.shape, BF16),
        compiler_params=_params(2 * D_MODEL * W_IN_COL_BLOCK * 6, 2), name="cast_w_in",
    )(w)


def _mixer_in_kernel(xp_ref, xs_ref, g_ref, w_ref, z_ref, h_ref):
    i, j = pl.program_id(0), pl.program_id(1)

    @pl.when((j == 0) & (i < PROMPT_TILES))
    def _():
        _norm_rows_to(xp_ref, g_ref, h_ref, TM)

    @pl.when((j == 0) & (i >= PROMPT_TILES))
    def _():
        _norm_rows_to(xs_ref, g_ref, h_ref, TM)

    z_ref[...] = jnp.dot(h_ref[...], w_ref[...], preferred_element_type=F32)


def _x_specs(width, col):
    return [
        pl.BlockSpec((TM, width), lambda i, j: (jnp.minimum(i, PROMPT_TILES - 1), col(i, j))),
        pl.BlockSpec((TM, width), lambda i, j: (0, jnp.where(i >= PROMPT_TILES, col(i, j), 0))),
    ]


def _mixer_in(x_p, x_s, g, w, layer):
    vmem = 4 * TM * D_MODEL * 4 + TM * D_MODEL * 2 + 2 * D_MODEL * TN_IN * 2 + 2 * TM * TN_IN * 4
    return pl.pallas_call(
        _mixer_in_kernel,
        grid=(ROW_TILES, IN_WIDTH // TN_IN),
        in_specs=_x_specs(D_MODEL, lambda i, j: 0) + [
            pl.BlockSpec((1, D_MODEL), lambda i, j: (0, 0)),
            pl.BlockSpec((None, D_MODEL, TN_IN), lambda i, j: (layer, 0, j)),
        ],
        out_specs=pl.BlockSpec((TM, TN_IN), lambda i, j: (i, j)),
        out_shape=jax.ShapeDtypeStruct((M_ALL, IN_WIDTH), F32),
        scratch_shapes=[pltpu.VMEM((TM, D_MODEL), BF16)],
        compiler_params=_params(vmem, 2),
        name="mixer_in",
    )(x_p, x_s, g, w)


def _k_norm_kernel(k_ref, g_ref, o_ref):
    for t in range(KV_WIDTH // LANES):
        cols = slice(t * LANES, (t + 1) * LANES)
        o_ref[:, cols] = _half_lane_rms(k_ref[:, cols], g_ref[...])


def _k_norm(z, gain2):
    vmem = 4 * TM_KN * KV_WIDTH * 4
    return pl.pallas_call(
        _k_norm_kernel,
        grid=(M_ALL // TM_KN,),
        in_specs=[
            pl.BlockSpec((TM_KN, KV_WIDTH), lambda i: (i, Z_K0 // KV_WIDTH)),
            pl.BlockSpec((1, LANES), lambda i: (0, 0)),
        ],
        out_specs=pl.BlockSpec((TM_KN, KV_WIDTH), lambda i: (i, 0)),
        out_shape=jax.ShapeDtypeStruct((M_ALL, KV_WIDTH), F32),
        compiler_params=_params(vmem, 1),
        name="k_norm",
    )(z, gain2)


PAIRS_PER_KV = N_HEADS // N_KV_HEADS // 2
QUERIES_PER_KV = PAIRS_PER_KV * CHUNK
N_PAIRS = N_HEADS // 2
SWA_CHUNKS = 8
Z_V_BLOCK = Z_V0 // KV_WIDTH


def _swa_chunk(q_ref, rows, k_refs, v_refs, qg_ref, bias_ref, table, sink_e_ref, sink_o_ref, o_ref):
    lo = lax.broadcasted_iota(jnp.int32, (1, LANES), 1) < HEAD_DIM

    q = jnp.concatenate([q_ref[rows, p * LANES:(p + 1) * LANES] for p in range(N_PAIRS)], axis=0)

    qn = _half_lane_rms(q, qg_ref[...]).astype(BF16)
    nt = (((1,), (1,)), ((), ()))

    kb = jnp.concatenate([ref[...] for ref in k_refs], axis=0)
    vb = jnp.concatenate([ref[...] for ref in v_refs] + [jnp.zeros((KV_WIDTH - BAND, KV_WIDTH), F32)], axis=0)
    v_t = vb.T

    def head_softmax(s_raw, bias, sink):
        s = s_raw - bias
        m = jnp.maximum(jnp.max(s, axis=0, keepdims=True), sink)
        p = jnp.exp2(s - m)
        return p.astype(BF16), jnp.sum(p, axis=0, keepdims=True) + jnp.exp2(sink - m)

    for kv_tile in range(KV_WIDTH // LANES):
        kt = kb[:, kv_tile * LANES:(kv_tile + 1) * LANES]
        kt_r = pltpu.roll(kt, HEAD_DIM, axis=1)
        for par in range(2):
            hk = 2 * kv_tile + par
            k_e = jnp.where(lo, kt_r if par else kt, 0.0)
            k_o = jnp.where(lo, 0.0, kt if par else kt_r)
            kk = jnp.concatenate([k_e, k_o], axis=0).astype(BF16)
            cols = slice(hk * QUERIES_PER_KV, (hk + 1) * QUERIES_PER_KV)
            s_t = lax.dot_general(kk, qn[cols], nt, preferred_element_type=F32)
            bias = bias_ref[table, hk]
            p_e, d_e = head_softmax(s_t[:BAND], bias[:BAND], sink_e_ref[hk])
            p_o, d_o = head_softmax(s_t[BAND:], bias[BAND:], sink_o_ref[hk])
            v_h = v_t[hk * HEAD_DIM:(hk + 1) * HEAD_DIM, :BAND].astype(BF16)
            o_e = jnp.dot(v_h, p_e, preferred_element_type=F32) / d_e
            o_o = jnp.dot(v_h, p_o, preferred_element_type=F32) / d_o
            o = jnp.concatenate([o_e, o_o], axis=0).T
            for pr in range(PAIRS_PER_KV):
                pair = hk * PAIRS_PER_KV + pr
                o_ref[rows, pair * LANES:(pair + 1) * LANES] = o[pr * CHUNK:(pr + 1) * CHUNK].astype(BF16)


def _swa_kernel(q_ref, *refs, prompt):
    n_band = SWA_CHUNKS * BAND_BLOCKS
    k_refs, v_refs = refs[:n_band], refs[n_band:2 * n_band]
    qg_ref, bias_ref, sink_e_ref, sink_o_ref, o_ref = refs[2 * n_band:]
    for ci in range(SWA_CHUNKS):
        table = jnp.minimum(pl.program_id(0) * SWA_CHUNKS + ci, BAND_BLOCKS - 1) if prompt else 0
        band = slice(ci * BAND_BLOCKS, (ci + 1) * BAND_BLOCKS)
        _swa_chunk(q_ref, slice(ci * CHUNK, (ci + 1) * CHUNK), k_refs[band], v_refs[band],
                   qg_ref, bias_ref, table, sink_e_ref, sink_o_ref, o_ref)


def _swa_call(prompt, n_rows, q_block0, band_k, band_v, operands, q_gain2, bias, sink_e, sink_o):
    step_rows = SWA_CHUNKS * CHUNK
    const = lambda a: pl.BlockSpec(a.shape, lambda s: (0,) * a.ndim)
    vmem = (4 * step_rows * ATTN_WIDTH * 4 + 8 * SWA_CHUNKS * BAND * KV_WIDTH * 4 + 2 * bias.size * 4
            + SWA_CHUNKS * (12 << 20))
    return pl.pallas_call(
        functools.partial(_swa_kernel, prompt=prompt),
        grid=(n_rows // step_rows,),
        in_specs=[pl.BlockSpec((step_rows, ATTN_WIDTH), lambda s: (q_block0 + s, 0))] + band_k + band_v + [
            pl.BlockSpec((1, LANES), lambda s: (0, 0)), const(bias), const(sink_e), const(sink_o)],
        out_specs=pl.BlockSpec((step_rows, ATTN_WIDTH), lambda s: (s, 0)),
        out_shape=jax.ShapeDtypeStruct((n_rows, ATTN_WIDTH), BF16),
        compiler_params=_params(vmem, 1),
        name="swa_prompt" if prompt else "swa_sample",
    )(*operands, q_gain2, bias, sink_e, sink_o)


def _swa_prompt(z, kn, q_gain2, bias, sink_e, sink_o):
    def spec(ci, j, col):
        return pl.BlockSpec((CHUNK, KV_WIDTH),
                            lambda s: (jnp.maximum(s * SWA_CHUNKS + ci - (BAND_BLOCKS - 1) + j, 0), col))
    slots = [(ci, j) for ci in range(SWA_CHUNKS) for j in range(BAND_BLOCKS)]
    band_k = [spec(ci, j, 0) for ci, j in slots]
    band_v = [spec(ci, j, Z_V_BLOCK) for ci, j in slots]
    operands = [z] + [kn] * len(slots) + [z] * len(slots)
    return _swa_call(True, M_PROMPT, 0, band_k, band_v, operands, q_gain2, bias, sink_e, sink_o)


def _swa_sample(z, kn, cache_k, cache_v, q_gain2, bias, sink_e, sink_o):
    cache_blocks = WINDOW // CHUNK
    first_chunk = N_CHUNKS_PROMPT

    def specs(new_col):
        out = []
        for ci in range(SWA_CHUNKS):
            out += [pl.BlockSpec((None, CHUNK, KV_WIDTH), functools.partial(lambda s, ci, j: (s * SWA_CHUNKS + ci, j, 0),
                                                                            ci=ci, j=j))
                    for j in range(cache_blocks)]
            out.append(pl.BlockSpec((CHUNK, KV_WIDTH),
                                    functools.partial(lambda s, ci: (first_chunk + s * SWA_CHUNKS + ci, new_col), ci=ci)))
        return out

    operands = [z] + ([cache_k] * cache_blocks + [kn]) * SWA_CHUNKS + ([cache_v] * cache_blocks + [z]) * SWA_CHUNKS
    return _swa_call(False, M_SAMPLE, M_PROMPT // (SWA_CHUNKS * CHUNK), specs(0), specs(Z_V_BLOCK), operands,
                     q_gain2, bias[BAND_BLOCKS - 1:], sink_e, sink_o)


def _swa_bias_tables():
    slopes = 2.0 ** (-8.0 * jnp.arange(1, N_HEADS + 1, dtype=F32) / N_HEADS)
    key = jnp.arange(BAND)
    dist = jnp.abs(jnp.arange(CHUNK)[None, :] + WINDOW - key[:, None]).astype(F32)
    head = (jnp.arange(N_KV_HEADS)[:, None, None] * (2 * PAIRS_PER_KV)
            + jnp.arange(2)[None, :, None] + 2 * jnp.arange(PAIRS_PER_KV)[None, None, :])
    bias = slopes[head][:, :, None, :, None] * dist[None, None, :, None, :]
    first_valid = jnp.maximum(WINDOW - CHUNK * jnp.arange(BAND_BLOCKS), 0)
    valid = key[None, :] >= first_valid[:, None]
    tables = jnp.where(valid[:, None, None, :, None, None], bias[None] * LOG2_E, -NEG_INF)
    return tables.reshape(BAND_BLOCKS, N_KV_HEADS, 2 * BAND, QUERIES_PER_KV)


def _swa_sink_rows(sinks):
    per_pair = sinks.astype(F32).reshape(N_KV_HEADS, PAIRS_PER_KV, 2) * LOG2_E
    rows = jnp.repeat(per_pair, CHUNK, axis=1)
    return rows[:, None, :, 0], rows[:, None, :, 1]


POOL_TILES_PROMPT = M_PROMPT // TP_POOL
BATCH_PER_POOL_TILE = TP_POOL // DEC_SEQ


def _trailing_sum(ext, w):
    s, k = ext, 1
    while k < w:
        s = s + pltpu.roll(s, k, axis=0)
        k *= 2
    return s


def _pool_kernel(u_ref, halo_ref, state_ref, w_ref, scale_ref, o_ref, d_ref):
    i = pl.program_id(0)

    def window_diffs(ext, cur, cnt, w):
        return _trailing_sum(ext, w)[HALO:] / cnt - cur

    @pl.when(i < POOL_TILES_PROMPT)
    def _():
        pos = i * TP_POOL + lax.broadcasted_iota(jnp.int32, (TP_POOL, 1), 0)
        for g, w in enumerate(POOL_WINDOWS):
            cols = slice(g * POOL_GROUP_WIDTH, (g + 1) * POOL_GROUP_WIDTH)
            cur = u_ref[:, cols]
            halo = jnp.where(i > 0, halo_ref[:, cols], 0.0)
            ext = jnp.concatenate([halo, cur], axis=0)
            cnt = jnp.minimum(pos + 1, w).astype(F32)
            d_ref[:, cols] = window_diffs(ext, cur, cnt, w)

    @pl.when(i >= POOL_TILES_PROMPT)
    def _():
        for g, w in enumerate(POOL_WINDOWS):
            cols = slice(g * POOL_GROUP_WIDTH, (g + 1) * POOL_GROUP_WIDTH)
            for b in range(BATCH_PER_POOL_TILE):
                rows = slice(b * DEC_SEQ, (b + 1) * DEC_SEQ)
                cur = u_ref[rows, cols]
                ext = jnp.concatenate([state_ref[b, :, cols], cur], axis=0)
                d_ref[rows, cols] = window_diffs(ext, cur, float(w), w)

    for g in range(len(POOL_WINDOWS)):
        cols = slice(g * POOL_GROUP_WIDTH, (g + 1) * POOL_GROUP_WIDTH)
        y = jnp.dot(d_ref[:, cols].astype(BF16), w_ref[cols, :], preferred_element_type=F32)
        o_ref[:, cols] = (y * scale_ref[:, cols]).astype(BF16)


def _pool(z, state_halo, w_pool, scale, layer):
    halo_blocks_per_tile = TP_POOL // HALO
    vmem = (2 * TP_POOL * POOL_WIDTH * 4 + 2 * HALO * POOL_WIDTH * 4 + 2 * BATCH_PER_POOL_TILE * HALO * POOL_WIDTH * 4
            + 2 * POOL_WIDTH * POOL_GROUP_WIDTH * 2 + 2 * TP_POOL * POOL_WIDTH * 2 + TP_POOL * POOL_WIDTH * 4
            + (8 << 20))
    return pl.pallas_call(
        _pool_kernel,
        grid=(M_ALL // TP_POOL,),
        in_specs=[
            pl.BlockSpec((TP_POOL, POOL_WIDTH), lambda i: (i, Z_U0 // POOL_WIDTH)),
            pl.BlockSpec((HALO, POOL_WIDTH),
                         lambda i: (jnp.maximum(i * halo_blocks_per_tile - 1, 0), Z_U0 // POOL_WIDTH)),
            pl.BlockSpec((BATCH_PER_POOL_TILE, HALO, POOL_WIDTH),
                         lambda i: (jnp.maximum(i - POOL_TILES_PROMPT, 0), 0, 0)),
            pl.BlockSpec((None, POOL_WIDTH, POOL_GROUP_WIDTH), lambda i: (layer, 0, 0)),
            pl.BlockSpec((1, POOL_WIDTH), lambda i: (0, 0)),
        ],
        out_specs=pl.BlockSpec((TP_POOL, POOL_WIDTH), lambda i: (i, 0)),
        out_shape=jax.ShapeDtypeStruct((M_ALL, POOL_WIDTH), BF16),
        scratch_shapes=[pltpu.VMEM((TP_POOL, POOL_WIDTH), F32)],
        compiler_params=_params(vmem, 1),
        name="pool",
    )(z, z, state_halo, w_pool, scale)


def _mixer_out_kernel(ap_ref, as_ref, p_ref, xp_ref, xs_ref, wa_ref, wp_ref, o_ref):
    prompt = pl.program_id(0) < PROMPT_TILES
    a = jnp.where(prompt, ap_ref[...], as_ref[...])
    y = jnp.dot(a, wa_ref[...], preferred_element_type=F32)
    y = y + jnp.dot(p_ref[...], wp_ref[...], preferred_element_type=F32)
    o_ref[...] = jnp.where(prompt, xp_ref[...], xs_ref[...]) + y


def _mixer_out(attn_p, attn_s, pooled, x_p, x_s, w_out, layer):
    vmem = 2 * (3 * TM * ATTN_WIDTH * 2 + D_MODEL * TN_OUT * 2 + 3 * TM * TN_OUT * 4) + TM * ATTN_WIDTH * 2
    return pl.pallas_call(
        _mixer_out_kernel,
        grid=(ROW_TILES, D_MODEL // TN_OUT),
        in_specs=_x_specs(ATTN_WIDTH, lambda i, j: 0) + [
            pl.BlockSpec((TM, POOL_WIDTH), lambda i, j: (i, 0)),
        ] + _x_specs(TN_OUT, lambda i, j: j) + [
            pl.BlockSpec((None, ATTN_WIDTH, TN_OUT), lambda i, j: (layer, 0, j)),
            pl.BlockSpec((None, POOL_WIDTH, TN_OUT), lambda i, j: (layer, 1, j)),
        ],
        out_specs=pl.BlockSpec((TM, TN_OUT), lambda i, j: (i, j)),
        out_shape=jax.ShapeDtypeStruct((M_ALL, D_MODEL), F32),
        compiler_params=_params(vmem, 2),
        name="mixer_out",
    )(attn_p, attn_s, pooled, x_p, x_s, w_out, w_out)


def _mem_kv_kernel(m_ref, g_ref, wk_ref, wv_ref, kg_ref, k_ref, v_ref, h_ref):
    _norm_rows_to(m_ref, g_ref, h_ref, N_MEM)
    h = h_ref[...]
    k = jnp.dot(h, wk_ref[...], preferred_element_type=F32)
    for hd in range(MEM_HEADS):
        cols = slice(hd * MEM_HEAD_DIM, (hd + 1) * MEM_HEAD_DIM)
        k_ref[:, cols] = _rms_rows(k[:, cols], kg_ref[...])
    v_ref[...] = jnp.dot(h, wv_ref[...], preferred_element_type=F32)


def _mem_kv(mem, g, w_k, w_v, k_gain, layer):
    vmem = 2 * (N_MEM * D_MODEL * 4 + 2 * D_MODEL * MEM_WIDTH * 2 + 2 * N_MEM * MEM_WIDTH * 4) + N_MEM * D_MODEL * 2
    full = lambda shape: pl.BlockSpec(shape, lambda i: (0,) * len(shape))
    weight = pl.BlockSpec((None, D_MODEL, MEM_WIDTH), lambda i: (layer, 0, 0))
    return pl.pallas_call(
        _mem_kv_kernel,
        grid=(1,),
        in_specs=[full((N_MEM, D_MODEL)), full((1, D_MODEL)), weight, weight, full((1, MEM_HEAD_DIM))],
        out_specs=[full((N_MEM, MEM_WIDTH)), full((N_MEM, MEM_WIDTH))],
        out_shape=[jax.ShapeDtypeStruct((N_MEM, MEM_WIDTH), F32)] * 2,
        scratch_shapes=[pltpu.VMEM((N_MEM, D_MODEL), BF16)],
        compiler_params=_params(vmem, 1),
        name="mem_kv",
    )(mem, g, w_k, w_v, k_gain)


CROSS_TILES_PROMPT = M_PROMPT // TM_CROSS
BATCH_PER_CROSS_TILE = TM_CROSS // DEC_SEQ


def _cross_heads(q, mk_ref, mv_ref, b, qg, oc_ref, rows):
    for hd in range(MEM_HEADS):
        cols = slice(hd * MEM_HEAD_DIM, (hd + 1) * MEM_HEAD_DIM)
        qn = _rms_rows(q[:, cols], qg).astype(BF16)
        s = lax.dot_general(qn, mk_ref[b, :, cols].astype(BF16), (((1,), (1,)), ((), ())),
                            preferred_element_type=F32) * (MEM_HEAD_DIM ** -0.5)
        p = jnp.exp(s - jnp.max(s, axis=-1, keepdims=True))
        den = jnp.sum(p, axis=-1, keepdims=True)
        o = jnp.dot(p.astype(BF16), mv_ref[b, :, cols].astype(BF16), preferred_element_type=F32) / den
        oc_ref[rows, cols] = o.astype(BF16)


def _cross_kernel(x_ref, g_ref, wq_ref, qg_ref, mkp_ref, mvp_ref, mks_ref, mvs_ref, wo_ref,
                  o_ref, h_ref, q_ref, oc_ref):
    i = pl.program_id(0)
    _norm_rows_to(x_ref, g_ref, h_ref, TM_CROSS)
    q_ref[...] = jnp.dot(h_ref[...], wq_ref[...], preferred_element_type=F32)

    @pl.when(i < CROSS_TILES_PROMPT)
    def _():
        _cross_heads(q_ref[...], mkp_ref, mvp_ref, 0, qg_ref[...], oc_ref, slice(0, TM_CROSS))

    @pl.when(i >= CROSS_TILES_PROMPT)
    def _():
        for b in range(BATCH_PER_CROSS_TILE):
            rows = slice(b * DEC_SEQ, (b + 1) * DEC_SEQ)
            _cross_heads(q_ref[rows, :], mks_ref, mvs_ref, b, qg_ref[...], oc_ref, rows)

    o_ref[...] = x_ref[...] + jnp.dot(oc_ref[...], wo_ref[...], preferred_element_type=F32)


def _cross(x, g, w_q, q_gain, mk_p, mv_p, mk_s, mv_s, w_o, layer):
    const = lambda shape: pl.BlockSpec(shape, lambda i: (0,) * len(shape))
    once = lambda shape: pl.BlockSpec((None,) + shape, lambda i: (layer, 0, 0), pipeline_mode=pl.Buffered(1))
    sample_mem = pl.BlockSpec((BATCH_PER_CROSS_TILE, N_MEM, MEM_WIDTH),
                              lambda i: (jnp.maximum(i - CROSS_TILES_PROMPT, 0), 0, 0))
    vmem = (4 * TM_CROSS * D_MODEL * 4 + 2 * D_MODEL * MEM_WIDTH * 2
            + 4 * (1 + BATCH_PER_CROSS_TILE) * N_MEM * MEM_WIDTH * 4
            + TM_CROSS * D_MODEL * 2 + TM_CROSS * MEM_WIDTH * 6 + (6 << 20))
    return pl.pallas_call(
        _cross_kernel,
        grid=(M_ALL // TM_CROSS,),
        in_specs=[
            pl.BlockSpec((TM_CROSS, D_MODEL), lambda i: (i, 0)),
            const((1, D_MODEL)),
            once((D_MODEL, MEM_WIDTH)),
            const((1, MEM_HEAD_DIM)),
            const(mk_p.shape), const(mv_p.shape), sample_mem, sample_mem,
            once((MEM_WIDTH, D_MODEL)),
        ],
        out_specs=pl.BlockSpec((TM_CROSS, D_MODEL), lambda i: (i, 0)),
        out_shape=jax.ShapeDtypeStruct((M_ALL, D_MODEL), F32),
        scratch_shapes=[pltpu.VMEM((TM_CROSS, D_MODEL), BF16), pltpu.VMEM((TM_CROSS, MEM_WIDTH), F32),
                        pltpu.VMEM((TM_CROSS, MEM_WIDTH), BF16)],
        compiler_params=_params(vmem, 1),
        name="cross",
    )(x, g, w_q, q_gain, mk_p, mv_p, mk_s, mv_s, w_o)


GROUP = TF_SAMPLE
GROUPS_PER_TILE = TF_FFN // GROUP
GROUPS_LAST = TF_LAST // GROUP
N_GROUPS = D_FF // GROUP


def _swiglu_groups(gate_up, n_groups):
    acts = []
    for k in range(n_groups):
        gate = gate_up[:, 2 * k * GROUP:(2 * k + 1) * GROUP]
        up = gate_up[:, (2 * k + 1) * GROUP:(2 * k + 2) * GROUP]
        acts.append(gate * jax.nn.sigmoid(gate) * up)
    return (acts[0] if n_groups == 1 else jnp.concatenate(acts, axis=1)).astype(BF16)


def _ffn_prompt_kernel(x_ref, g_ref, wgu_ref, wd_ref, o_ref, h_ref):
    f = pl.program_id(1)

    @pl.when(f == 0)
    def _():
        _norm_rows_to(x_ref, g_ref, h_ref, TM)

    def accumulate(n_groups, first):
        gate_up = jnp.dot(h_ref[...], wgu_ref[:, :2 * n_groups * GROUP], preferred_element_type=F32)
        act = _swiglu_groups(gate_up, n_groups)
        base = x_ref if first else o_ref
        o_ref[...] = base[...] + jnp.dot(act, wd_ref[:n_groups * GROUP, :], preferred_element_type=F32)

    pl.when(f == 0)(functools.partial(accumulate, GROUPS_PER_TILE, True))
    pl.when((f > 0) & (f < NF_FFN - 1))(functools.partial(accumulate, GROUPS_PER_TILE, False))
    pl.when(f == NF_FFN - 1)(functools.partial(accumulate, GROUPS_LAST, False))


def _ffn_prompt(x, g, w_gate_up, w_down):
    vmem = 3 * TM * D_MODEL * 4 + TM * D_MODEL * 2 + 2 * 3 * D_MODEL * TF_FFN * 2 + 4 * TM * TF_FFN * 4
    return pl.pallas_call(
        _ffn_prompt_kernel,
        grid=(PROMPT_TILES, NF_FFN),
        in_specs=[
            pl.BlockSpec((TM, D_MODEL), lambda i, f: (i, 0)),
            pl.BlockSpec((1, D_MODEL), lambda i, f: (0, 0)),
            pl.BlockSpec((D_MODEL, 2 * TF_FFN), lambda i, f: (0, f)),
            pl.BlockSpec((TF_FFN, D_MODEL), lambda i, f: (f, 0)),
        ],
        out_specs=pl.BlockSpec((TM, D_MODEL), lambda i, f: (i, 0), pipeline_mode=pl.Buffered(1)),
        out_shape=jax.ShapeDtypeStruct((M_PROMPT, D_MODEL), F32),
        scratch_shapes=[pltpu.VMEM((TM, D_MODEL), BF16)],
        compiler_params=_params(vmem, 2),
        name="ffn_prompt",
    )(x, g, w_gate_up, w_down)


def _ffn_sample_kernel(x_ref, g_ref, wg_ref, wu_ref, wd_ref, o_ref, wgu_out, wd_out, h_ref):
    @pl.when(pl.program_id(0) == 0)
    def _():
        _norm_rows_to(x_ref, g_ref, h_ref, TM, copy_ref=o_ref)

    wgu_out[:, :GROUP] = wg_ref[...].astype(BF16)
    wgu_out[:, GROUP:] = wu_ref[...].astype(BF16)
    wd_out[...] = wd_ref[...].astype(BF16)
    act = _swiglu_groups(jnp.dot(h_ref[...], wgu_out[...], preferred_element_type=F32), 1)
    o_ref[...] += jnp.dot(act, wd_out[...], preferred_element_type=F32)


def _ffn_sample(x, g, w_gate, w_up, w_down, layer):
    vmem = 2 * TM * D_MODEL * 4 + TM * D_MODEL * 2 + 2 * 3 * D_MODEL * GROUP * 6 + 4 * TM * GROUP * 4
    once = pl.Buffered(1)
    col_f32 = pl.BlockSpec((None, D_MODEL, GROUP), lambda f: (layer, 0, f))
    return pl.pallas_call(
        _ffn_sample_kernel,
        grid=(N_GROUPS,),
        in_specs=[
            pl.BlockSpec((TM, D_MODEL), lambda f: (PROMPT_TILES, 0), pipeline_mode=once),
            pl.BlockSpec((1, D_MODEL), lambda f: (0, 0)),
            col_f32, col_f32,
            pl.BlockSpec((None, GROUP, D_MODEL), lambda f: (layer, f, 0)),
        ],
        out_specs=[
            pl.BlockSpec((TM, D_MODEL), lambda f: (0, 0), pipeline_mode=once),
            pl.BlockSpec((D_MODEL, 2 * GROUP), lambda f: (0, f)),
            pl.BlockSpec((GROUP, D_MODEL), lambda f: (f, 0)),
        ],
        out_shape=[jax.ShapeDtypeStruct((M_SAMPLE, D_MODEL), F32),
                   jax.ShapeDtypeStruct((D_MODEL, 2 * D_FF), BF16),
                   jax.ShapeDtypeStruct((D_FF, D_MODEL), BF16)],
        scratch_shapes=[pltpu.VMEM((TM, D_MODEL), BF16)],
        compiler_params=_params(vmem, 1),
        name="ffn_sample",
    )(x, g, w_gate, w_up, w_down)


def _row(v):
    return v.astype(F32).reshape(1, -1)


def kernel(x_prompt, x_sample, cache_attn_k, cache_attn_v, state_pool, cache_mem_k, cache_mem_v, mem_prompt,
           g_mix, w_in, q_norm, k_norm, attn_sinks, w_pool, pool_scale, w_out,
           g_cross, g_mem, w_q_mem, w_k_mem, w_v_mem, q_norm_mem, k_norm_mem, w_o_mem,
           g_ffn, w_gate, w_up, w_down):
    x_p, x_s = x_prompt.reshape(M_PROMPT, D_MODEL), x_sample.reshape(M_SAMPLE, D_MODEL)
    mem = mem_prompt.reshape(N_MEM, D_MODEL)

    w_in_b = _cast_w_in(w_in)
    w_out_b, w_q_b, w_k_b, w_v_b, w_o_b = (_cast_rows(w) for w in (w_out, w_q_mem, w_k_mem, w_v_mem, w_o_mem))
    w_pool_b = _cast_rows(w_pool.reshape(DEPTH, POOL_WIDTH, POOL_GROUP_WIDTH))
    bias = _swa_bias_tables()

    ak_p, av_p, pl_p, mk_p, mv_p, ak_s, av_s, pl_s = ([] for _ in range(8))
    for l in range(DEPTH):
        z = _mixer_in(x_p, x_s, _row(g_mix[l]), w_in_b, l)
        kn = _k_norm(z, _row(jnp.tile(k_norm[l], 2)))
        cache_k = cache_attn_k[l].astype(F32).reshape(DEC_BATCH, WINDOW, KV_WIDTH)
        cache_v = cache_attn_v[l].astype(F32).reshape(DEC_BATCH, WINDOW, KV_WIDTH)
        sink_e, sink_o = _swa_sink_rows(attn_sinks[l])
        q_gain2 = _row(jnp.tile(q_norm[l], 2)) * (HEAD_DIM ** -0.5 * LOG2_E)
        attn_p = _swa_prompt(z, kn, q_gain2, bias, sink_e, sink_o)
        attn_s = _swa_sample(z, kn, cache_k, cache_v, q_gain2, bias, sink_e, sink_o)

        state_halo = jnp.pad(state_pool[l].astype(F32), ((0, 0), (HALO - POOL_STATE, 0), (0, 0)))
        pooled = _pool(z, state_halo, w_pool_b, _row(pool_scale[l]), l)
        x = _mixer_out(attn_p, attn_s, pooled, x_p, x_s, w_out_b, l)

        v = z[:, Z_V0:Z_V0 + KV_WIDTH]
        kn_s = kn[M_PROMPT:].reshape(DEC_BATCH, DEC_SEQ, KV_WIDTH)
        v_s = v[M_PROMPT:].reshape(DEC_BATCH, DEC_SEQ, KV_WIDTH)
        ak_p.append(kn[M_PROMPT - WINDOW:M_PROMPT].reshape(1, WINDOW, N_KV_HEADS, HEAD_DIM))
        av_p.append(v[M_PROMPT - WINDOW:M_PROMPT].reshape(1, WINDOW, N_KV_HEADS, HEAD_DIM))
        pl_p.append(z[M_PROMPT - POOL_STATE:M_PROMPT, Z_U0:Z_U0 + POOL_WIDTH].reshape(1, POOL_STATE, POOL_WIDTH))
        ak_s.append(jnp.concatenate([cache_k, kn_s], axis=1)[:, -WINDOW:]
                    .reshape(DEC_BATCH, WINDOW, N_KV_HEADS, HEAD_DIM))
        av_s.append(jnp.concatenate([cache_v, v_s], axis=1)[:, -WINDOW:]
                    .reshape(DEC_BATCH, WINDOW, N_KV_HEADS, HEAD_DIM))
        u_s = z[M_PROMPT:, Z_U0:Z_U0 + POOL_WIDTH].reshape(DEC_BATCH, DEC_SEQ, POOL_WIDTH)
        pl_s.append(jnp.concatenate([state_pool[l].astype(F32), u_s], axis=1)[:, -POOL_STATE:])

        mk, mv = _mem_kv(mem, _row(g_mem[l]), w_k_b, w_v_b, _row(k_norm_mem[l]), l)
        mk_p.append(mk.reshape(1, N_MEM, MEM_HEADS, MEM_HEAD_DIM))
        mv_p.append(mv.reshape(1, N_MEM, MEM_HEADS, MEM_HEAD_DIM))
        x = _cross(x, _row(g_cross[l]), w_q_b, _row(q_norm_mem[l]),
                   mk.reshape(1, N_MEM, MEM_WIDTH), mv.reshape(1, N_MEM, MEM_WIDTH),
                   cache_mem_k[l].reshape(DEC_BATCH, N_MEM, MEM_WIDTH).astype(F32),
                   cache_mem_v[l].reshape(DEC_BATCH, N_MEM, MEM_WIDTH).astype(F32),
                   w_o_b, l)

        x_s, w_gate_up_b, w_down_b = _ffn_sample(x, _row(g_ffn[l]), w_gate, w_up, w_down, l)
        x_p = _ffn_prompt(x, _row(g_ffn[l]), w_gate_up_b, w_down_b)

    return (x_p.reshape(1, SEQ, D_MODEL), x_s.reshape(DEC_BATCH, DEC_SEQ, D_MODEL),
            jnp.stack(ak_p), jnp.stack(av_p), jnp.stack(pl_p), jnp.stack(mk_p), jnp.stack(mv_p),
            jnp.stack(ak_s), jnp.stack(av_s), jnp.stack(pl_s))
```

```python
import functools

import jax
import jax.numpy as jnp
from jax import lax
from jax.experimental import pallas as pl
from jax.experimental.pallas import tpu as pltpu

F32 = jnp.float32
BF16 = jnp.bfloat16

D_MODEL = 4096
SEQ = 8192
DEPTH = 2
DEC_BATCH = 8
DEC_SEQ = 64
CHUNK = 64
WINDOW = 128
ATTN_WIDTH = D_MODEL // 2
POOL_WIDTH = D_MODEL - ATTN_WIDTH
HEAD_DIM = 64
N_HEADS = ATTN_WIDTH // HEAD_DIM
N_KV_HEADS = N_HEADS // 8
KV_WIDTH = N_KV_HEADS * HEAD_DIM
POOL_WINDOWS = (2, 4, 8, 16)
POOL_GROUP_WIDTH = POOL_WIDTH // len(POOL_WINDOWS)
POOL_STATE = max(POOL_WINDOWS) - 1
IN_WIDTH = ATTN_WIDTH + 2 * KV_WIDTH + POOL_WIDTH
N_MEM = 256
MEM_HEADS = 4
MEM_HEAD_DIM = 128
MEM_WIDTH = MEM_HEADS * MEM_HEAD_DIM
D_FF = -(-8 * D_MODEL // (3 * 256)) * 256
EPS = 1e-6
LOG2_E = 1.4426950408889634
NEG_INF = -1e30

M_PROMPT = SEQ
M_SAMPLE = DEC_BATCH * DEC_SEQ
M_ALL = M_PROMPT + M_SAMPLE
N_CHUNKS_PROMPT = M_PROMPT // CHUNK
BAND = WINDOW + CHUNK
BAND_BLOCKS = BAND // CHUNK

Z_U0 = ATTN_WIDTH
Z_K0 = ATTN_WIDTH + POOL_WIDTH
Z_V0 = Z_K0 + KV_WIDTH

LANES = 128
SUBLANES = 8
BF16_TILE_ROWS = 16
HALO = BF16_TILE_ROWS
VMEM_LIMIT_CAP = 60000 * 1024
CAST_BLOCK_BYTES = 8 << 20

TM = 512
PROMPT_TILES = M_PROMPT // TM
ROW_TILES = M_ALL // TM
TN_IN = 768
TN_OUT = 1024
TF_FFN = 512
NF_FFN = -(-D_FF // TF_FFN)
TF_LAST = D_FF - (NF_FFN - 1) * TF_FFN
TF_SAMPLE = 256
TM_CROSS = 256
TP_POOL = 512
NORM_ROWS = BF16_TILE_ROWS
NORM_UNROLL = 8
W_IN_COL_BLOCK = KV_WIDTH


def _params(vmem_bytes, n_axes):
    limit = min(int(vmem_bytes * 1.2) + (6 << 20), VMEM_LIMIT_CAP)
    return pltpu.CompilerParams(dimension_semantics=("arbitrary",) * n_axes, vmem_limit_bytes=limit)


def _rms_rows(x, g):
    return x * lax.rsqrt(jnp.mean(x * x, axis=-1, keepdims=True) + EPS) * g


def _norm_rows_to(x_ref, g_ref, h_ref, n_rows, copy_ref=None):
    def body(r, carry):
        rows = pl.ds(pl.multiple_of(r * NORM_ROWS, NORM_ROWS), NORM_ROWS)
        x = x_ref[rows, :]
        h_ref[rows, :] = _rms_rows(x, g_ref[...]).astype(BF16)
        if copy_ref is not None:
            copy_ref[rows, :] = x
        return carry

    lax.fori_loop(0, n_rows // NORM_ROWS, body, 0, unroll=NORM_UNROLL)


def _half_lane_rms(x, gain2):
    lo = lax.broadcasted_iota(jnp.int32, (1, LANES), 1) < HEAD_DIM
    x2 = x * x
    ss_lo = jnp.sum(jnp.where(lo, x2, 0.0), axis=-1, keepdims=True)
    ss_hi = jnp.sum(jnp.where(lo, 0.0, x2), axis=-1, keepdims=True)
    r = jnp.where(lo, lax.rsqrt(ss_lo / HEAD_DIM + EPS), lax.rsqrt(ss_hi / HEAD_DIM + EPS))
    return x * r * gain2


def _cast_kernel(w_ref, o_ref):
    o_ref[...] = w_ref[...].astype(BF16)


def _cast_rows(w):
    depth, rows, cols = w.shape
    tr = max(t for t in range(BF16_TILE_ROWS, rows + 1, BF16_TILE_ROWS)
             if rows % t == 0 and t * cols * 4 <= CAST_BLOCK_BYTES)
    spec = pl.BlockSpec((None, tr, cols), lambda l, i: (l, i, 0))
    return pl.pallas_call(
        _cast_kernel, grid=(depth, rows // tr), in_specs=[spec], out_specs=spec,
        out_shape=jax.ShapeDtypeStruct(w.shape, BF16),
        compiler_params=_params(2 * tr * cols * 6, 2), name="cast_rows",
    )(w)


def _w_in_source_block(j):
    q_blocks, u_blocks = ATTN_WIDTH // W_IN_COL_BLOCK, POOL_WIDTH // W_IN_COL_BLOCK
    kv_blocks = 2 * KV_WIDTH // W_IN_COL_BLOCK
    return jnp.where(j < q_blocks, j, jnp.where(j < q_blocks + u_blocks, j + kv_blocks, j - u_blocks))


def _cast_w_in(w):
    depth = w.shape[0]
    return pl.pallas_call(
        _cast_kernel, grid=(depth, IN_WIDTH // W_IN_COL_BLOCK),
        in_specs=[pl.BlockSpec((None, D_MODEL, W_IN_COL_BLOCK), lambda l, j: (l, 0, _w_in_source_block(j)))],
        out_specs=pl.BlockSpec((None, D_MODEL, W_IN_COL_BLOCK), lambda l, j: (l, 0, j)),
        out_shape=jax.ShapeDtypeStruct(w.shape, BF16),
        compiler_params=_params(2 * D_MODEL * W_IN_COL_BLOCK * 6, 2), name="cast_w_in",
    )(w)


K_TILE = Z_K0 // TN_IN
K_TILE_COL0 = Z_K0 - K_TILE * TN_IN


def _mixer_in_kernel(xp_ref, xs_ref, g_ref, kg_ref, w_ref, z_ref, kn_ref, h_ref):
    i, j = pl.program_id(0), pl.program_id(1)

    @pl.when((j == 0) & (i < PROMPT_TILES))
    def _():
        _norm_rows_to(xp_ref, g_ref, h_ref, TM)

    @pl.when((j == 0) & (i >= PROMPT_TILES))
    def _():
        _norm_rows_to(xs_ref, g_ref, h_ref, TM)

    z_ref[...] = jnp.dot(h_ref[...], w_ref[...], preferred_element_type=F32)

    @pl.when(j == K_TILE)
    def _():
        for t in range(KV_WIDTH // LANES):
            src = slice(K_TILE_COL0 + t * LANES, K_TILE_COL0 + (t + 1) * LANES)
            kn_ref[:, t * LANES:(t + 1) * LANES] = _half_lane_rms(z_ref[:, src], kg_ref[...])


def _x_specs(width, col):
    return [
        pl.BlockSpec((TM, width), lambda i, j: (jnp.minimum(i, PROMPT_TILES - 1), col(i, j))),
        pl.BlockSpec((TM, width), lambda i, j: (0, jnp.where(i >= PROMPT_TILES, col(i, j), 0))),
    ]


def _mixer_in(x_p, x_s, g, k_gain2, w, layer):
    assert K_TILE_COL0 + KV_WIDTH <= TN_IN and K_TILE_COL0 % LANES == 0
    vmem = (4 * TM * D_MODEL * 4 + TM * D_MODEL * 2 + 2 * D_MODEL * TN_IN * 2 + 2 * TM * TN_IN * 4
            + 2 * TM * KV_WIDTH * 4)
    return pl.pallas_call(
        _mixer_in_kernel,
        grid=(ROW_TILES, IN_WIDTH // TN_IN),
        in_specs=_x_specs(D_MODEL, lambda i, j: 0) + [
            pl.BlockSpec((1, D_MODEL), lambda i, j: (0, 0)),
            pl.BlockSpec((1, LANES), lambda i, j: (0, 0)),
            pl.BlockSpec((None, D_MODEL, TN_IN), lambda i, j: (layer, 0, j)),
        ],
        out_specs=[pl.BlockSpec((TM, TN_IN), lambda i, j: (i, j)),
                   pl.BlockSpec((TM, KV_WIDTH), lambda i, j: (i, 0))],
        out_shape=[jax.ShapeDtypeStruct((M_ALL, IN_WIDTH), F32), jax.ShapeDtypeStruct((M_ALL, KV_WIDTH), F32)],
        scratch_shapes=[pltpu.VMEM((TM, D_MODEL), BF16)],
        compiler_params=_params(vmem, 2),
        name="mixer_in",
    )(x_p, x_s, g, k_gain2, w)


PAIRS_PER_KV = N_HEADS // N_KV_HEADS // 2
QUERIES_PER_KV = PAIRS_PER_KV * CHUNK
N_PAIRS = N_HEADS // 2
SWA_CHUNKS = 8
Z_V_BLOCK = Z_V0 // KV_WIDTH


def _swa_chunk(q_ref, rows, k_refs, v_refs, qg_ref, bias_ref, table, sink_e_ref, sink_o_ref, o_ref):
    lo = lax.broadcasted_iota(jnp.int32, (1, LANES), 1) < HEAD_DIM

    q = jnp.concatenate([q_ref[rows, p * LANES:(p + 1) * LANES] for p in range(N_PAIRS)], axis=0)

    qn = _half_lane_rms(q, qg_ref[...]).astype(BF16)
    nt = (((1,), (1,)), ((), ()))

    kb = jnp.concatenate([ref[...] for ref in k_refs], axis=0)
    vb = jnp.concatenate([ref[...] for ref in v_refs] + [jnp.zeros((KV_WIDTH - BAND, KV_WIDTH), F32)], axis=0)
    v_t = vb.T

    def head_softmax(s_raw, bias, sink):
        s = s_raw - bias
        m = jnp.maximum(jnp.max(s, axis=0, keepdims=True), sink)
        p = jnp.exp2(s - m)
        return p.astype(BF16), jnp.sum(p, axis=0, keepdims=True) + jnp.exp2(sink - m)

    for kv_tile in range(KV_WIDTH // LANES):
        kt = kb[:, kv_tile * LANES:(kv_tile + 1) * LANES]
        kt_r = pltpu.roll(kt, HEAD_DIM, axis=1)
        for par in range(2):
            hk = 2 * kv_tile + par
            k_e = jnp.where(lo, kt_r if par else kt, 0.0)
            k_o = jnp.where(lo, 0.0, kt if par else kt_r)
            kk = jnp.concatenate([k_e, k_o], axis=0).astype(BF16)
            cols = slice(hk * QUERIES_PER_KV, (hk + 1) * QUERIES_PER_KV)
            s_t = lax.dot_general(kk, qn[cols], nt, preferred_element_type=F32)
            bias = bias_ref[table, hk]
            p_e, d_e = head_softmax(s_t[:BAND], bias[:BAND], sink_e_ref[hk])
            p_o, d_o = head_softmax(s_t[BAND:], bias[BAND:], sink_o_ref[hk])
            v_h = v_t[hk * HEAD_DIM:(hk + 1) * HEAD_DIM, :BAND].astype(BF16)
            o_e = jnp.dot(v_h, p_e, preferred_element_type=F32) / d_e
            o_o = jnp.dot(v_h, p_o, preferred_element_type=F32) / d_o
            o = jnp.concatenate([o_e, o_o], axis=0).T
            for pr in range(PAIRS_PER_KV):
                pair = hk * PAIRS_PER_KV + pr
                o_ref[rows, pair * LANES:(pair + 1) * LANES] = o[pr * CHUNK:(pr + 1) * CHUNK].astype(BF16)


def _swa_kernel(q_ref, *refs, prompt):
    n_band = SWA_CHUNKS * BAND_BLOCKS
    k_refs, v_refs = refs[:n_band], refs[n_band:2 * n_band]
    qg_ref, bias_ref, sink_e_ref, sink_o_ref, o_ref = refs[2 * n_band:]
    for ci in range(SWA_CHUNKS):
        table = jnp.minimum(pl.program_id(0) * SWA_CHUNKS + ci, BAND_BLOCKS - 1) if prompt else 0
        band = slice(ci * BAND_BLOCKS, (ci + 1) * BAND_BLOCKS)
        _swa_chunk(q_ref, slice(ci * CHUNK, (ci + 1) * CHUNK), k_refs[band], v_refs[band],
                   qg_ref, bias_ref, table, sink_e_ref, sink_o_ref, o_ref)


def _swa_call(prompt, n_rows, q_block0, band_k, band_v, operands, q_gain2, bias, sink_e, sink_o):
    step_rows = SWA_CHUNKS * CHUNK
    const = lambda a: pl.BlockSpec(a.shape, lambda s: (0,) * a.ndim)
    vmem = (4 * step_rows * ATTN_WIDTH * 4 + 8 * SWA_CHUNKS * BAND * KV_WIDTH * 4 + 2 * bias.size * 4
            + SWA_CHUNKS * (12 << 20))
    return pl.pallas_call(
        functools.partial(_swa_kernel, prompt=prompt),
        grid=(n_rows // step_rows,),
        in_specs=[pl.BlockSpec((step_rows, ATTN_WIDTH), lambda s: (q_block0 + s, 0))] + band_k + band_v + [
            pl.BlockSpec((1, LANES), lambda s: (0, 0)), const(bias), const(sink_e), const(sink_o)],
        out_specs=pl.BlockSpec((step_rows, ATTN_WIDTH), lambda s: (s, 0)),
        out_shape=jax.ShapeDtypeStruct((n_rows, ATTN_WIDTH), BF16),
        compiler_params=_params(vmem, 1),
        name="swa_prompt" if prompt else "swa_sample",
    )(*operands, q_gain2, bias, sink_e, sink_o)


def _swa_prompt(z, kn, q_gain2, bias, sink_e, sink_o):
    def spec(ci, j, col):
        return pl.BlockSpec((CHUNK, KV_WIDTH),
                            lambda s: (jnp.maximum(s * SWA_CHUNKS + ci - (BAND_BLOCKS - 1) + j, 0), col))
    slots = [(ci, j) for ci in range(SWA_CHUNKS) for j in range(BAND_BLOCKS)]
    band_k = [spec(ci, j, 0) for ci, j in slots]
    band_v = [spec(ci, j, Z_V_BLOCK) for ci, j in slots]
    operands = [z] + [kn] * len(slots) + [z] * len(slots)
    return _swa_call(True, M_PROMPT, 0, band_k, band_v, operands, q_gain2, bias, sink_e, sink_o)


def _swa_sample(z, kn, cache_k, cache_v, q_gain2, bias, sink_e, sink_o):
    cache_blocks = WINDOW // CHUNK
    first_chunk = N_CHUNKS_PROMPT

    def specs(new_col):
        out = []
        for ci in range(SWA_CHUNKS):
            out += [pl.BlockSpec((None, CHUNK, KV_WIDTH), functools.partial(lambda s, ci, j: (s * SWA_CHUNKS + ci, j, 0),
                                                                            ci=ci, j=j))
                    for j in range(cache_blocks)]
            out.append(pl.BlockSpec((CHUNK, KV_WIDTH),
                                    functools.partial(lambda s, ci: (first_chunk + s * SWA_CHUNKS + ci, new_col), ci=ci)))
        return out

    operands = [z] + ([cache_k] * cache_blocks + [kn]) * SWA_CHUNKS + ([cache_v] * cache_blocks + [z]) * SWA_CHUNKS
    return _swa_call(False, M_SAMPLE, M_PROMPT // (SWA_CHUNKS * CHUNK), specs(0), specs(Z_V_BLOCK), operands,
                     q_gain2, bias[BAND_BLOCKS - 1:], sink_e, sink_o)


def _swa_bias_tables():
    slopes = 2.0 ** (-8.0 * jnp.arange(1, N_HEADS + 1, dtype=F32) / N_HEADS)
    key = jnp.arange(BAND)
    dist = jnp.abs(jnp.arange(CHUNK)[None, :] + WINDOW - key[:, None]).astype(F32)
    head = (jnp.arange(N_KV_HEADS)[:, None, None] * (2 * PAIRS_PER_KV)
            + jnp.arange(2)[None, :, None] + 2 * jnp.arange(PAIRS_PER_KV)[None, None, :])
    bias = slopes[head][:, :, None, :, None] * dist[None, None, :, None, :]
    first_valid = jnp.maximum(WINDOW - CHUNK * jnp.arange(BAND_BLOCKS), 0)
    valid = key[None, :] >= first_valid[:, None]
    tables = jnp.where(valid[:, None, None, :, None, None], bias[None] * LOG2_E, -NEG_INF)
    return tables.reshape(BAND_BLOCKS, N_KV_HEADS, 2 * BAND, QUERIES_PER_KV)


def _swa_sink_rows(sinks):
    per_pair = sinks.astype(F32).reshape(N_KV_HEADS, PAIRS_PER_KV, 2) * LOG2_E
    rows = jnp.repeat(per_pair, CHUNK, axis=1)
    return rows[:, None, :, 0], rows[:, None, :, 1]


POOL_TILES_PROMPT = M_PROMPT // TP_POOL
BATCH_PER_POOL_TILE = TP_POOL // DEC_SEQ


def _trailing_sum(ext, w):
    s, k = ext, 1
    while k < w:
        s = s + pltpu.roll(s, k, axis=0)
        k *= 2
    return s


def _pool_kernel(u_ref, halo_ref, state_ref, w_ref, scale_ref, o_ref, d_ref):
    i = pl.program_id(0)

    def window_diffs(ext, cur, cnt, w):
        return _trailing_sum(ext, w)[HALO:] / cnt - cur

    @pl.when(i < POOL_TILES_PROMPT)
    def _():
        pos = i * TP_POOL + lax.broadcasted_iota(jnp.int32, (TP_POOL, 1), 0)
        for g, w in enumerate(POOL_WINDOWS):
            cols = slice(g * POOL_GROUP_WIDTH, (g + 1) * POOL_GROUP_WIDTH)
            cur = u_ref[:, cols]
            halo = jnp.where(i > 0, halo_ref[:, cols], 0.0)
            ext = jnp.concatenate([halo, cur], axis=0)
            cnt = jnp.minimum(pos + 1, w).astype(F32)
            d_ref[:, cols] = window_diffs(ext, cur, cnt, w)

    @pl.when(i >= POOL_TILES_PROMPT)
    def _():
        for g, w in enumerate(POOL_WINDOWS):
            cols = slice(g * POOL_GROUP_WIDTH, (g + 1) * POOL_GROUP_WIDTH)
            for b in range(BATCH_PER_POOL_TILE):
                rows = slice(b * DEC_SEQ, (b + 1) * DEC_SEQ)
                cur = u_ref[rows, cols]
                ext = jnp.concatenate([state_ref[b, :, cols], cur], axis=0)
                d_ref[rows, cols] = window_diffs(ext, cur, float(w), w)

    for g in range(len(POOL_WINDOWS)):
        cols = slice(g * POOL_GROUP_WIDTH, (g + 1) * POOL_GROUP_WIDTH)
        y = jnp.dot(d_ref[:, cols].astype(BF16), w_ref[cols, :], preferred_element_type=F32)
        o_ref[:, cols] = (y * scale_ref[:, cols]).astype(BF16)


def _pool(z, state_halo, w_pool, scale, layer):
    halo_blocks_per_tile = TP_POOL // HALO
    vmem = (2 * TP_POOL * POOL_WIDTH * 4 + 2 * HALO * POOL_WIDTH * 4 + 2 * BATCH_PER_POOL_TILE * HALO * POOL_WIDTH * 4
            + 2 * POOL_WIDTH * POOL_GROUP_WIDTH * 2 + 2 * TP_POOL * POOL_WIDTH * 2 + TP_POOL * POOL_WIDTH * 4
            + (8 << 20))
    return pl.pallas_call(
        _pool_kernel,
        grid=(M_ALL // TP_POOL,),
        in_specs=[
            pl.BlockSpec((TP_POOL, POOL_WIDTH), lambda i: (i, Z_U0 // POOL_WIDTH)),
            pl.BlockSpec((HALO, POOL_WIDTH),
                         lambda i: (jnp.maximum(i * halo_blocks_per_tile - 1, 0), Z_U0 // POOL_WIDTH)),
            pl.BlockSpec((BATCH_PER_POOL_TILE, HALO, POOL_WIDTH),
                         lambda i: (jnp.maximum(i - POOL_TILES_PROMPT, 0), 0, 0)),
            pl.BlockSpec((None, POOL_WIDTH, POOL_GROUP_WIDTH), lambda i: (layer, 0, 0)),
            pl.BlockSpec((1, POOL_WIDTH), lambda i: (0, 0)),
        ],
        out_specs=pl.BlockSpec((TP_POOL, POOL_WIDTH), lambda i: (i, 0)),
        out_shape=jax.ShapeDtypeStruct((M_ALL, POOL_WIDTH), BF16),
        scratch_shapes=[pltpu.VMEM((TP_POOL, POOL_WIDTH), F32)],
        compiler_params=_params(vmem, 1),
        name="pool",
    )(z, z, state_halo, w_pool, scale)


def _mixer_out_kernel(ap_ref, as_ref, p_ref, xp_ref, xs_ref, wa_ref, wp_ref, o_ref):
    prompt = pl.program_id(0) < PROMPT_TILES
    a = jnp.where(prompt, ap_ref[...], as_ref[...])
    y = jnp.dot(a, wa_ref[...], preferred_element_type=F32)
    y = y + jnp.dot(p_ref[...], wp_ref[...], preferred_element_type=F32)
    o_ref[...] = jnp.where(prompt, xp_ref[...], xs_ref[...]) + y


def _mixer_out(attn_p, attn_s, pooled, x_p, x_s, w_out, layer):
    vmem = 2 * (3 * TM * ATTN_WIDTH * 2 + D_MODEL * TN_OUT * 2 + 3 * TM * TN_OUT * 4) + TM * ATTN_WIDTH * 2
    return pl.pallas_call(
        _mixer_out_kernel,
        grid=(ROW_TILES, D_MODEL // TN_OUT),
        in_specs=_x_specs(ATTN_WIDTH, lambda i, j: 0) + [
            pl.BlockSpec((TM, POOL_WIDTH), lambda i, j: (i, 0)),
        ] + _x_specs(TN_OUT, lambda i, j: j) + [
            pl.BlockSpec((None, ATTN_WIDTH, TN_OUT), lambda i, j: (layer, 0, j)),
            pl.BlockSpec((None, POOL_WIDTH, TN_OUT), lambda i, j: (layer, 1, j)),
        ],
        out_specs=pl.BlockSpec((TM, TN_OUT), lambda i, j: (i, j)),
        out_shape=jax.ShapeDtypeStruct((M_ALL, D_MODEL), F32),
        compiler_params=_params(vmem, 2),
        name="mixer_out",
    )(attn_p, attn_s, pooled, x_p, x_s, w_out, w_out)


def _mem_kv_kernel(m_ref, g_ref, wk_ref, wv_ref, kg_ref, k_ref, v_ref, h_ref):
    _norm_rows_to(m_ref, g_ref, h_ref, N_MEM)
    h = h_ref[...]
    k = jnp.dot(h, wk_ref[...], preferred_element_type=F32)
    for hd in range(MEM_HEADS):
        cols = slice(hd * MEM_HEAD_DIM, (hd + 1) * MEM_HEAD_DIM)
        k_ref[:, cols] = _rms_rows(k[:, cols], kg_ref[...])
    v_ref[...] = jnp.dot(h, wv_ref[...], preferred_element_type=F32)


def _mem_kv(mem, g, w_k, w_v, k_gain, layer):
    vmem = 2 * (N_MEM * D_MODEL * 4 + 2 * D_MODEL * MEM_WIDTH * 2 + 2 * N_MEM * MEM_WIDTH * 4) + N_MEM * D_MODEL * 2
    full = lambda shape: pl.BlockSpec(shape, lambda i: (0,) * len(shape))
    weight = pl.BlockSpec((None, D_MODEL, MEM_WIDTH), lambda i: (layer, 0, 0))
    return pl.pallas_call(
        _mem_kv_kernel,
        grid=(1,),
        in_specs=[full((N_MEM, D_MODEL)), full((1, D_MODEL)), weight, weight, full((1, MEM_HEAD_DIM))],
        out_specs=[full((N_MEM, MEM_WIDTH)), full((N_MEM, MEM_WIDTH))],
        out_shape=[jax.ShapeDtypeStruct((N_MEM, MEM_WIDTH), F32)] * 2,
        scratch_shapes=[pltpu.VMEM((N_MEM, D_MODEL), BF16)],
        compiler_params=_params(vmem, 1),
        name="mem_kv",
    )(mem, g, w_k, w_v, k_gain)


CROSS_TILES_PROMPT = M_PROMPT // TM_CROSS
BATCH_PER_CROSS_TILE = TM_CROSS // DEC_SEQ


def _cross_heads(q, mk_ref, mv_ref, b, qg, oc_ref, rows):
    for hd in range(MEM_HEADS):
        cols = slice(hd * MEM_HEAD_DIM, (hd + 1) * MEM_HEAD_DIM)
        qn = _rms_rows(q[:, cols], qg).astype(BF16)
        s = lax.dot_general(qn, mk_ref[b, :, cols].astype(BF16), (((1,), (1,)), ((), ())),
                            preferred_element_type=F32) * (MEM_HEAD_DIM ** -0.5)
        p = jnp.exp(s - jnp.max(s, axis=-1, keepdims=True))
        den = jnp.sum(p, axis=-1, keepdims=True)
        o = jnp.dot(p.astype(BF16), mv_ref[b, :, cols].astype(BF16), preferred_element_type=F32) / den
        oc_ref[rows, cols] = o.astype(BF16)


def _cross_kernel(x_ref, g_ref, wq_ref, qg_ref, mkp_ref, mvp_ref, mks_ref, mvs_ref, wo_ref,
                  o_ref, h_ref, q_ref, oc_ref):
    i = pl.program_id(0)
    _norm_rows_to(x_ref, g_ref, h_ref, TM_CROSS)
    q_ref[...] = jnp.dot(h_ref[...], wq_ref[...], preferred_element_type=F32)

    @pl.when(i < CROSS_TILES_PROMPT)
    def _():
        _cross_heads(q_ref[...], mkp_ref, mvp_ref, 0, qg_ref[...], oc_ref, slice(0, TM_CROSS))

    @pl.when(i >= CROSS_TILES_PROMPT)
    def _():
        for b in range(BATCH_PER_CROSS_TILE):
            rows = slice(b * DEC_SEQ, (b + 1) * DEC_SEQ)
            _cross_heads(q_ref[rows, :], mks_ref, mvs_ref, b, qg_ref[...], oc_ref, rows)

    o_ref[...] = x_ref[...] + jnp.dot(oc_ref[...], wo_ref[...], preferred_element_type=F32)


def _cross(x, g, w_q, q_gain, mk_p, mv_p, mk_s, mv_s, w_o, layer):
    const = lambda shape: pl.BlockSpec(shape, lambda i: (0,) * len(shape))
    once = lambda shape: pl.BlockSpec((None,) + shape, lambda i: (layer, 0, 0), pipeline_mode=pl.Buffered(1))
    sample_mem = pl.BlockSpec((BATCH_PER_CROSS_TILE, N_MEM, MEM_WIDTH),
                              lambda i: (jnp.maximum(i - CROSS_TILES_PROMPT, 0), 0, 0))
    vmem = (4 * TM_CROSS * D_MODEL * 4 + 2 * D_MODEL * MEM_WIDTH * 2
            + 4 * (1 + BATCH_PER_CROSS_TILE) * N_MEM * MEM_WIDTH * 4
            + TM_CROSS * D_MODEL * 2 + TM_CROSS * MEM_WIDTH * 6 + (6 << 20))
    return pl.pallas_call(
        _cross_kernel,
        grid=(M_ALL // TM_CROSS,),
        in_specs=[
            pl.BlockSpec((TM_CROSS, D_MODEL), lambda i: (i, 0)),
            const((1, D_MODEL)),
            once((D_MODEL, MEM_WIDTH)),
            const((1, MEM_HEAD_DIM)),
            const(mk_p.shape), const(mv_p.shape), sample_mem, sample_mem,
            once((MEM_WIDTH, D_MODEL)),
        ],
        out_specs=pl.BlockSpec((TM_CROSS, D_MODEL), lambda i: (i, 0)),
        out_shape=jax.ShapeDtypeStruct((M_ALL, D_MODEL), F32),
        scratch_shapes=[pltpu.VMEM((TM_CROSS, D_MODEL), BF16), pltpu.VMEM((TM_CROSS, MEM_WIDTH), F32),
                        pltpu.VMEM((TM_CROSS, MEM_WIDTH), BF16)],
        compiler_params=_params(vmem, 1),
        name="cross",
    )(x, g, w_q, q_gain, mk_p, mv_p, mk_s, mv_s, w_o)


GROUP = TF_SAMPLE
GROUPS_PER_TILE = TF_FFN // GROUP
GROUPS_LAST = TF_LAST // GROUP
N_GROUPS = D_FF // GROUP


def _swiglu_groups(gate_up, n_groups):
    acts = []
    for k in range(n_groups):
        gate = gate_up[:, 2 * k * GROUP:(2 * k + 1) * GROUP]
        up = gate_up[:, (2 * k + 1) * GROUP:(2 * k + 2) * GROUP]
        acts.append(gate * jax.nn.sigmoid(gate) * up)
    return (acts[0] if n_groups == 1 else jnp.concatenate(acts, axis=1)).astype(BF16)


def _ffn_prompt_kernel(x_ref, g_ref, wgu_ref, wd_ref, o_ref, h_ref):
    f = pl.program_id(1)

    @pl.when(f == 0)
    def _():
        _norm_rows_to(x_ref, g_ref, h_ref, TM)

    def accumulate(n_groups, first):
        gate_up = jnp.dot(h_ref[...], wgu_ref[:, :2 * n_groups * GROUP], preferred_element_type=F32)
        act = _swiglu_groups(gate_up, n_groups)
        base = x_ref if first else o_ref
        o_ref[...] = base[...] + jnp.dot(act, wd_ref[:n_groups * GROUP, :], preferred_element_type=F32)

    pl.when(f == 0)(functools.partial(accumulate, GROUPS_PER_TILE, True))
    pl.when((f > 0) & (f < NF_FFN - 1))(functools.partial(accumulate, GROUPS_PER_TILE, False))
    pl.when(f == NF_FFN - 1)(functools.partial(accumulate, GROUPS_LAST, False))


def _ffn_prompt(x, g, w_gate_up, w_down):
    vmem = 3 * TM * D_MODEL * 4 + TM * D_MODEL * 2 + 2 * 3 * D_MODEL * TF_FFN * 2 + 4 * TM * TF_FFN * 4
    return pl.pallas_call(
        _ffn_prompt_kernel,
        grid=(PROMPT_TILES, NF_FFN),
        in_specs=[
            pl.BlockSpec((TM, D_MODEL), lambda i, f: (i, 0)),
            pl.BlockSpec((1, D_MODEL), lambda i, f: (0, 0)),
            pl.BlockSpec((D_MODEL, 2 * TF_FFN), lambda i, f: (0, f)),
            pl.BlockSpec((TF_FFN, D_MODEL), lambda i, f: (f, 0)),
        ],
        out_specs=pl.BlockSpec((TM, D_MODEL), lambda i, f: (i, 0), pipeline_mode=pl.Buffered(1)),
        out_shape=jax.ShapeDtypeStruct((M_PROMPT, D_MODEL), F32),
        scratch_shapes=[pltpu.VMEM((TM, D_MODEL), BF16)],
        compiler_params=_params(vmem, 2),
        name="ffn_prompt",
    )(x, g, w_gate_up, w_down)


def _ffn_sample_kernel(x_ref, g_ref, wg_ref, wu_ref, wd_ref, o_ref, wgu_out, wd_out, h_ref):
    @pl.when(pl.program_id(0) == 0)
    def _():
        _norm_rows_to(x_ref, g_ref, h_ref, TM, copy_ref=o_ref)

    wgu_out[:, :GROUP] = wg_ref[...].astype(BF16)
    wgu_out[:, GROUP:] = wu_ref[...].astype(BF16)
    wd_out[...] = wd_ref[...].astype(BF16)
    act = _swiglu_groups(jnp.dot(h_ref[...], wgu_out[...], preferred_element_type=F32), 1)
    o_ref[...] += jnp.dot(act, wd_out[...], preferred_element_type=F32)


def _ffn_sample(x, g, w_gate, w_up, w_down, layer):
    vmem = 2 * TM * D_MODEL * 4 + TM * D_MODEL * 2 + 2 * 3 * D_MODEL * GROUP * 6 + 4 * TM * GROUP * 4
    once = pl.Buffered(1)
    col_f32 = pl.BlockSpec((None, D_MODEL, GROUP), lambda f: (layer, 0, f))
    return pl.pallas_call(
        _ffn_sample_kernel,
        grid=(N_GROUPS,),
        in_specs=[
            pl.BlockSpec((TM, D_MODEL), lambda f: (PROMPT_TILES, 0), pipeline_mode=once),
            pl.BlockSpec((1, D_MODEL), lambda f: (0, 0)),
            col_f32, col_f32,
            pl.BlockSpec((None, GROUP, D_MODEL), lambda f: (layer, f, 0)),
        ],
        out_specs=[
            pl.BlockSpec((TM, D_MODEL), lambda f: (0, 0), pipeline_mode=once),
            pl.BlockSpec((D_MODEL, 2 * GROUP), lambda f: (0, f)),
            pl.BlockSpec((GROUP, D_MODEL), lambda f: (f, 0)),
        ],
        out_shape=[jax.ShapeDtypeStruct((M_SAMPLE, D_MODEL), F32),
                   jax.ShapeDtypeStruct((D_MODEL, 2 * D_FF), BF16),
                   jax.ShapeDtypeStruct((D_FF, D_MODEL), BF16)],
        scratch_shapes=[pltpu.VMEM((TM, D_MODEL), BF16)],
        compiler_params=_params(vmem, 1),
        name="ffn_sample",
    )(x, g, w_gate, w_up, w_down)


def _row(v):
    return v.astype(F32).reshape(1, -1)


def kernel(x_prompt, x_sample, cache_attn_k, cache_attn_v, state_pool, cache_mem_k, cache_mem_v, mem_prompt,
           g_mix, w_in, q_norm, k_norm, attn_sinks, w_pool, pool_scale, w_out,
           g_cross, g_mem, w_q_mem, w_k_mem, w_v_mem, q_norm_mem, k_norm_mem, w_o_mem,
           g_ffn, w_gate, w_up, w_down):
    x_p, x_s = x_prompt.reshape(M_PROMPT, D_MODEL), x_sample.reshape(M_SAMPLE, D_MODEL)
    mem = mem_prompt.reshape(N_MEM, D_MODEL)

    w_in_b = _cast_w_in(w_in)
    w_out_b, w_q_b, w_k_b, w_v_b, w_o_b = (_cast_rows(w) for w in (w_out, w_q_mem, w_k_mem, w_v_mem, w_o_mem))
    w_pool_b = _cast_rows(w_pool.reshape(DEPTH, POOL_WIDTH, POOL_GROUP_WIDTH))
    bias = _swa_bias_tables()

    ak_p, av_p, pl_p, mk_p, mv_p, ak_s, av_s, pl_s = ([] for _ in range(8))
    for l in range(DEPTH):
        z, kn = _mixer_in(x_p, x_s, _row(g_mix[l]), _row(jnp.tile(k_norm[l], 2)), w_in_b, l)
        cache_k = cache_attn_k[l].astype(F32).reshape(DEC_BATCH, WINDOW, KV_WIDTH)
        cache_v = cache_attn_v[l].astype(F32).reshape(DEC_BATCH, WINDOW, KV_WIDTH)
        sink_e, sink_o = _swa_sink_rows(attn_sinks[l])
        q_gain2 = _row(jnp.tile(q_norm[l], 2)) * (HEAD_DIM ** -0.5 * LOG2_E)
        attn_p = _swa_prompt(z, kn, q_gain2, bias, sink_e, sink_o)
        attn_s = _swa_sample(z, kn, cache_k, cache_v, q_gain2, bias, sink_e, sink_o)

        state_halo = jnp.pad(state_pool[l].astype(F32), ((0, 0), (HALO - POOL_STATE, 0), (0, 0)))
        pooled = _pool(z, state_halo, w_pool_b, _row(pool_scale[l]), l)
        x = _mixer_out(attn_p, attn_s, pooled, x_p, x_s, w_out_b, l)

        v = z[:, Z_V0:Z_V0 + KV_WIDTH]
        kn_s = kn[M_PROMPT:].reshape(DEC_BATCH, DEC_SEQ, KV_WIDTH)
        v_s = v[M_PROMPT:].reshape(DEC_BATCH, DEC_SEQ, KV_WIDTH)
        ak_p.append(kn[M_PROMPT - WINDOW:M_PROMPT].reshape(1, WINDOW, N_KV_HEADS, HEAD_DIM))
        av_p.append(v[M_PROMPT - WINDOW:M_PROMPT].reshape(1, WINDOW, N_KV_HEADS, HEAD_DIM))
        pl_p.append(z[M_PROMPT - POOL_STATE:M_PROMPT, Z_U0:Z_U0 + POOL_WIDTH].reshape(1, POOL_STATE, POOL_WIDTH))
        ak_s.append(jnp.concatenate([cache_k, kn_s], axis=1)[:, -WINDOW:]
                    .reshape(DEC_BATCH, WINDOW, N_KV_HEADS, HEAD_DIM))
        av_s.append(jnp.concatenate([cache_v, v_s], axis=1)[:, -WINDOW:]
                    .reshape(DEC_BATCH, WINDOW, N_KV_HEADS, HEAD_DIM))
        u_s = z[M_PROMPT:, Z_U0:Z_U0 + POOL_WIDTH].reshape(DEC_BATCH, DEC_SEQ, POOL_WIDTH)
        pl_s.append(jnp.concatenate([state_pool[l].astype(F32), u_s], axis=1)[:, -POOL_STATE:])

        mk, mv = _mem_kv(mem, _row(g_mem[l]), w_k_b, w_v_b, _row(k_norm_mem[l]), l)
        mk_p.append(mk.reshape(1, N_MEM, MEM_HEADS, MEM_HEAD_DIM))
        mv_p.append(mv.reshape(1, N_MEM, MEM_HEADS, MEM_HEAD_DIM))
        x = _cross(x, _row(g_cross[l]), w_q_b, _row(q_norm_mem[l]),
                   mk.reshape(1, N_MEM, MEM_WIDTH), mv.reshape(1, N_MEM, MEM_WIDTH),
                   cache_mem_k[l].reshape(DEC_BATCH, N_MEM, MEM_WIDTH).astype(F32),
                   cache_mem_v[l].reshape(DEC_BATCH, N_MEM, MEM_WIDTH).astype(F32),
                   w_o_b, l)

        x_s, w_gate_up_b, w_down_b = _ffn_sample(x, _row(g_ffn[l]), w_gate, w_up, w_down, l)
        x_p = _ffn_prompt(x, _row(g_ffn[l]), w_gate_up_b, w_down_b)

    return (x_p.reshape(1, SEQ, D_MODEL), x_s.reshape(DEC_BATCH, DEC_SEQ, D_MODEL),
            jnp.stack(ak_p), jnp.stack(av_p), jnp.stack(pl_p), jnp.stack(mk_p), jnp.stack(mv_p),
            jnp.stack(ak_s), jnp.stack(av_s), jnp.stack(pl_s))
```
